```python
import math
import jax, jax.numpy as jnp
from jax import lax
import numpy as np


D_MODEL = 1024
BATCH = 8
SEQ = 2048
DEPTH = 4

N_A_LAYERS = DEPTH // 2
N_B_LAYERS = DEPTH - N_A_LAYERS
POOL_WINDOWS = (2, 4, 8, 16)
N_POOL_GROUPS = len(POOL_WINDOWS)
POOL_GROUP = D_MODEL // N_POOL_GROUPS
HEAD_DIM = 64
N_HEADS = D_MODEL // HEAD_DIM
MOBA_BLOCK = 256
MOBA_TOPK = 3
Q_BLOCK = 128
NUM_BUCKETS = 32
MAX_DISTANCE = 128
D_FF = -(-8 * D_MODEL // (3 * 256)) * 256
EPS = 1e-6

kernel_name = 'yoco_pool_moba_hybrid'


def rmsnorm(x, g):
    xf = x.astype(jnp.float32)
    y = xf * lax.rsqrt(jnp.mean(xf * xf, axis=-1, keepdims=True) + EPS)
    return (y * g.astype(jnp.float32)).astype(x.dtype)


def pool_mixer(h, w, scale):
    B, T, D = h.shape
    hf = h.astype(jnp.float32)
    csum = jnp.concatenate([jnp.zeros((B, 1, D), jnp.float32), jnp.cumsum(hf, axis=1)], axis=1)
    t = jnp.arange(T)
    outs = []
    for g, win in enumerate(POOL_WINDOWS):
        sl = slice(g * POOL_GROUP, (g + 1) * POOL_GROUP)
        start = jnp.maximum(t + 1 - win, 0)
        cnt = (t + 1 - start).astype(jnp.float32)
        window_sum = csum[:, 1:, sl] - jnp.take(csum[:, :, sl], start, axis=1)
        outs.append(window_sum / cnt[None, :, None] - hf[:, :, sl])
    d = jnp.stack(outs, axis=2).astype(h.dtype)
    y = jnp.einsum('btgc,gce->btge', d, w).reshape(B, T, D)
    return y * scale


def swiglu(h, w_gate_up, w_down):
    gu = h @ w_gate_up
    gate, up = jnp.split(gu, 2, axis=-1)
    return (jax.nn.silu(gate) * up) @ w_down


def rel_bucket(n):
    max_exact = NUM_BUCKETS // 2
    nf = jnp.maximum(n, max_exact).astype(jnp.float32)
    large = max_exact + (jnp.log(nf / max_exact) / math.log(MAX_DISTANCE / max_exact)
                         * (NUM_BUCKETS - max_exact)).astype(jnp.int32)
    large = jnp.minimum(large, NUM_BUCKETS - 1)
    return jnp.where(n < max_exact, n, large)


def shared_kv(x, g, w_kv):
    B, T, D = x.shape
    h = rmsnorm(x, g)
    kv = h @ w_kv
    k, v = jnp.split(kv, 2, axis=-1)
    n_blocks = -(-T // MOBA_BLOCK)
    pad = n_blocks * MOBA_BLOCK - T

    def to_blocks(a):
        a = a.reshape(B, T, N_HEADS, HEAD_DIM).transpose(0, 2, 1, 3)
        a = jnp.pad(a, ((0, 0), (0, 0), (0, pad), (0, 0)))
        return a.reshape(B, N_HEADS, n_blocks, MOBA_BLOCK, HEAD_DIM)

    k_blocks = to_blocks(k)
    v_blocks = to_blocks(v)
    k_mean = jnp.mean(k_blocks.astype(jnp.float32), axis=3).astype(k_blocks.dtype)
    return k_blocks, v_blocks, k_mean


def moba_attention(q, k_blocks, v_blocks, k_mean, rel_bias):
    B, T = q.shape[0], q.shape[1]
    n_q = T // Q_BLOCK
    n_blocks = k_blocks.shape[2]
    topk = min(MOBA_TOPK, n_blocks)
    scale = HEAD_DIM ** -0.5
    h_idx = jnp.arange(N_HEADS)[:, None, None]
    bias_t = rel_bias.T.astype(jnp.float32)
    q_blocks = q.transpose(0, 2, 1, 3).reshape(B, N_HEADS, n_q, Q_BLOCK, HEAD_DIM)
    q_blocks = q_blocks.transpose(0, 2, 1, 3, 4).reshape(B * n_q, N_HEADS, Q_BLOCK, HEAD_DIM)
    b_idx = jnp.repeat(jnp.arange(B), n_q)
    qb_idx = jnp.tile(jnp.arange(n_q), B)
    j_blk = jnp.arange(MOBA_BLOCK)

    def step(args):
        qblk, b, qb = args
        kb = k_blocks[b]
        vb = v_blocks[b]
        km = k_mean[b]
        qs = qb * Q_BLOCK
        own = qs // MOBA_BLOCK
        qpos = qs + jnp.arange(Q_BLOCK)
        gate = jnp.einsum('hqd,hnd->hqn', qblk, km).astype(jnp.float32)
        gate = jnp.where(jnp.arange(n_blocks)[None, None, :] < own, gate, -jnp.inf)
        _, sel = lax.top_k(gate, topk)
        sel_valid = jnp.arange(topk) < jnp.minimum(own, topk)
        k_sel = kb[h_idx, sel]
        v_sel = vb[h_idx, sel]
        s_sel = jnp.einsum('hqd,hqkjd->hqkj', qblk, k_sel).astype(jnp.float32) * scale
        kpos_sel = sel[..., None] * MOBA_BLOCK + j_blk
        k_own = lax.dynamic_index_in_dim(kb, own, axis=1, keepdims=False)
        v_own = lax.dynamic_index_in_dim(vb, own, axis=1, keepdims=False)
        s_own = jnp.einsum('hqd,hjd->hqj', qblk, k_own).astype(jnp.float32) * scale
        kpos_own = own * MOBA_BLOCK + j_blk
        L_sel = topk * MOBA_BLOCK
        s = jnp.concatenate([s_sel.reshape(N_HEADS, Q_BLOCK, L_sel), s_own], axis=-1)
        kpos = jnp.concatenate([kpos_sel.reshape(N_HEADS, Q_BLOCK, L_sel),
                                jnp.broadcast_to(kpos_own[None, None, :], (N_HEADS, Q_BLOCK, MOBA_BLOCK))], axis=-1)
        valid_sel = jnp.broadcast_to(sel_valid[None, None, :, None], (N_HEADS, Q_BLOCK, topk, MOBA_BLOCK)).reshape(N_HEADS, Q_BLOCK, L_sel)
        valid_own = jnp.broadcast_to(kpos_own[None, None, :] <= qpos[None, :, None], (N_HEADS, Q_BLOCK, MOBA_BLOCK))
        valid = jnp.concatenate([valid_sel, valid_own], axis=-1)
        bucket = rel_bucket(jnp.maximum(qpos[None, :, None] - kpos, 0))
        bias = bias_t[h_idx, bucket]
        s = jnp.where(valid, s + bias, -jnp.inf)
        p = jax.nn.softmax(s, axis=-1).astype(vb.dtype)
        p_sel = p[..., :L_sel].reshape(N_HEADS, Q_BLOCK, topk, MOBA_BLOCK)
        p_own = p[..., L_sel:]
        return (jnp.einsum('hqkj,hqkjd->hqd', p_sel, v_sel)
                + jnp.einsum('hqj,hjd->hqd', p_own, v_own))

    o = lax.map(step, (q_blocks, b_idx, qb_idx))
    o = o.reshape(B, n_q, N_HEADS, Q_BLOCK, HEAD_DIM).transpose(0, 1, 3, 2, 4)
    return o.reshape(B, T, D_MODEL)


def setup_inputs(seed: int = 0) -> dict:
    key = jax.random.key(seed)
    ks = jax.random.split(key, 14)
    f32 = jnp.float32
    nrm = lambda k, shape: jax.random.normal(k, shape, f32)
    x = nrm(ks[0], (BATCH, SEQ, D_MODEL))
    norm_mixer = 1.0 + 0.05 * nrm(ks[1], (DEPTH, D_MODEL))
    norm_ffn = 1.0 + 0.05 * nrm(ks[2], (DEPTH, D_MODEL))
    pool_w = nrm(ks[3], (N_A_LAYERS, N_POOL_GROUPS, POOL_GROUP, POOL_GROUP)) * POOL_GROUP ** -0.5
    pool_scale = 1.0 + 0.05 * nrm(ks[4], (N_A_LAYERS, D_MODEL))
    kv_norm = 1.0 + 0.05 * nrm(ks[5], (D_MODEL,))
    w_kv = nrm(ks[6], (D_MODEL, 2 * D_MODEL)) * D_MODEL ** -0.5
    w_q = nrm(ks[7], (N_B_LAYERS, D_MODEL, D_MODEL)) * D_MODEL ** -0.5
    w_o = nrm(ks[8], (N_B_LAYERS, D_MODEL, D_MODEL)) * D_MODEL ** -0.5
    rel_bias = 0.5 * nrm(ks[9], (NUM_BUCKETS, N_HEADS))
    w_gate_up = nrm(ks[10], (DEPTH, D_MODEL, 2 * D_FF)) * D_MODEL ** -0.5
    w_down = nrm(ks[11], (DEPTH, D_FF, D_MODEL)) * D_FF ** -0.5
    final_norm = 1.0 + 0.05 * nrm(ks[12], (D_MODEL,))
    return {'x': x, 'norm_mixer': norm_mixer, 'norm_ffn': norm_ffn, 'pool_w': pool_w,
            'pool_scale': pool_scale, 'kv_norm': kv_norm, 'w_kv': w_kv, 'w_q': w_q, 'w_o': w_o,
            'rel_bias': rel_bias, 'w_gate_up': w_gate_up, 'w_down': w_down, 'final_norm': final_norm}


def reference(x, norm_mixer, norm_ffn, pool_w, pool_scale, kv_norm, w_kv, w_q, w_o,
              rel_bias, w_gate_up, w_down, final_norm):
    B, T, D = x.shape
    k_blocks = v_blocks = k_mean = None
    for layer in range(DEPTH):
        if layer < N_A_LAYERS:
            x = x + pool_mixer(rmsnorm(x, norm_mixer[layer]), pool_w[layer], pool_scale[layer])
        else:
            j = layer - N_A_LAYERS
            if j == 0:
                k_blocks, v_blocks, k_mean = shared_kv(x, kv_norm, w_kv)
            h = rmsnorm(x, norm_mixer[layer])
            q = (h @ w_q[j]).reshape(B, T, N_HEADS, HEAD_DIM)
            o = moba_attention(q, k_blocks, v_blocks, k_mean, rel_bias)
            x = x + o @ w_o[j]
        x = x + swiglu(rmsnorm(x, norm_ffn[layer]), w_gate_up[layer], w_down[layer])
    return rmsnorm(x, final_norm)
```

```python
import functools
import math

import jax
import jax.numpy as jnp
from jax import lax
from jax.experimental import pallas as pl
from jax.experimental.pallas import tpu as pltpu

D_MODEL = 1024
DEPTH = 4
N_A_LAYERS = DEPTH // 2
POOL_WINDOWS = (2, 4, 8, 16)
POOL_GROUP = D_MODEL // len(POOL_WINDOWS)
HEAD_DIM = 64
N_HEADS = D_MODEL // HEAD_DIM
MOBA_BLOCK = 256
MOBA_TOPK = 3
NUM_BUCKETS = 32
MAX_DISTANCE = 128
D_FF = -(-8 * D_MODEL // (3 * 256)) * 256
EPS = 1e-6

LANES = 128
POOL_HALO = 16
HEADS_PER_LANE_TILE = LANES // HEAD_DIM
N_HEAD_PAIRS = N_HEADS // HEADS_PER_LANE_TILE
VMEM_LIMIT = 56 * 1024 * 1024

_NT = (((1,), (1,)), ((), ()))
_NEG_INF = float("-inf")


def _rms(x, g):
    return x * lax.rsqrt(jnp.mean(x * x, axis=-1, keepdims=True) + EPS) * g


def _resident(shape):
    zeros = (0,) * len(shape)
    return pl.BlockSpec(shape, lambda *_: zeros, pipeline_mode=pl.Buffered(1))


def _pool_kernel(x_ref, halo_ref, g_ref, w_ref, scale_ref, o_ref, hh_ref, *, tm):
    i = pl.program_id(1)
    g = g_ref[...]
    x = x_ref[0]
    h = _rms(x, g)
    halo = _rms(halo_ref[0], g)
    hh_ref[0:POOL_HALO, :] = jnp.where(i > 0, halo, 0.0)
    hh_ref[POOL_HALO:, :] = h
    t = i * tm + lax.broadcasted_iota(jnp.int32, (tm, 1), 0)
    ys = []
    for gi, win in enumerate(POOL_WINDOWS):
        sl = slice(gi * POOL_GROUP, (gi + 1) * POOL_GROUP)
        wsum = h[:, sl]
        for k in range(1, win):
            wsum = wsum + hh_ref[POOL_HALO - k:POOL_HALO - k + tm, sl]
        cnt = jnp.minimum(t + 1, win).astype(jnp.float32)
        d = wsum / cnt - h[:, sl]
        ys.append(jnp.dot(d.astype(jnp.bfloat16), w_ref[gi],
                          preferred_element_type=jnp.float32))
    y = jnp.concatenate(ys, axis=-1)
    o_ref[0] = x + y * scale_ref[...]


def _pool_call(x, g, w, scale, tm=512):
    B, T, D = x.shape
    halo_blocks = tm // POOL_HALO
    return pl.pallas_call(
        functools.partial(_pool_kernel, tm=tm),
        grid=(B, T // tm),
        in_specs=[
            pl.BlockSpec((1, tm, D), lambda b, i: (b, i, 0)),
            pl.BlockSpec((1, POOL_HALO, D),
                         lambda b, i: (b, jnp.maximum(i * halo_blocks - 1, 0), 0)),
            _resident((1, D)),
            _resident(w.shape),
            _resident((1, D)),
        ],
        out_specs=pl.BlockSpec((1, tm, D), lambda b, i: (b, i, 0)),
        out_shape=jax.ShapeDtypeStruct(x.shape, x.dtype),
        scratch_shapes=[pltpu.VMEM((tm + POOL_HALO, D), jnp.float32)],
        compiler_params=pltpu.CompilerParams(
            dimension_semantics=("arbitrary", "arbitrary"),
            vmem_limit_bytes=VMEM_LIMIT),
        name="pool_mixer",
    )(x, x, g.reshape(1, D), w, scale.reshape(1, D))


def _ffn_kernel(x_ref, g_ref, wgu_ref, wd_ref, fg_ref, o_ref, act_ref, *, fc, final_norm):
    x = x_ref[...]
    h = _rms(x, g_ref[...]).astype(jnp.bfloat16)
    for c in range(D_FF // fc):
        gate = jnp.dot(h, wgu_ref[:, c * fc:(c + 1) * fc],
                       preferred_element_type=jnp.float32)
        up = jnp.dot(h, wgu_ref[:, D_FF + c * fc:D_FF + (c + 1) * fc],
                     preferred_element_type=jnp.float32)
        act = gate * (1.0 / (1.0 + jnp.exp(-gate))) * up
        act_ref[:, c * fc:(c + 1) * fc] = act.astype(jnp.bfloat16)
    y = x + jnp.dot(act_ref[...], wd_ref[...], preferred_element_type=jnp.float32)
    if final_norm:
        y = _rms(y, fg_ref[...])
    o_ref[...] = y


def _ffn_call(x, g, wgu, wd, final_g, final_norm, tm=512, fc=256):
    B, T, D = x.shape
    xf = x.reshape(B * T, D)
    out = pl.pallas_call(
        functools.partial(_ffn_kernel, fc=fc, final_norm=final_norm),
        grid=(B * T // tm,),
        in_specs=[
            pl.BlockSpec((tm, D), lambda i: (i, 0)),
            _resident((1, D)),
            _resident(wgu.shape),
            _resident(wd.shape),
            _resident((1, D)),
        ],
        out_specs=pl.BlockSpec((tm, D), lambda i: (i, 0)),
        out_shape=jax.ShapeDtypeStruct(xf.shape, xf.dtype),
        scratch_shapes=[pltpu.VMEM((tm, D_FF), jnp.bfloat16)],
        compiler_params=pltpu.CompilerParams(
            dimension_semantics=("arbitrary",),
            vmem_limit_bytes=VMEM_LIMIT),
        name="swiglu_ffn",
    )(xf, g.reshape(1, D), wgu, wd, final_g.reshape(1, D))
    return out.reshape(B, T, D)


def _kv_kernel(x_ref, g_ref, wk_ref, wvt_ref, k_ref, vt_ref, km_ref):
    j = pl.program_id(1)
    h = _rms(x_ref[0], g_ref[...]).astype(jnp.bfloat16)
    k = jnp.dot(h, wk_ref[...], preferred_element_type=jnp.float32)
    k_ref[0] = k.astype(jnp.bfloat16)
    vt = lax.dot_general(wvt_ref[...], h, _NT, preferred_element_type=jnp.float32)
    vt_ref[0] = vt.astype(jnp.bfloat16)
    km_ref[0, pl.ds(j, 1), :] = jnp.mean(k, axis=0, keepdims=True)


def _kv_call(x, g, wk, wvt):
    B, T, D = x.shape
    nb = T // MOBA_BLOCK
    return pl.pallas_call(
        _kv_kernel,
        grid=(B, nb),
        in_specs=[
            pl.BlockSpec((1, MOBA_BLOCK, D), lambda b, j: (b, j, 0)),
            _resident((1, D)),
            _resident(wk.shape),
            _resident(wvt.shape),
        ],
        out_specs=[
            pl.BlockSpec((1, MOBA_BLOCK, D), lambda b, j: (b, j, 0)),
            pl.BlockSpec((1, D, MOBA_BLOCK), lambda b, j: (b, 0, j)),
            pl.BlockSpec((1, nb, D), lambda b, j: (b, 0, 0)),
        ],
        out_shape=[
            jax.ShapeDtypeStruct((B, T, D), jnp.bfloat16),
            jax.ShapeDtypeStruct((B, D, T), jnp.bfloat16),
            jax.ShapeDtypeStruct((B, nb, D), jnp.float32),
        ],
        compiler_params=pltpu.CompilerParams(
            dimension_semantics=("arbitrary", "arbitrary"),
            vmem_limit_bytes=VMEM_LIMIT),
        name="shared_kv",
    )(x, g.reshape(1, D), wk, wvt)


def _rel_bucket(n):
    max_exact = NUM_BUCKETS // 2
    nf = jnp.maximum(n, max_exact).astype(jnp.float32)
    large = max_exact + (jnp.log(nf / max_exact) / math.log(MAX_DISTANCE / max_exact)
                         * (NUM_BUCKETS - max_exact)).astype(jnp.int32)
    large = jnp.minimum(large, NUM_BUCKETS - 1)
    return jnp.where(n < max_exact, n, large)


def _bias_kernel(rb_ref, o_ref):
    hd = pl.program_id(0)
    kl = lax.broadcasted_iota(jnp.int32, (MOBA_BLOCK, MOBA_BLOCK), 0)
    ql = lax.broadcasted_iota(jnp.int32, (MOBA_BLOCK, MOBA_BLOCK), 1)
    d_own = ql - kl
    for ti, dist in enumerate((d_own, d_own + MOBA_BLOCK)):
        bucket = _rel_bucket(jnp.maximum(dist, 0))
        tile = jnp.zeros((MOBA_BLOCK, MOBA_BLOCK), jnp.float32)
        for bi in range(NUM_BUCKETS):
            tile = jnp.where(bucket == bi, rb_ref[bi, hd], tile)
        if ti == 0:
            tile = jnp.where(dist >= 0, tile, _NEG_INF)
        o_ref[0, ti] = tile


def _bias_call(rel_bias):
    return pl.pallas_call(
        _bias_kernel,
        grid=(N_HEADS,),
        in_specs=[pl.BlockSpec(memory_space=pltpu.SMEM)],
        out_specs=pl.BlockSpec((1, 2, MOBA_BLOCK, MOBA_BLOCK), lambda h: (h, 0, 0, 0)),
        out_shape=jax.ShapeDtypeStruct((N_HEADS, 2, MOBA_BLOCK, MOBA_BLOCK), jnp.float32),
        compiler_params=pltpu.CompilerParams(dimension_semantics=("arbitrary",)),
        name="rel_bias_tiles",
    )(rel_bias)


def _attn_kernel(rb_ref, x_ref, g_ref, wq_ref, wo_ref, k_ref, vt_ref, km_ref, bias_ref,
                 o_ref, s_ref, sel_ref, ot_ref):
    own = pl.program_id(1)
    nb = km_ref.shape[1]
    tq = x_ref.shape[1]
    x = x_ref[0]
    h = _rms(x, g_ref[...]).astype(jnp.bfloat16)
    q = (jnp.dot(h, wq_ref[...], preferred_element_type=jnp.float32)
         * HEAD_DIM ** -0.5).astype(jnp.bfloat16)
    lane = lax.broadcasted_iota(jnp.int32, (1, LANES), 1)
    blk = lax.broadcasted_iota(jnp.int32, (nb, tq), 0)

    for p in range(N_HEAD_PAIRS):
        ls = slice(p * LANES, (p + 1) * LANES)
        qp = q[:, ls]
        kmp = km_ref[0, :, ls].astype(jnp.bfloat16)
        for hh in range(HEADS_PER_LANE_TILE):
            head = p * HEADS_PER_LANE_TILE + hh
            in_head = (lane >= hh * HEAD_DIM) & (lane < (hh + 1) * HEAD_DIM)
            qh = jnp.where(in_head, qp, jnp.zeros_like(qp))

            gate = lax.dot_general(kmp, qh, _NT, preferred_element_type=jnp.float32)
            gate = jnp.where(blk < own, gate, _NEG_INF)
            rank = jnp.zeros((nb, tq), jnp.int32)
            for bi in range(nb):
                gi = gate[bi:bi + 1, :]
                beats = (gi > gate) | ((gi == gate) & (bi < blk))
                rank = rank + beats.astype(jnp.int32)
            sel_ref[...] = ((blk < own) & (rank < MOBA_TOPK)).astype(jnp.int32)

            def scores(j):
                kj = k_ref[0, pl.ds(pl.multiple_of(j * MOBA_BLOCK, MOBA_BLOCK), MOBA_BLOCK), ls]
                return lax.dot_general(kj, qh, _NT, preferred_element_type=jnp.float32)

            def masked_store(j, s, m):
                keep = sel_ref[pl.ds(j, 1), :] > 0
                s = jnp.where(keep, s, _NEG_INF)
                s_ref[j] = s
                return jnp.maximum(m, jnp.max(s, axis=0, keepdims=True))

            far_bias = rb_ref[NUM_BUCKETS - 1, head]

            def far_body(j, m):
                return masked_store(j, scores(j) + far_bias, m)

            m = lax.fori_loop(0, jnp.maximum(own - 1, 0), far_body,
                              jnp.full((1, tq), _NEG_INF, jnp.float32))
            m_ref_val = m

            def prev_block(m):
                j = own - 1
                return masked_store(j, scores(j) + bias_ref[head, 1], m)

            m = lax.cond(own > 0, prev_block, lambda m: m, m_ref_val)

            s_own = scores(own) + bias_ref[head, 0]
            s_ref[own] = s_own
            m = jnp.maximum(m, jnp.max(s_own, axis=0, keepdims=True))

            def pv_body(j, carry):
                l, acc = carry
                pj = jnp.exp(s_ref[j] - m)
                l = l + jnp.sum(pj, axis=0, keepdims=True)
                vj = vt_ref[0, head * HEAD_DIM:(head + 1) * HEAD_DIM,
                            pl.ds(pl.multiple_of(j * MOBA_BLOCK, MOBA_BLOCK), MOBA_BLOCK)]
                acc = acc + jnp.dot(vj, pj.astype(jnp.bfloat16),
                                    preferred_element_type=jnp.float32)
                return l, acc

            l, acc = lax.fori_loop(
                0, own + 1, pv_body,
                (jnp.zeros((1, tq), jnp.float32), jnp.zeros((HEAD_DIM, tq), jnp.float32)))
            ot_ref[head * HEAD_DIM:(head + 1) * HEAD_DIM, :] = acc / l

    o = ot_ref[...].T.astype(jnp.bfloat16)
    o_ref[0] = x + jnp.dot(o, wo_ref[...], preferred_element_type=jnp.float32)


def _attn_call(x, g, wq, wo, k, vt, km, bias_tiles, rel_bias):
    B, T, D = x.shape
    nb = T // MOBA_BLOCK
    tq = MOBA_BLOCK
    return pl.pallas_call(
        _attn_kernel,
        grid=(B, T // tq),
        in_specs=[
            pl.BlockSpec(memory_space=pltpu.SMEM),
            pl.BlockSpec((1, tq, D), lambda b, i: (b, i, 0)),
            _resident((1, D)),
            _resident(wq.shape),
            _resident(wo.shape),
            pl.BlockSpec((1, T, D), lambda b, i: (b, 0, 0)),
            pl.BlockSpec((1, D, T), lambda b, i: (b, 0, 0)),
            pl.BlockSpec((1, nb, D), lambda b, i: (b, 0, 0)),
            _resident(bias_tiles.shape),
        ],
        out_specs=pl.BlockSpec((1, tq, D), lambda b, i: (b, i, 0)),
        out_shape=jax.ShapeDtypeStruct(x.shape, x.dtype),
        scratch_shapes=[
            pltpu.VMEM((nb, MOBA_BLOCK, tq), jnp.float32),
            pltpu.VMEM((nb, tq), jnp.int32),
            pltpu.VMEM((D, tq), jnp.float32),
        ],
        compiler_params=pltpu.CompilerParams(
            dimension_semantics=("arbitrary", "arbitrary"),
            vmem_limit_bytes=VMEM_LIMIT),
        name="moba_attention",
    )(rel_bias, x, g.reshape(1, D), wq, wo, k, vt, km, bias_tiles)


def kernel(x, norm_mixer, norm_ffn, pool_w, pool_scale, kv_norm, w_kv, w_q, w_o,
           rel_bias, w_gate_up, w_down, final_norm):
    bf = jnp.bfloat16
    assert x.shape[1] % MOBA_BLOCK == 0 and x.shape[2] == D_MODEL
    pool_w = pool_w.astype(bf)
    w_gate_up = w_gate_up.astype(bf)
    w_down = w_down.astype(bf)
    w_q = w_q.astype(bf)
    w_o = w_o.astype(bf)
    wk = w_kv[:, :D_MODEL].astype(bf)
    wvt = w_kv[:, D_MODEL:].T.astype(bf)

    k = vt = km = bias_tiles = None
    for layer in range(DEPTH):
        if layer < N_A_LAYERS:
            x = _pool_call(x, norm_mixer[layer], pool_w[layer], pool_scale[layer])
        else:
            j = layer - N_A_LAYERS
            if j == 0:
                k, vt, km = _kv_call(x, kv_norm, wk, wvt)
                bias_tiles = _bias_call(rel_bias)
            x = _attn_call(x, norm_mixer[layer], w_q[j], w_o[j], k, vt, km, bias_tiles, rel_bias)
        x = _ffn_call(x, norm_ffn[layer], w_gate_up[layer], w_down[layer], final_norm,
                      final_norm=(layer == DEPTH - 1))
    return x
```

```python
import functools
import math

import jax
import jax.numpy as jnp
from jax import lax
from jax.experimental import pallas as pl
from jax.experimental.pallas import tpu as pltpu

D_MODEL = 1024
DEPTH = 4
N_A_LAYERS = DEPTH // 2
POOL_WINDOWS = (2, 4, 8, 16)
POOL_GROUP = D_MODEL // len(POOL_WINDOWS)
HEAD_DIM = 64
N_HEADS = D_MODEL // HEAD_DIM
MOBA_BLOCK = 256
MOBA_TOPK = 3
NUM_BUCKETS = 32
MAX_DISTANCE = 128
D_FF = -(-8 * D_MODEL // (3 * 256)) * 256
EPS = 1e-6

LANES = 128
SUBLANES = 8
POOL_HALO = 16
HEADS_PER_LANE_TILE = LANES // HEAD_DIM
VMEM_LIMIT = 56 * 1024 * 1024
SCORE_LOOKAHEAD = 3

_NT = (((1,), (1,)), ((), ()))
_NEG_INF = float("-inf")
_MASKED = -1e30
LOG2E = math.log2(math.e)
ONES_ROWS = 16


def _rms(x, g):
    return x * lax.rsqrt(jnp.mean(x * x, axis=-1, keepdims=True) + EPS) * g


def _resident(shape):
    zeros = (0,) * len(shape)
    return pl.BlockSpec(shape, lambda *_: zeros, pipeline_mode=pl.Buffered(1))


def _pool_kernel(x_ref, halo_ref, g_ref, w_ref, scale_ref, o_ref, hh_ref, *, tm):
    i = pl.program_id(1)
    g = g_ref[...]
    x = x_ref[0]
    h = _rms(x, g)
    halo = _rms(halo_ref[0], g)
    hh_ref[0:POOL_HALO, :] = jnp.where(i > 0, halo, 0.0)
    hh_ref[POOL_HALO:, :] = h
    t = i * tm + lax.broadcasted_iota(jnp.int32, (tm, 1), 0)
    ys = []
    for gi, win in enumerate(POOL_WINDOWS):
        sl = slice(gi * POOL_GROUP, (gi + 1) * POOL_GROUP)
        wsum = h[:, sl]
        for k in range(1, win):
            wsum = wsum + hh_ref[POOL_HALO - k:POOL_HALO - k + tm, sl]
        cnt = jnp.minimum(t + 1, win).astype(jnp.float32)
        d = wsum / cnt - h[:, sl]
        ys.append(jnp.dot(d.astype(jnp.bfloat16), w_ref[gi],
                          preferred_element_type=jnp.float32))
    y = jnp.concatenate(ys, axis=-1)
    o_ref[0] = x + y * scale_ref[...]


def _pool_call(x, g, w, scale, tm=512):
    B, T, D = x.shape
    halo_blocks = tm // POOL_HALO
    return pl.pallas_call(
        functools.partial(_pool_kernel, tm=tm),
        grid=(B, T // tm),
        in_specs=[
            pl.BlockSpec((1, tm, D), lambda b, i: (b, i, 0)),
            pl.BlockSpec((1, POOL_HALO, D),
                         lambda b, i: (b, jnp.maximum(i * halo_blocks - 1, 0), 0)),
            _resident((1, D)),
            _resident(w.shape),
            _resident((1, D)),
        ],
        out_specs=pl.BlockSpec((1, tm, D), lambda b, i: (b, i, 0)),
        out_shape=jax.ShapeDtypeStruct(x.shape, x.dtype),
        scratch_shapes=[pltpu.VMEM((tm + POOL_HALO, D), jnp.float32)],
        compiler_params=pltpu.CompilerParams(
            dimension_semantics=("arbitrary", "arbitrary"),
            vmem_limit_bytes=VMEM_LIMIT),
        name="pool_mixer",
    )(x, x, g.reshape(1, D), w, scale.reshape(1, D))


def _ffn_kernel(x_ref, g_ref, wgu_ref, wd_ref, fg_ref, o_ref, act_ref, *, fc, final_norm):
    x = x_ref[...]
    h = _rms(x, g_ref[...]).astype(jnp.bfloat16)
    for c in range(D_FF // fc):
        gate = jnp.dot(h, wgu_ref[:, c * fc:(c + 1) * fc],
                       preferred_element_type=jnp.float32)
        up = jnp.dot(h, wgu_ref[:, D_FF + c * fc:D_FF + (c + 1) * fc],
                     preferred_element_type=jnp.float32)
        act = gate * (1.0 / (1.0 + jnp.exp(-gate))) * up
        act_ref[:, c * fc:(c + 1) * fc] = act.astype(jnp.bfloat16)
    y = x + jnp.dot(act_ref[...], wd_ref[...], preferred_element_type=jnp.float32)
    if final_norm:
        y = _rms(y, fg_ref[...])
    o_ref[...] = y


def _ffn_call(x, g, wgu, wd, final_g, final_norm, tm=512, fc=256):
    B, T, D = x.shape
    xf = x.reshape(B * T, D)
    out = pl.pallas_call(
        functools.partial(_ffn_kernel, fc=fc, final_norm=final_norm),
        grid=(B * T // tm,),
        in_specs=[
            pl.BlockSpec((tm, D), lambda i: (i, 0)),
            _resident((1, D)),
            _resident(wgu.shape),
            _resident(wd.shape),
            _resident((1, D)),
        ],
        out_specs=pl.BlockSpec((tm, D), lambda i: (i, 0)),
        out_shape=jax.ShapeDtypeStruct(xf.shape, xf.dtype),
        scratch_shapes=[pltpu.VMEM((tm, D_FF), jnp.bfloat16)],
        compiler_params=pltpu.CompilerParams(
            dimension_semantics=("arbitrary",),
            vmem_limit_bytes=VMEM_LIMIT),
        name="swiglu_ffn",
    )(xf, g.reshape(1, D), wgu, wd, final_g.reshape(1, D))
    return out.reshape(B, T, D)


def _kv_kernel(x_ref, g_ref, wk_ref, wvt_ref, k_ref, vt_ref, km_ref):
    j = pl.program_id(1)
    h = _rms(x_ref[0], g_ref[...]).astype(jnp.bfloat16)
    k = jnp.dot(h, wk_ref[...], preferred_element_type=jnp.float32)
    k_ref[0] = k.astype(jnp.bfloat16)
    vt = lax.dot_general(wvt_ref[...], h, _NT, preferred_element_type=jnp.float32)
    vt_ref[0] = vt.astype(jnp.bfloat16)
    km_ref[0, pl.ds(j, 1), :] = jnp.mean(k, axis=0, keepdims=True)


def _kv_call(x, g, wk, wvt):
    B, T, D = x.shape
    nb = T // MOBA_BLOCK
    return pl.pallas_call(
        _kv_kernel,
        grid=(B, nb),
        in_specs=[
            pl.BlockSpec((1, MOBA_BLOCK, D), lambda b, j: (b, j, 0)),
            _resident((1, D)),
            _resident(wk.shape),
            _resident(wvt.shape),
        ],
        out_specs=[
            pl.BlockSpec((1, MOBA_BLOCK, D), lambda b, j: (b, j, 0)),
            pl.BlockSpec((1, D, MOBA_BLOCK), lambda b, j: (b, 0, j)),
            pl.BlockSpec((1, nb, D), lambda b, j: (b, 0, 0)),
        ],
        out_shape=[
            jax.ShapeDtypeStruct((B, T, D), jnp.bfloat16),
            jax.ShapeDtypeStruct((B, D, T), jnp.bfloat16),
            jax.ShapeDtypeStruct((B, nb, D), jnp.float32),
        ],
        compiler_params=pltpu.CompilerParams(
            dimension_semantics=("arbitrary", "arbitrary"),
            vmem_limit_bytes=VMEM_LIMIT),
        name="shared_kv",
    )(x, g.reshape(1, D), wk, wvt)


def _rel_bucket(n):
    max_exact = NUM_BUCKETS // 2
    nf = jnp.maximum(n, max_exact).astype(jnp.float32)
    large = max_exact + (jnp.log(nf / max_exact) / math.log(MAX_DISTANCE / max_exact)
                         * (NUM_BUCKETS - max_exact)).astype(jnp.int32)
    large = jnp.minimum(large, NUM_BUCKETS - 1)
    return jnp.where(n < max_exact, n, large)


def _bias_kernel(rb_ref, o_ref):
    hd = pl.program_id(0)
    kl = lax.broadcasted_iota(jnp.int32, (MOBA_BLOCK, MOBA_BLOCK), 0)
    ql = lax.broadcasted_iota(jnp.int32, (MOBA_BLOCK, MOBA_BLOCK), 1)
    d_own = ql - kl
    for ti, dist in enumerate((d_own, d_own + MOBA_BLOCK)):
        bucket = _rel_bucket(jnp.maximum(dist, 0))
        tile = jnp.zeros((MOBA_BLOCK, MOBA_BLOCK), jnp.float32)
        for bi in range(NUM_BUCKETS):
            tile = jnp.where(bucket == bi, rb_ref[bi, hd], tile)
        if ti == 0:
            tile = jnp.where(dist >= 0, tile, _MASKED)
        o_ref[0, ti] = tile * LOG2E


def _bias_call(rel_bias):
    return pl.pallas_call(
        _bias_kernel,
        grid=(N_HEADS,),
        in_specs=[pl.BlockSpec(memory_space=pltpu.SMEM)],
        out_specs=pl.BlockSpec((1, 2, MOBA_BLOCK, MOBA_BLOCK), lambda h: (h, 0, 0, 0)),
        out_shape=jax.ShapeDtypeStruct((N_HEADS, 2, MOBA_BLOCK, MOBA_BLOCK), jnp.float32),
        compiler_params=pltpu.CompilerParams(dimension_semantics=("arbitrary",)),
        name="rel_bias_tiles",
    )(rel_bias)


def _sublane_all(op, x):
    shift = SUBLANES // 2
    while shift:
        x = op(x, pltpu.roll(x, shift, axis=0))
        shift //= 2
    return x


def _attn_kernel(rb_ref, x_ref, g_ref, wq_ref, wo_ref, k_ref, vt_ref, km_ref, bias_ref,
                 o_ref, qh_ref, mask_ref, m_ref, l_ref, acc_ref):
    own = pl.program_id(1)
    nb = km_ref.shape[1]
    tq = x_ref.shape[1]
    D = x_ref.shape[2]
    row_groups = MOBA_BLOCK // SUBLANES
    head_groups = HEAD_DIM // SUBLANES
    x = x_ref[0]
    h = _rms(x, g_ref[...]).astype(jnp.bfloat16)
    qf = jnp.dot(h, wq_ref[...], preferred_element_type=jnp.float32)
    q_gate = (qf * HEAD_DIM ** -0.5).astype(jnp.bfloat16)
    q = (qf * (HEAD_DIM ** -0.5 * LOG2E)).astype(jnp.bfloat16)

    lane = lax.broadcasted_iota(jnp.int32, (1, LANES), 1)
    for head in range(N_HEADS):
        p, hh = divmod(head, HEADS_PER_LANE_TILE)
        qp = q[:, p * LANES:(p + 1) * LANES]
        in_head = (lane >= hh * HEAD_DIM) & (lane < (hh + 1) * HEAD_DIM)
        qh_ref[head] = jnp.where(in_head, qp, jnp.zeros_like(qp))

    km = jnp.broadcast_to(km_ref[0][None], (N_HEADS, nb, D)).reshape(N_HEADS * nb, D)
    row_head = lax.broadcasted_iota(jnp.int32, (N_HEADS * nb, D), 0) // nb
    col_head = lax.broadcasted_iota(jnp.int32, (N_HEADS * nb, D), 1) // HEAD_DIM
    km_bd = jnp.where(row_head == col_head, km, 0.0).astype(jnp.bfloat16)
    gate = lax.dot_general(km_bd, q_gate, _NT, preferred_element_type=jnp.float32)
    gate = gate.reshape(N_HEADS, nb, tq)
    blk = lax.broadcasted_iota(jnp.int32, (N_HEADS, nb, tq), 1)
    past = blk < own
    gate = jnp.where(past, gate, _NEG_INF)
    rank = jnp.zeros((N_HEADS, nb, tq), jnp.int32)
    for bi in range(nb):
        gi = gate[:, bi:bi + 1, :]
        beats = (gi > gate) | ((gi == gate) & (bi < blk))
        rank = rank + beats.astype(jnp.int32)
    mask_ref[...] = jnp.where(past & (rank < MOBA_TOPK), 0.0, _MASKED)

    m_ref[...] = jnp.full(m_ref.shape, _MASKED, jnp.float32)
    l_ref[...] = jnp.zeros(l_ref.shape, jnp.float32)
    acc_ref[...] = jnp.zeros(acc_ref.shape, jnp.float32)

    def block(j, kind):
        keys = pl.ds(pl.multiple_of(j * MOBA_BLOCK, MOBA_BLOCK), MOBA_BLOCK)

        def scores(head):
            p = head // HEADS_PER_LANE_TILE
            kj = k_ref[0, keys, p * LANES:(p + 1) * LANES]
            return lax.dot_general(kj, qh_ref[head], _NT, preferred_element_type=jnp.float32)

        def softmax(head, s):
            s = s.reshape(row_groups, SUBLANES, tq)
            if kind == "own":
                s = s + bias_ref[head, 0].reshape(row_groups, SUBLANES, tq)
            else:
                row = mask_ref[head, pl.ds(j, 1), :]
                if kind == "far":
                    row = row + rb_ref[NUM_BUCKETS - 1, head] * LOG2E
                else:
                    s = s + bias_ref[head, 1].reshape(row_groups, SUBLANES, tq)
                s = s + jnp.broadcast_to(row, (SUBLANES, tq))[None]
            m_old = m_ref[head]
            m_new = jnp.maximum(m_old, _sublane_all(jnp.maximum, jnp.max(s, axis=0)))
            alpha = jnp.exp2(m_old - m_new)
            pj = jnp.exp2(s - m_new[None])
            m_ref[head] = m_new
            return pj.reshape(MOBA_BLOCK, tq).astype(jnp.bfloat16), alpha

        def accumulate(head, pv, alpha):
            hs = slice(head * HEAD_DIM, (head + 1) * HEAD_DIM)
            l_ref[head] = l_ref[head] * alpha + pv[HEAD_DIM:HEAD_DIM + SUBLANES]
            acc = acc_ref[hs, :].reshape(head_groups, SUBLANES, tq)
            acc = acc * alpha[None] + pv[:HEAD_DIM].reshape(head_groups, SUBLANES, tq)
            acc_ref[hs, :] = acc.reshape(HEAD_DIM, tq)

        ones = jnp.ones((ONES_ROWS, MOBA_BLOCK), jnp.bfloat16)

        def weighted_values(head, pj):
            vj = vt_ref[0, head * HEAD_DIM:(head + 1) * HEAD_DIM, keys]
            return jnp.dot(jnp.concatenate([vj, ones], axis=0), pj,
                           preferred_element_type=jnp.float32)

        pending_scores = {h: scores(h) for h in range(SCORE_LOOKAHEAD)}
        probs = {}
        products = {}
        for step in range(N_HEADS + 2):
            if step + SCORE_LOOKAHEAD < N_HEADS:
                pending_scores[step + SCORE_LOOKAHEAD] = scores(step + SCORE_LOOKAHEAD)
            if 0 <= step - 1 < N_HEADS:
                pj, alpha = probs.pop(step - 1)
                products[step - 1] = (weighted_values(step - 1, pj), alpha)
            if step < N_HEADS:
                probs[step] = softmax(step, pending_scores.pop(step))
            if 0 <= step - 2 < N_HEADS:
                accumulate(step - 2, *products.pop(step - 2))

    def far_body(j, carry):
        block(j, "far")
        return carry

    lax.fori_loop(0, jnp.maximum(own - 1, 0), far_body, 0)

    @pl.when(own > 0)
    def _():
        block(own - 1, "prev")

    block(own, "own")

    for head in range(N_HEADS):
        hs = slice(head * HEAD_DIM, (head + 1) * HEAD_DIM)
        inv = 1.0 / l_ref[head]
        acc = acc_ref[hs, :].reshape(head_groups, SUBLANES, tq) * inv[None]
        acc_ref[hs, :] = acc.reshape(HEAD_DIM, tq)
    o = acc_ref[...].T.astype(jnp.bfloat16)
    o_ref[0] = x + jnp.dot(o, wo_ref[...], preferred_element_type=jnp.float32)


def _attn_call(x, g, wq, wo, k, vt, km, bias_tiles, rel_bias):
    B, T, D = x.shape
    nb = T // MOBA_BLOCK
    tq = MOBA_BLOCK
    return pl.pallas_call(
        _attn_kernel,
        grid=(B, T // tq),
        in_specs=[
            pl.BlockSpec(memory_space=pltpu.SMEM),
            pl.BlockSpec((1, tq, D), lambda b, i: (b, i, 0)),
            _resident((1, D)),
            _resident(wq.shape),
            _resident(wo.shape),
            pl.BlockSpec((1, T, D), lambda b, i: (b, 0, 0)),
            pl.BlockSpec((1, D, T), lambda b, i: (b, 0, 0)),
            pl.BlockSpec((1, nb, D), lambda b, i: (b, 0, 0)),
            _resident(bias_tiles.shape),
        ],
        out_specs=pl.BlockSpec((1, tq, D), lambda b, i: (b, i, 0)),
        out_shape=jax.ShapeDtypeStruct(x.shape, x.dtype),
        scratch_shapes=[
            pltpu.VMEM((N_HEADS, tq, LANES), jnp.bfloat16),
            pltpu.VMEM((N_HEADS, nb, tq), jnp.float32),
            pltpu.VMEM((N_HEADS, SUBLANES, tq), jnp.float32),
            pltpu.VMEM((N_HEADS, SUBLANES, tq), jnp.float32),
            pltpu.VMEM((D, tq), jnp.float32),
        ],
        compiler_params=pltpu.CompilerParams(
            dimension_semantics=("arbitrary", "arbitrary"),
            vmem_limit_bytes=VMEM_LIMIT),
        name="moba_attention",
    )(rel_bias, x, g.reshape(1, D), wq, wo, k, vt, km, bias_tiles)


def kernel(x, norm_mixer, norm_ffn, pool_w, pool_scale, kv_norm, w_kv, w_q, w_o,
           rel_bias, w_gate_up, w_down, final_norm):
    bf = jnp.bfloat16
    assert x.shape[1] % MOBA_BLOCK == 0 and x.shape[2] == D_MODEL
    pool_w = pool_w.astype(bf)
    w_gate_up = w_gate_up.astype(bf)
    w_down = w_down.astype(bf)
    w_q = w_q.astype(bf)
    w_o = w_o.astype(bf)
    wk = w_kv[:, :D_MODEL].astype(bf)
    wvt = w_kv[:, D_MODEL:].T.astype(bf)

    k = vt = km = bias_tiles = None
    for layer in range(DEPTH):
        if layer < N_A_LAYERS:
            x = _pool_call(x, norm_mixer[layer], pool_w[layer], pool_scale[layer])
        else:
            j = layer - N_A_LAYERS
            if j == 0:
                k, vt, km = _kv_call(x, kv_norm, wk, wvt)
                bias_tiles = _bias_call(rel_bias)
            x = _attn_call(x, norm_mixer[layer], w_q[j], w_o[j], k, vt, km, bias_tiles, rel_bias)
        x = _ffn_call(x, norm_ffn[layer], w_gate_up[layer], w_down[layer], final_norm,
                      final_norm=(layer == DEPTH - 1))
    return x
```

```python
import functools
import math

import jax
import jax.numpy as jnp
from jax import lax
from jax.experimental import pallas as pl
from jax.experimental.pallas import tpu as pltpu

D_MODEL = 1024
DEPTH = 4
N_A_LAYERS = DEPTH // 2
POOL_WINDOWS = (2, 4, 8, 16)
POOL_GROUP = D_MODEL // len(POOL_WINDOWS)
HEAD_DIM = 64
N_HEADS = D_MODEL // HEAD_DIM
MOBA_BLOCK = 256
MOBA_TOPK = 3
NUM_BUCKETS = 32
MAX_DISTANCE = 128
D_FF = -(-8 * D_MODEL // (3 * 256)) * 256
EPS = 1e-6

LANES = 128
SUBLANES = 8
POOL_HALO = 16
HEADS_PER_LANE_TILE = LANES // HEAD_DIM
VMEM_LIMIT = 56 * 1024 * 1024
SCORE_LOOKAHEAD = 3
Q_SUBTILES = 2

_NT = (((1,), (1,)), ((), ()))
_NEG_INF = float("-inf")
_MASKED = -1e30
LOG2E = math.log2(math.e)
ONES_ROWS = 16


def _rms(x, g):
    return x * lax.rsqrt(jnp.mean(x * x, axis=-1, keepdims=True) + EPS) * g


def _resident(shape):
    zeros = (0,) * len(shape)
    return pl.BlockSpec(shape, lambda *_: zeros, pipeline_mode=pl.Buffered(1))


def _pool_kernel(x_ref, halo_ref, g_ref, w_ref, scale_ref, o_ref, hh_ref, *, tm):
    i = pl.program_id(1)
    g = g_ref[...]
    x = x_ref[0]
    h = _rms(x, g)
    halo = _rms(halo_ref[0], g)
    hh_ref[0:POOL_HALO, :] = jnp.where(i > 0, halo, 0.0)
    hh_ref[POOL_HALO:, :] = h
    t = i * tm + lax.broadcasted_iota(jnp.int32, (tm, 1), 0)
    ys = []
    for gi, win in enumerate(POOL_WINDOWS):
        sl = slice(gi * POOL_GROUP, (gi + 1) * POOL_GROUP)
        wsum = h[:, sl]
        for k in range(1, win):
            wsum = wsum + hh_ref[POOL_HALO - k:POOL_HALO - k + tm, sl]
        cnt = jnp.minimum(t + 1, win).astype(jnp.float32)
        d = wsum / cnt - h[:, sl]
        ys.append(jnp.dot(d.astype(jnp.bfloat16), w_ref[gi],
                          preferred_element_type=jnp.float32))
    y = jnp.concatenate(ys, axis=-1)
    o_ref[0] = x + y * scale_ref[...]


def _pool_call(x, g, w, scale, tm=512):
    B, T, D = x.shape
    halo_blocks = tm // POOL_HALO
    return pl.pallas_call(
        functools.partial(_pool_kernel, tm=tm),
        grid=(B, T // tm),
        in_specs=[
            pl.BlockSpec((1, tm, D), lambda b, i: (b, i, 0)),
            pl.BlockSpec((1, POOL_HALO, D),
                         lambda b, i: (b, jnp.maximum(i * halo_blocks - 1, 0), 0)),
            _resident((1, D)),
            _resident(w.shape),
            _resident((1, D)),
        ],
        out_specs=pl.BlockSpec((1, tm, D), lambda b, i: (b, i, 0)),
        out_shape=jax.ShapeDtypeStruct(x.shape, x.dtype),
        scratch_shapes=[pltpu.VMEM((tm + POOL_HALO, D), jnp.float32)],
        compiler_params=pltpu.CompilerParams(
            dimension_semantics=("arbitrary", "arbitrary"),
            vmem_limit_bytes=VMEM_LIMIT),
        name="pool_mixer",
    )(x, x, g.reshape(1, D), w, scale.reshape(1, D))


def _ffn_kernel(x_ref, g_ref, wgu_ref, wd_ref, fg_ref, o_ref, act_ref, *, fc, final_norm):
    x = x_ref[...]
    h = _rms(x, g_ref[...]).astype(jnp.bfloat16)
    for c in range(D_FF // fc):
        gate = jnp.dot(h, wgu_ref[:, c * fc:(c + 1) * fc],
                       preferred_element_type=jnp.float32)
        up = jnp.dot(h, wgu_ref[:, D_FF + c * fc:D_FF + (c + 1) * fc],
                     preferred_element_type=jnp.float32)
        act = gate * (1.0 / (1.0 + jnp.exp(-gate))) * up
        act_ref[:, c * fc:(c + 1) * fc] = act.astype(jnp.bfloat16)
    y = x + jnp.dot(act_ref[...], wd_ref[...], preferred_element_type=jnp.float32)
    if final_norm:
        y = _rms(y, fg_ref[...])
    o_ref[...] = y


def _ffn_call(x, g, wgu, wd, final_g, final_norm, tm=1024, fc=256):
    B, T, D = x.shape
    xf = x.reshape(B * T, D)
    out = pl.pallas_call(
        functools.partial(_ffn_kernel, fc=fc, final_norm=final_norm),
        grid=(B * T // tm,),
        in_specs=[
            pl.BlockSpec((tm, D), lambda i: (i, 0)),
            _resident((1, D)),
            _resident(wgu.shape),
            _resident(wd.shape),
            _resident((1, D)),
        ],
        out_specs=pl.BlockSpec((tm, D), lambda i: (i, 0)),
        out_shape=jax.ShapeDtypeStruct(xf.shape, xf.dtype),
        scratch_shapes=[pltpu.VMEM((tm, D_FF), jnp.bfloat16)],
        compiler_params=pltpu.CompilerParams(
            dimension_semantics=("arbitrary",),
            vmem_limit_bytes=VMEM_LIMIT),
        name="swiglu_ffn",
    )(xf, g.reshape(1, D), wgu, wd, final_g.reshape(1, D))
    return out.reshape(B, T, D)


def _kv_kernel(x_ref, g_ref, wk_ref, wvt_ref, k_ref, vt_ref, km_ref):
    j = pl.program_id(1)
    h = _rms(x_ref[0], g_ref[...]).astype(jnp.bfloat16)
    k = jnp.dot(h, wk_ref[...], preferred_element_type=jnp.float32)
    k_ref[0] = k.astype(jnp.bfloat16)
    vt = lax.dot_general(wvt_ref[...], h, _NT, preferred_element_type=jnp.float32)
    vt_ref[0] = vt.astype(jnp.bfloat16)
    km_ref[0, pl.ds(j, 1), :] = jnp.mean(k, axis=0, keepdims=True)


def _kv_call(x, g, wk, wvt):
    B, T, D = x.shape
    nb = T // MOBA_BLOCK
    return pl.pallas_call(
        _kv_kernel,
        grid=(B, nb),
        in_specs=[
            pl.BlockSpec((1, MOBA_BLOCK, D), lambda b, j: (b, j, 0)),
            _resident((1, D)),
            _resident(wk.shape),
            _resident(wvt.shape),
        ],
        out_specs=[
            pl.BlockSpec((1, MOBA_BLOCK, D), lambda b, j: (b, j, 0)),
            pl.BlockSpec((1, D, MOBA_BLOCK), lambda b, j: (b, 0, j)),
            pl.BlockSpec((1, nb, D), lambda b, j: (b, 0, 0)),
        ],
        out_shape=[
            jax.ShapeDtypeStruct((B, T, D), jnp.bfloat16),
            jax.ShapeDtypeStruct((B, D, T), jnp.bfloat16),
            jax.ShapeDtypeStruct((B, nb, D), jnp.float32),
        ],
        compiler_params=pltpu.CompilerParams(
            dimension_semantics=("arbitrary", "arbitrary"),
            vmem_limit_bytes=VMEM_LIMIT),
        name="shared_kv",
    )(x, g.reshape(1, D), wk, wvt)


def _rel_bucket(n):
    max_exact = NUM_BUCKETS // 2
    nf = jnp.maximum(n, max_exact).astype(jnp.float32)
    large = max_exact + (jnp.log(nf / max_exact) / math.log(MAX_DISTANCE / max_exact)
                         * (NUM_BUCKETS - max_exact)).astype(jnp.int32)
    large = jnp.minimum(large, NUM_BUCKETS - 1)
    return jnp.where(n < max_exact, n, large)


def _bias_kernel(rb_ref, o_ref):
    hd = pl.program_id(0)
    kl = lax.broadcasted_iota(jnp.int32, (MOBA_BLOCK, MOBA_BLOCK), 0)
    ql = lax.broadcasted_iota(jnp.int32, (MOBA_BLOCK, MOBA_BLOCK), 1)
    d_own = ql - kl
    for ti, dist in enumerate((d_own, d_own + MOBA_BLOCK)):
        bucket = _rel_bucket(jnp.maximum(dist, 0))
        tile = jnp.zeros((MOBA_BLOCK, MOBA_BLOCK), jnp.float32)
        for bi in range(NUM_BUCKETS):
            tile = jnp.where(bucket == bi, rb_ref[bi, hd], tile)
        if ti == 0:
            tile = jnp.where(dist >= 0, tile, _MASKED)
        o_ref[0, ti] = tile * LOG2E


def _bias_call(rel_bias):
    return pl.pallas_call(
        _bias_kernel,
        grid=(N_HEADS,),
        in_specs=[pl.BlockSpec(memory_space=pltpu.SMEM)],
        out_specs=pl.BlockSpec((1, 2, MOBA_BLOCK, MOBA_BLOCK), lambda h: (h, 0, 0, 0)),
        out_shape=jax.ShapeDtypeStruct((N_HEADS, 2, MOBA_BLOCK, MOBA_BLOCK), jnp.float32),
        compiler_params=pltpu.CompilerParams(dimension_semantics=("arbitrary",)),
        name="rel_bias_tiles",
    )(rel_bias)


def _sublane_all(op, x):
    shift = SUBLANES // 2
    while shift:
        x = op(x, pltpu.roll(x, shift, axis=0))
        shift //= 2
    return x


def _attn_kernel(rb_ref, x_ref, g_ref, wq_ref, wo_ref, k_ref, vt_ref, km_ref, bias_ref,
                 o_ref, qh_ref, mask_ref, m_ref, l_ref, acc_ref):
    first_own = pl.program_id(1) * Q_SUBTILES
    nb = km_ref.shape[1]
    tq = MOBA_BLOCK
    D = x_ref.shape[2]
    row_groups = MOBA_BLOCK // SUBLANES
    head_groups = HEAD_DIM // SUBLANES
    x = x_ref[0]
    h = _rms(x, g_ref[...]).astype(jnp.bfloat16)
    qf = jnp.dot(h, wq_ref[...], preferred_element_type=jnp.float32)
    q_gate = (qf * HEAD_DIM ** -0.5).astype(jnp.bfloat16)
    q = (qf * (HEAD_DIM ** -0.5 * LOG2E)).astype(jnp.bfloat16)

    lane = lax.broadcasted_iota(jnp.int32, (1, LANES), 1)
    for sub in range(Q_SUBTILES):
        for head in range(N_HEADS):
            p, hh = divmod(head, HEADS_PER_LANE_TILE)
            qp = q[sub * tq:(sub + 1) * tq, p * LANES:(p + 1) * LANES]
            in_head = (lane >= hh * HEAD_DIM) & (lane < (hh + 1) * HEAD_DIM)
            qh_ref[sub, head] = jnp.where(in_head, qp, jnp.zeros_like(qp))

    n_q = Q_SUBTILES * tq
    km = jnp.broadcast_to(km_ref[0][:, None, :], (nb, N_HEADS, D)).reshape(nb * N_HEADS, D)
    row_head = lax.broadcasted_iota(jnp.int32, (nb * N_HEADS, D), 0) % N_HEADS
    col_head = lax.broadcasted_iota(jnp.int32, (nb * N_HEADS, D), 1) // HEAD_DIM
    km_bd = jnp.where(row_head == col_head, km, 0.0).astype(jnp.bfloat16)
    gate = lax.dot_general(km_bd, q_gate, _NT, preferred_element_type=jnp.float32)
    own_of_query = first_own + lax.broadcasted_iota(jnp.int32, (N_HEADS, n_q), 1) // tq
    past = [own_of_query > j for j in range(nb)]
    gates = [jnp.where(past[j], gate[j * N_HEADS:(j + 1) * N_HEADS], _NEG_INF)
             for j in range(nb)]
    rank = [jnp.zeros((N_HEADS, n_q), jnp.int32) for _ in range(nb)]
    for lo in range(nb):
        for hi in range(lo + 1, nb):
            lo_wins = jnp.where(gates[lo] >= gates[hi], 1, 0)
            rank[hi] = rank[hi] + lo_wins
            rank[lo] = rank[lo] + (1 - lo_wins)
    for j in range(nb):
        keep = jnp.where(past[j], 0.0, _MASKED)
        mask_ref[j] = jnp.where(rank[j] < MOBA_TOPK, keep, _MASKED)

    m_ref[...] = jnp.full(m_ref.shape, _MASKED, jnp.float32)
    l_ref[...] = jnp.zeros(l_ref.shape, jnp.float32)
    acc_ref[...] = jnp.zeros(acc_ref.shape, jnp.float32)

    def block(j, kinds):
        keys = pl.ds(pl.multiple_of(j * MOBA_BLOCK, MOBA_BLOCK), MOBA_BLOCK)
        units = [(sub, head) for sub, kind in enumerate(kinds) if kind is not None
                 for head in range(N_HEADS)]

        def scores(unit):
            sub, head = unit
            p = head // HEADS_PER_LANE_TILE
            kj = k_ref[0, keys, p * LANES:(p + 1) * LANES]
            return lax.dot_general(kj, qh_ref[sub, head], _NT,
                                   preferred_element_type=jnp.float32)

        def softmax(unit, s):
            sub, head = unit
            kind = kinds[sub]
            s = s.reshape(row_groups, SUBLANES, tq)
            if kind == "own":
                s = s + bias_ref[head, 0].reshape(row_groups, SUBLANES, tq)
            else:
                row = mask_ref[j, head:head + 1, sub * tq:(sub + 1) * tq]
                if kind == "far":
                    row = row + rb_ref[NUM_BUCKETS - 1, head] * LOG2E
                else:
                    s = s + bias_ref[head, 1].reshape(row_groups, SUBLANES, tq)
                s = s + jnp.broadcast_to(row, (SUBLANES, tq))[None]
            m_old = m_ref[sub, head]
            m_new = jnp.maximum(m_old, _sublane_all(jnp.maximum, jnp.max(s, axis=0)))
            alpha = jnp.exp2(m_old - m_new)
            pj = jnp.exp2(s - m_new[None])
            m_ref[sub, head] = m_new
            return pj.reshape(MOBA_BLOCK, tq).astype(jnp.bfloat16), alpha

        def accumulate(unit, pv, alpha):
            sub, head = unit
            hs = slice(head * HEAD_DIM, (head + 1) * HEAD_DIM)
            l_ref[sub, head] = l_ref[sub, head] * alpha + pv[HEAD_DIM:HEAD_DIM + SUBLANES]
            acc = acc_ref[sub, hs, :].reshape(head_groups, SUBLANES, tq)
            acc = acc * alpha[None] + pv[:HEAD_DIM].reshape(head_groups, SUBLANES, tq)
            acc_ref[sub, hs, :] = acc.reshape(HEAD_DIM, tq)

        ones = jnp.ones((ONES_ROWS, MOBA_BLOCK), jnp.bfloat16)

        def weighted_values(unit, pj):
            _, head = unit
            vj = vt_ref[0, head * HEAD_DIM:(head + 1) * HEAD_DIM, keys]
            return jnp.dot(jnp.concatenate([vj, ones], axis=0), pj,
                           preferred_element_type=jnp.float32)

        n = len(units)
        pending_scores = {i: scores(units[i]) for i in range(SCORE_LOOKAHEAD)}
        probs = {}
        products = {}
        for step in range(n + 2):
            if step + SCORE_LOOKAHEAD < n:
                pending_scores[step + SCORE_LOOKAHEAD] = scores(units[step + SCORE_LOOKAHEAD])
            if 0 <= step - 1 < n:
                pj, alpha = probs.pop(step - 1)
                products[step - 1] = (weighted_values(units[step - 1], pj), alpha)
            if step < n:
                probs[step] = softmax(units[step], pending_scores.pop(step))
            if 0 <= step - 2 < n:
                accumulate(units[step - 2], *products.pop(step - 2))

    def far_body(j, carry):
        block(j, ("far",) * Q_SUBTILES)
        return carry

    lax.fori_loop(0, jnp.maximum(first_own - 1, 0), far_body, 0)

    @pl.when(first_own > 0)
    def _():
        block(first_own - 1, ("prev",) + ("far",) * (Q_SUBTILES - 1))

    for d in range(Q_SUBTILES):
        kinds = tuple(None if sub < d else ("own", "prev")[sub - d] if sub - d < 2 else "far"
                      for sub in range(Q_SUBTILES))
        block(first_own + d, kinds)

    outs = []
    for sub in range(Q_SUBTILES):
        for head in range(N_HEADS):
            hs = slice(head * HEAD_DIM, (head + 1) * HEAD_DIM)
            inv = 1.0 / l_ref[sub, head]
            acc = acc_ref[sub, hs, :].reshape(head_groups, SUBLANES, tq) * inv[None]
            acc_ref[sub, hs, :] = acc.reshape(HEAD_DIM, tq)
        outs.append(acc_ref[sub].T.astype(jnp.bfloat16))
    o = jnp.concatenate(outs, axis=0)
    o_ref[0] = x + jnp.dot(o, wo_ref[...], preferred_element_type=jnp.float32)


def _attn_call(x, g, wq, wo, k, vt, km, bias_tiles, rel_bias):
    B, T, D = x.shape
    nb = T // MOBA_BLOCK
    tq = MOBA_BLOCK
    n_q = Q_SUBTILES * tq
    return pl.pallas_call(
        _attn_kernel,
        grid=(B, T // n_q),
        in_specs=[
            pl.BlockSpec(memory_space=pltpu.SMEM),
            pl.BlockSpec((1, n_q, D), lambda b, i: (b, i, 0)),
            _resident((1, D)),
            _resident(wq.shape),
            _resident(wo.shape),
            pl.BlockSpec((1, T, D), lambda b, i: (b, 0, 0)),
            pl.BlockSpec((1, D, T), lambda b, i: (b, 0, 0)),
            pl.BlockSpec((1, nb, D), lambda b, i: (b, 0, 0)),
            _resident(bias_tiles.shape),
        ],
        out_specs=pl.BlockSpec((1, n_q, D), lambda b, i: (b, i, 0)),
        out_shape=jax.ShapeDtypeStruct(x.shape, x.dtype),
        scratch_shapes=[
            pltpu.VMEM((Q_SUBTILES, N_HEADS, tq, LANES), jnp.bfloat16),
            pltpu.VMEM((nb, N_HEADS, n_q), jnp.float32),
            pltpu.VMEM((Q_SUBTILES, N_HEADS, SUBLANES, tq), jnp.float32),
            pltpu.VMEM((Q_SUBTILES, N_HEADS, SUBLANES, tq), jnp.float32),
            pltpu.VMEM((Q_SUBTILES, D, tq), jnp.float32),
        ],
        compiler_params=pltpu.CompilerParams(
            dimension_semantics=("arbitrary", "arbitrary"),
            vmem_limit_bytes=VMEM_LIMIT),
        name="moba_attention",
    )(rel_bias, x, g.reshape(1, D), wq, wo, k, vt, km, bias_tiles)


def kernel(x, norm_mixer, norm_ffn, pool_w, pool_scale, kv_norm, w_kv, w_q, w_o,
           rel_bias, w_gate_up, w_down, final_norm):
    bf = jnp.bfloat16
    assert x.shape[1] % MOBA_BLOCK == 0 and x.shape[2] == D_MODEL
    pool_w = pool_w.astype(bf)
    w_gate_up = w_gate_up.astype(bf)
    w_down = w_down.astype(bf)
    w_q = w_q.astype(bf)
    w_o = w_o.astype(bf)
    wk = w_kv[:, :D_MODEL].astype(bf)
    wvt = w_kv[:, D_MODEL:].T.astype(bf)

    k = vt = km = bias_tiles = None
    for layer in range(DEPTH):
        if layer < N_A_LAYERS:
            x = _pool_call(x, norm_mixer[layer], pool_w[layer], pool_scale[layer])
        else:
            j = layer - N_A_LAYERS
            if j == 0:
                k, vt, km = _kv_call(x, kv_norm, wk, wvt)
                bias_tiles = _bias_call(rel_bias)
            x = _attn_call(x, norm_mixer[layer], w_q[j], w_o[j], k, vt, km, bias_tiles, rel_bias)
        x = _ffn_call(x, norm_ffn[layer], w_gate_up[layer], w_down[layer], final_norm,
                      final_norm=(layer == DEPTH - 1))
    return x
```

```python
import functools
import math

import jax
import jax.numpy as jnp
from jax import lax
from jax.experimental import pallas as pl
from jax.experimental.pallas import tpu as pltpu

D_MODEL = 1024
DEPTH = 4
N_A_LAYERS = DEPTH // 2
POOL_WINDOWS = (2, 4, 8, 16)
POOL_GROUP = D_MODEL // len(POOL_WINDOWS)
HEAD_DIM = 64
N_HEADS = D_MODEL // HEAD_DIM
MOBA_BLOCK = 256
MOBA_TOPK = 3
NUM_BUCKETS = 32
MAX_DISTANCE = 128
D_FF = -(-8 * D_MODEL // (3 * 256)) * 256
EPS = 1e-6

LANES = 128
SUBLANES = 8
POOL_HALO = 16
HEADS_PER_LANE_TILE = LANES // HEAD_DIM
VMEM_LIMIT = 56 * 1024 * 1024
SCORE_LOOKAHEAD = 3
FAR_UNROLL = 2
Q_SUBTILES = 2

_NT = (((1,), (1,)), ((), ()))
_NEG_INF = float("-inf")
_MASKED = -1e30
LOG2E = math.log2(math.e)
ONES_ROWS = 16


def _rms(x, g):
    return x * lax.rsqrt(jnp.mean(x * x, axis=-1, keepdims=True) + EPS) * g


def _resident(shape):
    zeros = (0,) * len(shape)
    return pl.BlockSpec(shape, lambda *_: zeros, pipeline_mode=pl.Buffered(1))


def _pool_kernel(x_ref, halo_ref, g_ref, w_ref, scale_ref, o_ref, hh_ref, *, tm):
    i = pl.program_id(1)
    g = g_ref[...]
    x = x_ref[0]
    h = _rms(x, g)
    halo = _rms(halo_ref[0], g)
    hh_ref[0:POOL_HALO, :] = jnp.where(i > 0, halo, 0.0)
    hh_ref[POOL_HALO:, :] = h
    t = i * tm + lax.broadcasted_iota(jnp.int32, (tm, 1), 0)
    ys = []
    for gi, win in enumerate(POOL_WINDOWS):
        sl = slice(gi * POOL_GROUP, (gi + 1) * POOL_GROUP)
        wsum = h[:, sl]
        for k in range(1, win):
            wsum = wsum + hh_ref[POOL_HALO - k:POOL_HALO - k + tm, sl]
        cnt = jnp.minimum(t + 1, win).astype(jnp.float32)
        d = wsum / cnt - h[:, sl]
        ys.append(jnp.dot(d.astype(jnp.bfloat16), w_ref[gi],
                          preferred_element_type=jnp.float32))
    y = jnp.concatenate(ys, axis=-1)
    o_ref[0] = x + y * scale_ref[...]


def _pool_call(x, g, w, scale, tm=512):
    B, T, D = x.shape
    halo_blocks = tm // POOL_HALO
    return pl.pallas_call(
        functools.partial(_pool_kernel, tm=tm),
        grid=(B, T // tm),
        in_specs=[
            pl.BlockSpec((1, tm, D), lambda b, i: (b, i, 0)),
            pl.BlockSpec((1, POOL_HALO, D),
                         lambda b, i: (b, jnp.maximum(i * halo_blocks - 1, 0), 0)),
            _resident((1, D)),
            _resident(w.shape),
            _resident((1, D)),
        ],
        out_specs=pl.BlockSpec((1, tm, D), lambda b, i: (b, i, 0)),
        out_shape=jax.ShapeDtypeStruct(x.shape, x.dtype),
        scratch_shapes=[pltpu.VMEM((tm + POOL_HALO, D), jnp.float32)],
        compiler_params=pltpu.CompilerParams(
            dimension_semantics=("arbitrary", "arbitrary"),
            vmem_limit_bytes=VMEM_LIMIT),
        name="pool_mixer",
    )(x, x, g.reshape(1, D), w, scale.reshape(1, D))


def _ffn_kernel(x_ref, g_ref, wgu_ref, wd_ref, fg_ref, o_ref, act_ref, *, fc, final_norm):
    x = x_ref[...]
    h = _rms(x, g_ref[...]).astype(jnp.bfloat16)
    for c in range(D_FF // fc):
        gate = jnp.dot(h, wgu_ref[:, c * fc:(c + 1) * fc],
                       preferred_element_type=jnp.float32)
        up = jnp.dot(h, wgu_ref[:, D_FF + c * fc:D_FF + (c + 1) * fc],
                     preferred_element_type=jnp.float32)
        act = gate * (1.0 / (1.0 + jnp.exp(-gate))) * up
        act_ref[:, c * fc:(c + 1) * fc] = act.astype(jnp.bfloat16)
    y = x + jnp.dot(act_ref[...], wd_ref[...], preferred_element_type=jnp.float32)
    if final_norm:
        y = _rms(y, fg_ref[...])
    o_ref[...] = y


def _ffn_call(x, g, wgu, wd, final_g, final_norm, tm=1024, fc=256):
    B, T, D = x.shape
    xf = x.reshape(B * T, D)
    out = pl.pallas_call(
        functools.partial(_ffn_kernel, fc=fc, final_norm=final_norm),
        grid=(B * T // tm,),
        in_specs=[
            pl.BlockSpec((tm, D), lambda i: (i, 0)),
            _resident((1, D)),
            _resident(wgu.shape),
            _resident(wd.shape),
            _resident((1, D)),
        ],
        out_specs=pl.BlockSpec((tm, D), lambda i: (i, 0)),
        out_shape=jax.ShapeDtypeStruct(xf.shape, xf.dtype),
        scratch_shapes=[pltpu.VMEM((tm, D_FF), jnp.bfloat16)],
        compiler_params=pltpu.CompilerParams(
            dimension_semantics=("arbitrary",),
            vmem_limit_bytes=VMEM_LIMIT),
        name="swiglu_ffn",
    )(xf, g.reshape(1, D), wgu, wd, final_g.reshape(1, D))
    return out.reshape(B, T, D)


def _kv_kernel(x_ref, g_ref, wk_ref, wvt_ref, k_ref, vt_ref, km_ref):
    j = pl.program_id(1)
    h = _rms(x_ref[0], g_ref[...]).astype(jnp.bfloat16)
    k = jnp.dot(h, wk_ref[...], preferred_element_type=jnp.float32)
    k_ref[0] = k.astype(jnp.bfloat16)
    vt = lax.dot_general(wvt_ref[...], h, _NT, preferred_element_type=jnp.float32)
    vt_ref[0] = vt.astype(jnp.bfloat16)
    km_ref[0, pl.ds(j, 1), :] = jnp.mean(k, axis=0, keepdims=True)


def _kv_call(x, g, wk, wvt):
    B, T, D = x.shape
    nb = T // MOBA_BLOCK
    return pl.pallas_call(
        _kv_kernel,
        grid=(B, nb),
        in_specs=[
            pl.BlockSpec((1, MOBA_BLOCK, D), lambda b, j: (b, j, 0)),
            _resident((1, D)),
            _resident(wk.shape),
            _resident(wvt.shape),
        ],
        out_specs=[
            pl.BlockSpec((1, MOBA_BLOCK, D), lambda b, j: (b, j, 0)),
            pl.BlockSpec((1, D, MOBA_BLOCK), lambda b, j: (b, 0, j)),
            pl.BlockSpec((1, nb, D), lambda b, j: (b, 0, 0)),
        ],
        out_shape=[
            jax.ShapeDtypeStruct((B, T, D), jnp.bfloat16),
            jax.ShapeDtypeStruct((B, D, T), jnp.bfloat16),
            jax.ShapeDtypeStruct((B, nb, D), jnp.float32),
        ],
        compiler_params=pltpu.CompilerParams(
            dimension_semantics=("arbitrary", "arbitrary"),
            vmem_limit_bytes=VMEM_LIMIT),
        name="shared_kv",
    )(x, g.reshape(1, D), wk, wvt)


def _rel_bucket(n):
    max_exact = NUM_BUCKETS // 2
    nf = jnp.maximum(n, max_exact).astype(jnp.float32)
    large = max_exact + (jnp.log(nf / max_exact) / math.log(MAX_DISTANCE / max_exact)
                         * (NUM_BUCKETS - max_exact)).astype(jnp.int32)
    large = jnp.minimum(large, NUM_BUCKETS - 1)
    return jnp.where(n < max_exact, n, large)


def _bias_kernel(rb_ref, o_ref):
    hd = pl.program_id(0)
    kl = lax.broadcasted_iota(jnp.int32, (MOBA_BLOCK, MOBA_BLOCK), 0)
    ql = lax.broadcasted_iota(jnp.int32, (MOBA_BLOCK, MOBA_BLOCK), 1)
    d_own = ql - kl
    for ti, dist in enumerate((d_own, d_own + MOBA_BLOCK)):
        bucket = _rel_bucket(jnp.maximum(dist, 0))
        tile = jnp.zeros((MOBA_BLOCK, MOBA_BLOCK), jnp.float32)
        for bi in range(NUM_BUCKETS):
            tile = jnp.where(bucket == bi, rb_ref[bi, hd], tile)
        if ti == 0:
            tile = jnp.where(dist >= 0, tile, _MASKED)
        o_ref[0, ti] = tile * LOG2E


def _bias_call(rel_bias):
    return pl.pallas_call(
        _bias_kernel,
        grid=(N_HEADS,),
        in_specs=[pl.BlockSpec(memory_space=pltpu.SMEM)],
        out_specs=pl.BlockSpec((1, 2, MOBA_BLOCK, MOBA_BLOCK), lambda h: (h, 0, 0, 0)),
        out_shape=jax.ShapeDtypeStruct((N_HEADS, 2, MOBA_BLOCK, MOBA_BLOCK), jnp.float32),
        compiler_params=pltpu.CompilerParams(dimension_semantics=("arbitrary",)),
        name="rel_bias_tiles",
    )(rel_bias)


def _sublane_all(op, x):
    shift = SUBLANES // 2
    while shift:
        x = op(x, pltpu.roll(x, shift, axis=0))
        shift //= 2
    return x


def _attn_kernel(rb_ref, x_ref, g_ref, wqt_ref, wo_ref, k_ref, vt_ref, km_ref, bias_ref,
                 o_ref, qh_ref, mask_ref, m_ref, l_ref, acc_ref):
    first_own = pl.program_id(1) * Q_SUBTILES
    nb = km_ref.shape[1]
    tq = MOBA_BLOCK
    D = x_ref.shape[2]
    row_groups = MOBA_BLOCK // SUBLANES
    head_groups = HEAD_DIM // SUBLANES
    x = x_ref[0]
    h = _rms(x, g_ref[...]).astype(jnp.bfloat16)
    qf = lax.dot_general(wqt_ref[...], h, _NT, preferred_element_type=jnp.float32)
    q_gate = (qf * HEAD_DIM ** -0.5).astype(jnp.bfloat16)
    q = (qf * (HEAD_DIM ** -0.5 * LOG2E)).astype(jnp.bfloat16)

    feat = lax.broadcasted_iota(jnp.int32, (LANES, 1), 0)
    for sub in range(Q_SUBTILES):
        for head in range(N_HEADS):
            p, hh = divmod(head, HEADS_PER_LANE_TILE)
            qp = q[p * LANES:(p + 1) * LANES, sub * tq:(sub + 1) * tq]
            in_head = (feat >= hh * HEAD_DIM) & (feat < (hh + 1) * HEAD_DIM)
            qh_ref[sub, head] = jnp.where(in_head, qp, jnp.zeros_like(qp))

    n_q = Q_SUBTILES * tq
    km = jnp.broadcast_to(km_ref[0][:, None, :], (nb, N_HEADS, D)).reshape(nb * N_HEADS, D)
    row_head = lax.broadcasted_iota(jnp.int32, (nb * N_HEADS, D), 0) % N_HEADS
    col_head = lax.broadcasted_iota(jnp.int32, (nb * N_HEADS, D), 1) // HEAD_DIM
    km_bd = jnp.where(row_head == col_head, km, 0.0).astype(jnp.bfloat16)
    gate = jnp.dot(km_bd, q_gate, preferred_element_type=jnp.float32)
    own_of_query = first_own + lax.broadcasted_iota(jnp.int32, (N_HEADS, n_q), 1) // tq
    past = [own_of_query > j for j in range(nb)]
    gates = [jnp.where(past[j], gate[j * N_HEADS:(j + 1) * N_HEADS], _NEG_INF)
             for j in range(nb)]
    rank = [jnp.zeros((N_HEADS, n_q), jnp.int32) for _ in range(nb)]
    for lo in range(nb):
        for hi in range(lo + 1, nb):
            lo_wins = jnp.where(gates[lo] >= gates[hi], 1, 0)
            rank[hi] = rank[hi] + lo_wins
            rank[lo] = rank[lo] + (1 - lo_wins)
    for j in range(nb):
        keep = jnp.where(past[j], 0.0, _MASKED)
        mask_ref[j] = jnp.where(rank[j] < MOBA_TOPK, keep, _MASKED)

    m_ref[...] = jnp.full(m_ref.shape, _MASKED, jnp.float32)
    l_ref[...] = jnp.zeros(l_ref.shape, jnp.float32)
    acc_ref[...] = jnp.zeros(acc_ref.shape, jnp.float32)

    def attend(blocks):
        units = []
        for j, kinds in blocks:
            keys = pl.ds(pl.multiple_of(j * MOBA_BLOCK, MOBA_BLOCK), MOBA_BLOCK)
            units += [(j, keys, kind, sub, head) for sub, kind in enumerate(kinds)
                      if kind is not None for head in range(N_HEADS)]

        def scores(unit):
            _, keys, _, sub, head = unit
            p = head // HEADS_PER_LANE_TILE
            kj = k_ref[0, keys, p * LANES:(p + 1) * LANES]
            return jnp.dot(kj, qh_ref[sub, head], preferred_element_type=jnp.float32)

        def softmax(unit, s):
            j, _, kind, sub, head = unit
            s = s.reshape(row_groups, SUBLANES, tq)
            if kind == "own":
                s = s + bias_ref[head, 0].reshape(row_groups, SUBLANES, tq)
            elif kind == "prev":
                s = s + bias_ref[head, 1].reshape(row_groups, SUBLANES, tq)
            block_max = _sublane_all(jnp.maximum, jnp.max(s, axis=0))
            m_old = m_ref[sub, head]
            if kind == "own":
                m_new = jnp.maximum(m_old, block_max)
                shift = m_new
            else:
                row = mask_ref[j, head:head + 1, sub * tq:(sub + 1) * tq]
                if kind == "far":
                    row = row + rb_ref[NUM_BUCKETS - 1, head] * LOG2E
                row = jnp.broadcast_to(row, (SUBLANES, tq))
                m_new = jnp.maximum(m_old, block_max + row)
                shift = jnp.maximum(m_new - row, block_max)
            alpha = jnp.exp2(m_old - m_new)
            pj = jnp.exp2(s - shift[None])
            m_ref[sub, head] = m_new
            return pj.reshape(MOBA_BLOCK, tq).astype(jnp.bfloat16), alpha

        def accumulate(unit, pv, alpha):
            _, _, _, sub, head = unit
            hs = slice(head * HEAD_DIM, (head + 1) * HEAD_DIM)
            l_ref[sub, head] = l_ref[sub, head] * alpha + pv[HEAD_DIM:HEAD_DIM + SUBLANES]
            acc = acc_ref[sub, hs, :].reshape(head_groups, SUBLANES, tq)
            acc = acc * alpha[None] + pv[:HEAD_DIM].reshape(head_groups, SUBLANES, tq)
            acc_ref[sub, hs, :] = acc.reshape(HEAD_DIM, tq)

        ones = jnp.ones((ONES_ROWS, MOBA_BLOCK), jnp.bfloat16)

        def weighted_values(unit, pj):
            _, keys, _, _, head = unit
            vj = vt_ref[0, head * HEAD_DIM:(head + 1) * HEAD_DIM, keys]
            return jnp.dot(jnp.concatenate([vj, ones], axis=0), pj,
                           preferred_element_type=jnp.float32)

        n = len(units)
        pending_scores = {i: scores(units[i]) for i in range(SCORE_LOOKAHEAD)}
        probs = {}
        products = {}
        for step in range(n + 2):
            if step + SCORE_LOOKAHEAD < n:
                pending_scores[step + SCORE_LOOKAHEAD] = scores(units[step + SCORE_LOOKAHEAD])
            if 0 <= step - 1 < n:
                pj, alpha = probs.pop(step - 1)
                products[step - 1] = (weighted_values(units[step - 1], pj), alpha)
            if step < n:
                probs[step] = softmax(units[step], pending_scores.pop(step))
            if 0 <= step - 2 < n:
                accumulate(units[step - 2], *products.pop(step - 2))

    assert Q_SUBTILES == 2 and FAR_UNROLL == 2
    all_far = ("far", "far")
    tail = [(first_own, ("own", "prev")), (first_own + 1, (None, "own"))]

    def far_pair(t, carry):
        attend([(FAR_UNROLL * t, all_far), (FAR_UNROLL * t + 1, all_far)])
        return carry

    lax.fori_loop(0, jnp.maximum(first_own - 1, 0) // FAR_UNROLL, far_pair, 0)

    @pl.when(first_own > 0)
    def _():
        attend([(first_own - 2, all_far), (first_own - 1, ("prev", "far"))] + tail)

    @pl.when(first_own == 0)
    def _():
        attend(tail)

    outs = []
    for sub in range(Q_SUBTILES):
        for head in range(N_HEADS):
            hs = slice(head * HEAD_DIM, (head + 1) * HEAD_DIM)
            inv = 1.0 / l_ref[sub, head]
            acc = acc_ref[sub, hs, :].reshape(head_groups, SUBLANES, tq) * inv[None]
            acc_ref[sub, hs, :] = acc.reshape(HEAD_DIM, tq)
        outs.append(acc_ref[sub].T.astype(jnp.bfloat16))
    o = jnp.concatenate(outs, axis=0)
    o_ref[0] = x + jnp.dot(o, wo_ref[...], preferred_element_type=jnp.float32)


def _attn_call(x, g, wqt, wo, k, vt, km, bias_tiles, rel_bias):
    B, T, D = x.shape
    nb = T // MOBA_BLOCK
    tq = MOBA_BLOCK
    n_q = Q_SUBTILES * tq
    return pl.pallas_call(
        _attn_kernel,
        grid=(B, T // n_q),
        in_specs=[
            pl.BlockSpec(memory_space=pltpu.SMEM),
            pl.BlockSpec((1, n_q, D), lambda b, i: (b, i, 0)),
            _resident((1, D)),
            _resident(wqt.shape),
            _resident(wo.shape),
            pl.BlockSpec((1, T, D), lambda b, i: (b, 0, 0)),
            pl.BlockSpec((1, D, T), lambda b, i: (b, 0, 0)),
            pl.BlockSpec((1, nb, D), lambda b, i: (b, 0, 0)),
            _resident(bias_tiles.shape),
        ],
        out_specs=pl.BlockSpec((1, n_q, D), lambda b, i: (b, i, 0)),
        out_shape=jax.ShapeDtypeStruct(x.shape, x.dtype),
        scratch_shapes=[
            pltpu.VMEM((Q_SUBTILES, N_HEADS, LANES, tq), jnp.bfloat16),
            pltpu.VMEM((nb, N_HEADS, n_q), jnp.float32),
            pltpu.VMEM((Q_SUBTILES, N_HEADS, SUBLANES, tq), jnp.float32),
            pltpu.VMEM((Q_SUBTILES, N_HEADS, SUBLANES, tq), jnp.float32),
            pltpu.VMEM((Q_SUBTILES, D, tq), jnp.float32),
        ],
        compiler_params=pltpu.CompilerParams(
            dimension_semantics=("arbitrary", "arbitrary"),
            vmem_limit_bytes=VMEM_LIMIT),
        name="moba_attention",
    )(rel_bias, x, g.reshape(1, D), wqt, wo, k, vt, km, bias_tiles)


def kernel(x, norm_mixer, norm_ffn, pool_w, pool_scale, kv_norm, w_kv, w_q, w_o,
           rel_bias, w_gate_up, w_down, final_norm):
    bf = jnp.bfloat16
    assert x.shape[1] % MOBA_BLOCK == 0 and x.shape[2] == D_MODEL
    pool_w = pool_w.astype(bf)
    w_gate_up = w_gate_up.astype(bf)
    w_down = w_down.astype(bf)
    w_qt = jnp.swapaxes(w_q, 1, 2).astype(bf)
    w_o = w_o.astype(bf)
    wk = w_kv[:, :D_MODEL].astype(bf)
    wvt = w_kv[:, D_MODEL:].T.astype(bf)

    k = vt = km = bias_tiles = None
    for layer in range(DEPTH):
        if layer < N_A_LAYERS:
            x = _pool_call(x, norm_mixer[layer], pool_w[layer], pool_scale[layer])
        else:
            j = layer - N_A_LAYERS
            if j == 0:
                k, vt, km = _kv_call(x, kv_norm, wk, wvt)
                bias_tiles = _bias_call(rel_bias)
            x = _attn_call(x, norm_mixer[layer], w_qt[j], w_o[j], k, vt, km, bias_tiles, rel_bias)
        x = _ffn_call(x, norm_ffn[layer], w_gate_up[layer], w_down[layer], final_norm,
                      final_norm=(layer == DEPTH - 1))
    return x
```

```python
import functools
import math

import jax
import jax.numpy as jnp
from jax import lax
from jax.experimental import pallas as pl
from jax.experimental.pallas import tpu as pltpu

D_MODEL = 1024
DEPTH = 4
N_A_LAYERS = DEPTH // 2
POOL_WINDOWS = (2, 4, 8, 16)
POOL_GROUP = D_MODEL // len(POOL_WINDOWS)
HEAD_DIM = 64
N_HEADS = D_MODEL // HEAD_DIM
MOBA_BLOCK = 256
MOBA_TOPK = 3
NUM_BUCKETS = 32
MAX_DISTANCE = 128
D_FF = -(-8 * D_MODEL // (3 * 256)) * 256
EPS = 1e-6

LANES = 128
SUBLANES = 8
POOL_HALO = 16
HEADS_PER_LANE_TILE = LANES // HEAD_DIM
VMEM_LIMIT = 56 * 1024 * 1024
SCORE_LOOKAHEAD = 3
FAR_UNROLL = 2
Q_SUBTILES = 2

_NT = (((1,), (1,)), ((), ()))
_NEG_INF = float("-inf")
_MASKED = -1e30
LOG2E = math.log2(math.e)
ONES_ROWS = 16


def _rms(x, g):
    return x * lax.rsqrt(jnp.mean(x * x, axis=-1, keepdims=True) + EPS) * g


def _resident(shape):
    zeros = (0,) * len(shape)
    return pl.BlockSpec(shape, lambda *_: zeros, pipeline_mode=pl.Buffered(1))


def _resident_layer(stacked_shape, layer):
    index = (layer,) + (0,) * (len(stacked_shape) - 1)
    return pl.BlockSpec((None,) + tuple(stacked_shape[1:]), lambda *_: index,
                        pipeline_mode=pl.Buffered(1))


def _pool_kernel(x_ref, halo_ref, g_ref, w_ref, scale_ref, o_ref, hh_ref, *, tm):
    i = pl.program_id(1)
    g = g_ref[...]
    x = x_ref[0]
    h = _rms(x, g)
    halo = _rms(halo_ref[0], g)
    hh_ref[0:POOL_HALO, :] = jnp.where(i > 0, halo, 0.0)
    hh_ref[POOL_HALO:, :] = h
    t = i * tm + lax.broadcasted_iota(jnp.int32, (tm, 1), 0)
    ys = []
    for gi, win in enumerate(POOL_WINDOWS):
        sl = slice(gi * POOL_GROUP, (gi + 1) * POOL_GROUP)
        wsum = hh_ref[:, sl]
        span = 1
        while span < win:
            wsum = wsum + pltpu.roll(wsum, span, axis=0)
            span *= 2
        wsum = wsum[POOL_HALO:]
        cnt = jnp.minimum(t + 1, win).astype(jnp.float32)
        d = wsum / cnt - h[:, sl]
        ys.append(jnp.dot(d.astype(jnp.bfloat16), w_ref[gi],
                          preferred_element_type=jnp.float32))
    y = jnp.concatenate(ys, axis=-1)
    o_ref[0] = x + y * scale_ref[...]


def _pool_call(x, g, w, layer, scale, tm=512):
    B, T, D = x.shape
    halo_blocks = tm // POOL_HALO
    return pl.pallas_call(
        functools.partial(_pool_kernel, tm=tm),
        grid=(B, T // tm),
        in_specs=[
            pl.BlockSpec((1, tm, D), lambda b, i: (b, i, 0)),
            pl.BlockSpec((1, POOL_HALO, D),
                         lambda b, i: (b, jnp.maximum(i * halo_blocks - 1, 0), 0)),
            _resident((1, D)),
            _resident_layer(w.shape, layer),
            _resident((1, D)),
        ],
        out_specs=pl.BlockSpec((1, tm, D), lambda b, i: (b, i, 0)),
        out_shape=jax.ShapeDtypeStruct(x.shape, x.dtype),
        scratch_shapes=[pltpu.VMEM((tm + POOL_HALO, D), jnp.float32)],
        compiler_params=pltpu.CompilerParams(
            dimension_semantics=("arbitrary", "arbitrary"),
            vmem_limit_bytes=VMEM_LIMIT),
        name="pool_mixer",
    )(x, x, g.reshape(1, D), w, scale.reshape(1, D))


def _ffn_kernel(x_ref, g_ref, wgu_ref, wd_ref, fg_ref, o_ref, act_ref, *, fc, final_norm):
    x = x_ref[...]
    h = _rms(x, g_ref[...]).astype(jnp.bfloat16)
    for c in range(D_FF // fc):
        gate = jnp.dot(h, wgu_ref[:, c * fc:(c + 1) * fc],
                       preferred_element_type=jnp.float32)
        up = jnp.dot(h, wgu_ref[:, D_FF + c * fc:D_FF + (c + 1) * fc],
                     preferred_element_type=jnp.float32)
        act = gate * (1.0 / (1.0 + jnp.exp(-gate))) * up
        act_ref[:, c * fc:(c + 1) * fc] = act.astype(jnp.bfloat16)
    y = x + jnp.dot(act_ref[...], wd_ref[...], preferred_element_type=jnp.float32)
    if final_norm:
        y = _rms(y, fg_ref[...])
    o_ref[...] = y


def _ffn_call(x, g, wgu, wd, layer, final_g, final_norm, tm=1024, fc=256):
    B, T, D = x.shape
    xf = x.reshape(B * T, D)
    out = pl.pallas_call(
        functools.partial(_ffn_kernel, fc=fc, final_norm=final_norm),
        grid=(B * T // tm,),
        in_specs=[
            pl.BlockSpec((tm, D), lambda i: (i, 0)),
            _resident((1, D)),
            _resident_layer(wgu.shape, layer),
            _resident_layer(wd.shape, layer),
            _resident((1, D)),
        ],
        out_specs=pl.BlockSpec((tm, D), lambda i: (i, 0)),
        out_shape=jax.ShapeDtypeStruct(xf.shape, xf.dtype),
        scratch_shapes=[pltpu.VMEM((tm, D_FF), jnp.bfloat16)],
        compiler_params=pltpu.CompilerParams(
            dimension_semantics=("arbitrary",),
            vmem_limit_bytes=VMEM_LIMIT),
        name="swiglu_ffn",
    )(xf, g.reshape(1, D), wgu, wd, final_g.reshape(1, D))
    return out.reshape(B, T, D)


def _kv_kernel(x_ref, g_ref, wk_ref, wvt_ref, k_ref, vt_ref, km_ref, *, blocks_per_tile):
    j = pl.program_id(1)
    h = _rms(x_ref[0], g_ref[...]).astype(jnp.bfloat16)
    k = jnp.dot(h, wk_ref[...], preferred_element_type=jnp.float32)
    k_ref[0] = k.astype(jnp.bfloat16)
    vt = lax.dot_general(wvt_ref[...], h, _NT, preferred_element_type=jnp.float32)
    vt_ref[0] = vt.astype(jnp.bfloat16)
    for i in range(blocks_per_tile):
        km_ref[0, pl.ds(j * blocks_per_tile + i, 1), :] = jnp.mean(
            k[i * MOBA_BLOCK:(i + 1) * MOBA_BLOCK], axis=0, keepdims=True)


def _kv_call(x, g, wk, wvt, tm=1024):
    B, T, D = x.shape
    nb = T // MOBA_BLOCK
    return pl.pallas_call(
        functools.partial(_kv_kernel, blocks_per_tile=tm // MOBA_BLOCK),
        grid=(B, T // tm),
        in_specs=[
            pl.BlockSpec((1, tm, D), lambda b, j: (b, j, 0)),
            _resident((1, D)),
            _resident(wk.shape),
            _resident(wvt.shape),
        ],
        out_specs=[
            pl.BlockSpec((1, tm, D), lambda b, j: (b, j, 0)),
            pl.BlockSpec((1, D, tm), lambda b, j: (b, 0, j)),
            pl.BlockSpec((1, nb, D), lambda b, j: (b, 0, 0)),
        ],
        out_shape=[
            jax.ShapeDtypeStruct((B, T, D), jnp.bfloat16),
            jax.ShapeDtypeStruct((B, D, T), jnp.bfloat16),
            jax.ShapeDtypeStruct((B, nb, D), jnp.float32),
        ],
        compiler_params=pltpu.CompilerParams(
            dimension_semantics=("arbitrary", "arbitrary"),
            vmem_limit_bytes=VMEM_LIMIT),
        name="shared_kv",
    )(x, g.reshape(1, D), wk, wvt)


def _rel_bucket(n):
    max_exact = NUM_BUCKETS // 2
    nf = jnp.maximum(n, max_exact).astype(jnp.float32)
    large = max_exact + (jnp.log(nf / max_exact) / math.log(MAX_DISTANCE / max_exact)
                         * (NUM_BUCKETS - max_exact)).astype(jnp.int32)
    large = jnp.minimum(large, NUM_BUCKETS - 1)
    return jnp.where(n < max_exact, n, large)


def _bias_kernel(rb_ref, o_ref):
    hd = pl.program_id(0)
    kl = lax.broadcasted_iota(jnp.int32, (MOBA_BLOCK, MOBA_BLOCK), 0)
    ql = lax.broadcasted_iota(jnp.int32, (MOBA_BLOCK, MOBA_BLOCK), 1)
    d_own = ql - kl
    for ti, dist in enumerate((d_own, d_own + MOBA_BLOCK)):
        bucket = _rel_bucket(jnp.maximum(dist, 0))
        tile = jnp.zeros((MOBA_BLOCK, MOBA_BLOCK), jnp.float32)
        for bi in range(NUM_BUCKETS):
            tile = jnp.where(bucket == bi, rb_ref[bi, hd], tile)
        if ti == 0:
            tile = jnp.where(dist >= 0, tile, _MASKED)
        o_ref[0, ti] = tile * LOG2E


def _bias_call(rel_bias):
    return pl.pallas_call(
        _bias_kernel,
        grid=(N_HEADS,),
        in_specs=[pl.BlockSpec(memory_space=pltpu.SMEM)],
        out_specs=pl.BlockSpec((1, 2, MOBA_BLOCK, MOBA_BLOCK), lambda h: (h, 0, 0, 0)),
        out_shape=jax.ShapeDtypeStruct((N_HEADS, 2, MOBA_BLOCK, MOBA_BLOCK), jnp.float32),
        compiler_params=pltpu.CompilerParams(dimension_semantics=("arbitrary",)),
        name="rel_bias_tiles",
    )(rel_bias)


def _sublane_all(op, x):
    shift = SUBLANES // 2
    while shift:
        x = op(x, pltpu.roll(x, shift, axis=0))
        shift //= 2
    return x


def _attn_kernel(rb_ref, x_ref, g_ref, wqt_ref, wo_ref, k_ref, vt_ref, km_ref, bias_ref,
                 o_ref, qh_ref, mask_ref, m_ref, l_ref, acc_ref):
    first_own = pl.program_id(1) * Q_SUBTILES
    nb = km_ref.shape[1]
    tq = MOBA_BLOCK
    D = x_ref.shape[2]
    row_groups = MOBA_BLOCK // SUBLANES
    head_groups = HEAD_DIM // SUBLANES
    x = x_ref[0]
    h = _rms(x, g_ref[...]).astype(jnp.bfloat16)
    qf = lax.dot_general(wqt_ref[...], h, _NT, preferred_element_type=jnp.float32)
    q_gate = (qf * HEAD_DIM ** -0.5).astype(jnp.bfloat16)
    q = (qf * (HEAD_DIM ** -0.5 * LOG2E)).astype(jnp.bfloat16)

    feat = lax.broadcasted_iota(jnp.int32, (LANES, 1), 0)
    for sub in range(Q_SUBTILES):
        for head in range(N_HEADS):
            p, hh = divmod(head, HEADS_PER_LANE_TILE)
            qp = q[p * LANES:(p + 1) * LANES, sub * tq:(sub + 1) * tq]
            in_head = (feat >= hh * HEAD_DIM) & (feat < (hh + 1) * HEAD_DIM)
            qh_ref[sub, head] = jnp.where(in_head, qp, jnp.zeros_like(qp))

    n_q = Q_SUBTILES * tq
    km = jnp.broadcast_to(km_ref[0][:, None, :], (nb, N_HEADS, D)).reshape(nb * N_HEADS, D)
    row_head = lax.broadcasted_iota(jnp.int32, (nb * N_HEADS, D), 0) % N_HEADS
    col_head = lax.broadcasted_iota(jnp.int32, (nb * N_HEADS, D), 1) // HEAD_DIM
    km_bd = jnp.where(row_head == col_head, km, 0.0).astype(jnp.bfloat16)
    gate = jnp.dot(km_bd, q_gate, preferred_element_type=jnp.float32)
    own_of_query = first_own + lax.broadcasted_iota(jnp.int32, (N_HEADS, n_q), 1) // tq
    past = [own_of_query > j for j in range(nb)]
    gates = [jnp.where(past[j], gate[j * N_HEADS:(j + 1) * N_HEADS], _NEG_INF)
             for j in range(nb)]
    rank = [jnp.zeros((N_HEADS, n_q), jnp.int32) for _ in range(nb)]
    for lo in range(nb):
        for hi in range(lo + 1, nb):
            lo_wins = jnp.where(gates[lo] >= gates[hi], 1, 0)
            rank[hi] = rank[hi] + lo_wins
            rank[lo] = rank[lo] + (1 - lo_wins)
    for j in range(nb):
        keep = jnp.where(past[j], 0.0, _MASKED)
        mask_ref[j] = jnp.where(rank[j] < MOBA_TOPK, keep, _MASKED)

    m_ref[...] = jnp.full(m_ref.shape, _MASKED, jnp.float32)
    l_ref[...] = jnp.zeros(l_ref.shape, jnp.float32)
    acc_ref[...] = jnp.zeros(acc_ref.shape, jnp.float32)

    def attend(blocks):
        units = []
        for j, kinds in blocks:
            keys = pl.ds(pl.multiple_of(j * MOBA_BLOCK, MOBA_BLOCK), MOBA_BLOCK)
            units += [(j, keys, kind, sub, head) for sub, kind in enumerate(kinds)
                      if kind is not None for head in range(N_HEADS)]

        def scores(unit):
            _, keys, _, sub, head = unit
            p = head // HEADS_PER_LANE_TILE
            kj = k_ref[0, keys, p * LANES:(p + 1) * LANES]
            return jnp.dot(kj, qh_ref[sub, head], preferred_element_type=jnp.float32)

        def softmax(unit, s):
            j, _, kind, sub, head = unit
            s = s.reshape(row_groups, SUBLANES, tq)
            if kind == "own":
                s = s + bias_ref[head, 0].reshape(row_groups, SUBLANES, tq)
            elif kind == "prev":
                s = s + bias_ref[head, 1].reshape(row_groups, SUBLANES, tq)
            block_max = _sublane_all(jnp.maximum, jnp.max(s, axis=0))
            m_old = m_ref[sub, head]
            if kind == "own":
                m_new = jnp.maximum(m_old, block_max)
                shift = m_new
            else:
                row = mask_ref[j, head:head + 1, sub * tq:(sub + 1) * tq]
                if kind == "far":
                    row = row + rb_ref[NUM_BUCKETS - 1, head] * LOG2E
                row = jnp.broadcast_to(row, (SUBLANES, tq))
                m_new = jnp.maximum(m_old, block_max + row)
                shift = jnp.maximum(m_new - row, block_max)
            alpha = jnp.exp2(m_old - m_new)
            pj = jnp.exp2(s - shift[None])
            m_ref[sub, head] = m_new
            return pj.reshape(MOBA_BLOCK, tq).astype(jnp.bfloat16), alpha

        def accumulate(unit, pv, alpha):
            _, _, _, sub, head = unit
            hs = slice(head * HEAD_DIM, (head + 1) * HEAD_DIM)
            l_ref[sub, head] = l_ref[sub, head] * alpha + pv[HEAD_DIM:HEAD_DIM + SUBLANES]
            acc = acc_ref[sub, hs, :].reshape(head_groups, SUBLANES, tq)
            acc = acc * alpha[None] + pv[:HEAD_DIM].reshape(head_groups, SUBLANES, tq)
            acc_ref[sub, hs, :] = acc.reshape(HEAD_DIM, tq)

        ones = jnp.ones((ONES_ROWS, MOBA_BLOCK), jnp.bfloat16)

        def weighted_values(unit, pj):
            _, keys, _, _, head = unit
            vj = vt_ref[0, head * HEAD_DIM:(head + 1) * HEAD_DIM, keys]
            return jnp.dot(jnp.concatenate([vj, ones], axis=0), pj,
                           preferred_element_type=jnp.float32)

        n = len(units)
        pending_scores = {i: scores(units[i]) for i in range(SCORE_LOOKAHEAD)}
        probs = {}
        products = {}
        for step in range(n + 2):
            if step + SCORE_LOOKAHEAD < n:
                pending_scores[step + SCORE_LOOKAHEAD] = scores(units[step + SCORE_LOOKAHEAD])
            if 0 <= step - 1 < n:
                pj, alpha = probs.pop(step - 1)
                products[step - 1] = (weighted_values(units[step - 1], pj), alpha)
            if step < n:
                probs[step] = softmax(units[step], pending_scores.pop(step))
            if 0 <= step - 2 < n:
                accumulate(units[step - 2], *products.pop(step - 2))

    assert Q_SUBTILES == 2 and FAR_UNROLL == 2
    all_far = ("far", "far")
    tail = [(first_own, ("own", "prev")), (first_own + 1, (None, "own"))]

    def far_pair(t, carry):
        attend([(FAR_UNROLL * t, all_far), (FAR_UNROLL * t + 1, all_far)])
        return carry

    lax.fori_loop(0, jnp.maximum(first_own - 1, 0) // FAR_UNROLL, far_pair, 0)

    @pl.when(first_own > 0)
    def _():
        attend([(first_own - 2, all_far), (first_own - 1, ("prev", "far"))] + tail)

    @pl.when(first_own == 0)
    def _():
        attend(tail)

    outs = []
    for sub in range(Q_SUBTILES):
        for head in range(N_HEADS):
            hs = slice(head * HEAD_DIM, (head + 1) * HEAD_DIM)
            inv = 1.0 / l_ref[sub, head]
            acc = acc_ref[sub, hs, :].reshape(head_groups, SUBLANES, tq) * inv[None]
            acc_ref[sub, hs, :] = acc.reshape(HEAD_DIM, tq)
        outs.append(acc_ref[sub].T.astype(jnp.bfloat16))
    o = jnp.concatenate(outs, axis=0)
    o_ref[0] = x + jnp.dot(o, wo_ref[...], preferred_element_type=jnp.float32)


def _attn_call(x, g, wqt, wo, layer, k, vt, km, bias_tiles, rel_bias):
    B, T, D = x.shape
    nb = T // MOBA_BLOCK
    tq = MOBA_BLOCK
    n_q = Q_SUBTILES * tq
    return pl.pallas_call(
        _attn_kernel,
        grid=(B, T // n_q),
        in_specs=[
            pl.BlockSpec(memory_space=pltpu.SMEM),
            pl.BlockSpec((1, n_q, D), lambda b, i: (b, i, 0)),
            _resident((1, D)),
            _resident_layer(wqt.shape, layer),
            _resident_layer(wo.shape, layer),
            pl.BlockSpec((1, T, D), lambda b, i: (b, 0, 0)),
            pl.BlockSpec((1, D, T), lambda b, i: (b, 0, 0)),
            pl.BlockSpec((1, nb, D), lambda b, i: (b, 0, 0)),
            _resident(bias_tiles.shape),
        ],
        out_specs=pl.BlockSpec((1, n_q, D), lambda b, i: (b, i, 0)),
        out_shape=jax.ShapeDtypeStruct(x.shape, x.dtype),
        scratch_shapes=[
            pltpu.VMEM((Q_SUBTILES, N_HEADS, LANES, tq), jnp.bfloat16),
            pltpu.VMEM((nb, N_HEADS, n_q), jnp.float32),
            pltpu.VMEM((Q_SUBTILES, N_HEADS, SUBLANES, tq), jnp.float32),
            pltpu.VMEM((Q_SUBTILES, N_HEADS, SUBLANES, tq), jnp.float32),
            pltpu.VMEM((Q_SUBTILES, D, tq), jnp.float32),
        ],
        compiler_params=pltpu.CompilerParams(
            dimension_semantics=("arbitrary", "arbitrary"),
            vmem_limit_bytes=VMEM_LIMIT),
        name="moba_attention",
    )(rel_bias, x, g.reshape(1, D), wqt, wo, k, vt, km, bias_tiles)


def kernel(x, norm_mixer, norm_ffn, pool_w, pool_scale, kv_norm, w_kv, w_q, w_o,
           rel_bias, w_gate_up, w_down, final_norm):
    bf = jnp.bfloat16
    assert x.shape[1] % MOBA_BLOCK == 0 and x.shape[2] == D_MODEL
    pool_w = pool_w.astype(bf)
    w_gate_up = w_gate_up.astype(bf)
    w_down = w_down.astype(bf)
    w_qt = jnp.swapaxes(w_q, 1, 2).astype(bf)
    w_o = w_o.astype(bf)
    wk = w_kv[:, :D_MODEL].astype(bf)
    wvt = w_kv[:, D_MODEL:].T.astype(bf)

    k = vt = km = bias_tiles = None
    for layer in range(DEPTH):
        if layer < N_A_LAYERS:
            x = _pool_call(x, norm_mixer[layer], pool_w, layer, pool_scale[layer])
        else:
            j = layer - N_A_LAYERS
            if j == 0:
                k, vt, km = _kv_call(x, kv_norm, wk, wvt)
                bias_tiles = _bias_call(rel_bias)
            x = _attn_call(x, norm_mixer[layer], w_qt, w_o, j, k, vt, km, bias_tiles, rel_bias)
        x = _ffn_call(x, norm_ffn[layer], w_gate_up, w_down, layer, final_norm,
                      final_norm=(layer == DEPTH - 1))
    return x
```

```python
import functools
import math

import jax
import jax.numpy as jnp
from jax import lax
from jax.experimental import pallas as pl
from jax.experimental.pallas import tpu as pltpu

D_MODEL = 1024
DEPTH = 4
N_A_LAYERS = DEPTH // 2
POOL_WINDOWS = (2, 4, 8, 16)
POOL_GROUP = D_MODEL // len(POOL_WINDOWS)
HEAD_DIM = 64
N_HEADS = D_MODEL // HEAD_DIM
MOBA_BLOCK = 256
MOBA_TOPK = 3
NUM_BUCKETS = 32
MAX_DISTANCE = 128
D_FF = -(-8 * D_MODEL // (3 * 256)) * 256
EPS = 1e-6

LANES = 128
SUBLANES = 8
POOL_HALO = 16
assert all(w & (w - 1) == 0 and w <= POOL_HALO for w in POOL_WINDOWS)
HEADS_PER_LANE_TILE = LANES // HEAD_DIM
VMEM_LIMIT = 56 * 1024 * 1024
SCORE_LOOKAHEAD = 4
FAR_UNROLL = 2
Q_SUBTILES = 2

_NT = (((1,), (1,)), ((), ()))
_NEG_INF = float("-inf")
_MASKED = -1e30
LOG2E = math.log2(math.e)
ONES_ROWS = 16


def _rms(x, g):
    return x * lax.rsqrt(jnp.mean(x * x, axis=-1, keepdims=True) + EPS) * g


def _resident(shape):
    zeros = (0,) * len(shape)
    return pl.BlockSpec(shape, lambda *_: zeros, pipeline_mode=pl.Buffered(1))


def _resident_layer(stacked_shape, layer):
    index = (layer,) + (0,) * (len(stacked_shape) - 1)
    return pl.BlockSpec((None,) + tuple(stacked_shape[1:]), lambda *_: index,
                        pipeline_mode=pl.Buffered(1))


def _pool_kernel(x_ref, halo_ref, g_ref, w_ref, scale_ref, o_ref, hh_ref, *, tm):
    i = pl.program_id(1)
    g = g_ref[...]
    x = x_ref[0]
    h = _rms(x, g)
    halo = _rms(halo_ref[0], g)
    hh_ref[0:POOL_HALO, :] = jnp.where(i > 0, halo, 0.0)
    hh_ref[POOL_HALO:, :] = h
    t = i * tm + lax.broadcasted_iota(jnp.int32, (tm, 1), 0)
    ys = []
    for gi, win in enumerate(POOL_WINDOWS):
        sl = slice(gi * POOL_GROUP, (gi + 1) * POOL_GROUP)
        wsum = hh_ref[:, sl]
        span = 1
        while span < win:
            wsum = wsum + pltpu.roll(wsum, span, axis=0)
            span *= 2
        wsum = wsum[POOL_HALO:]
        cnt = jnp.minimum(t + 1, win).astype(jnp.float32)
        d = wsum / cnt - h[:, sl]
        ys.append(jnp.dot(d.astype(jnp.bfloat16), w_ref[gi],
                          preferred_element_type=jnp.float32))
    y = jnp.concatenate(ys, axis=-1)
    o_ref[0] = x + y * scale_ref[...]


def _pool_call(x, g, w, layer, scale, tm=512):
    B, T, D = x.shape
    halo_blocks = tm // POOL_HALO
    return pl.pallas_call(
        functools.partial(_pool_kernel, tm=tm),
        grid=(B, T // tm),
        in_specs=[
            pl.BlockSpec((1, tm, D), lambda b, i: (b, i, 0)),
            pl.BlockSpec((1, POOL_HALO, D),
                         lambda b, i: (b, jnp.maximum(i * halo_blocks - 1, 0), 0)),
            _resident((1, D)),
            _resident_layer(w.shape, layer),
            _resident((1, D)),
        ],
        out_specs=pl.BlockSpec((1, tm, D), lambda b, i: (b, i, 0)),
        out_shape=jax.ShapeDtypeStruct(x.shape, x.dtype),
        scratch_shapes=[pltpu.VMEM((tm + POOL_HALO, D), jnp.float32)],
        compiler_params=pltpu.CompilerParams(
            dimension_semantics=("arbitrary", "arbitrary"),
            vmem_limit_bytes=VMEM_LIMIT),
        name="pool_mixer",
    )(x, x, g.reshape(1, D), w, scale.reshape(1, D))


def _ffn_kernel(x_ref, g_ref, wgu_ref, wd_ref, fg_ref, o_ref, act_ref, *, fc, final_norm):
    x = x_ref[...]
    h = _rms(x, g_ref[...]).astype(jnp.bfloat16)
    for c in range(D_FF // fc):
        gate = jnp.dot(h, wgu_ref[:, c * fc:(c + 1) * fc],
                       preferred_element_type=jnp.float32)
        up = jnp.dot(h, wgu_ref[:, D_FF + c * fc:D_FF + (c + 1) * fc],
                     preferred_element_type=jnp.float32)
        act = gate * (1.0 / (1.0 + jnp.exp(-gate))) * up
        act_ref[:, c * fc:(c + 1) * fc] = act.astype(jnp.bfloat16)
    y = x + jnp.dot(act_ref[...], wd_ref[...], preferred_element_type=jnp.float32)
    if final_norm:
        y = _rms(y, fg_ref[...])
    o_ref[...] = y


def _ffn_call(x, g, wgu, wd, layer, final_g, final_norm, tm=1024, fc=256):
    B, T, D = x.shape
    xf = x.reshape(B * T, D)
    out = pl.pallas_call(
        functools.partial(_ffn_kernel, fc=fc, final_norm=final_norm),
        grid=(B * T // tm,),
        in_specs=[
            pl.BlockSpec((tm, D), lambda i: (i, 0)),
            _resident((1, D)),
            _resident_layer(wgu.shape, layer),
            _resident_layer(wd.shape, layer),
            _resident((1, D)),
        ],
        out_specs=pl.BlockSpec((tm, D), lambda i: (i, 0)),
        out_shape=jax.ShapeDtypeStruct(xf.shape, xf.dtype),
        scratch_shapes=[pltpu.VMEM((tm, D_FF), jnp.bfloat16)],
        compiler_params=pltpu.CompilerParams(
            dimension_semantics=("arbitrary",),
            vmem_limit_bytes=VMEM_LIMIT),
        name="swiglu_ffn",
    )(xf, g.reshape(1, D), wgu, wd, final_g.reshape(1, D))
    return out.reshape(B, T, D)


def _kv_kernel(x_ref, g_ref, wk_ref, wvt_ref, k_ref, vt_ref, km_ref, *, blocks_per_tile):
    j = pl.program_id(1)
    h = _rms(x_ref[0], g_ref[...]).astype(jnp.bfloat16)
    k = jnp.dot(h, wk_ref[...], preferred_element_type=jnp.float32)
    k_ref[0] = k.astype(jnp.bfloat16)
    vt = lax.dot_general(wvt_ref[...], h, _NT, preferred_element_type=jnp.float32)
    vt_ref[0] = vt.astype(jnp.bfloat16)
    for i in range(blocks_per_tile):
        km_ref[0, pl.ds(j * blocks_per_tile + i, 1), :] = jnp.mean(
            k[i * MOBA_BLOCK:(i + 1) * MOBA_BLOCK], axis=0, keepdims=True)


def _kv_call(x, g, wk, wvt, tm=1024):
    B, T, D = x.shape
    nb = T // MOBA_BLOCK
    return pl.pallas_call(
        functools.partial(_kv_kernel, blocks_per_tile=tm // MOBA_BLOCK),
        grid=(B, T // tm),
        in_specs=[
            pl.BlockSpec((1, tm, D), lambda b, j: (b, j, 0)),
            _resident((1, D)),
            _resident(wk.shape),
            _resident(wvt.shape),
        ],
        out_specs=[
            pl.BlockSpec((1, tm, D), lambda b, j: (b, j, 0)),
            pl.BlockSpec((1, D, tm), lambda b, j: (b, 0, j)),
            pl.BlockSpec((1, nb, D), lambda b, j: (b, 0, 0)),
        ],
        out_shape=[
            jax.ShapeDtypeStruct((B, T, D), jnp.bfloat16),
            jax.ShapeDtypeStruct((B, D, T), jnp.bfloat16),
            jax.ShapeDtypeStruct((B, nb, D), jnp.float32),
        ],
        compiler_params=pltpu.CompilerParams(
            dimension_semantics=("arbitrary", "arbitrary"),
            vmem_limit_bytes=VMEM_LIMIT),
        name="shared_kv",
    )(x, g.reshape(1, D), wk, wvt)


def _rel_bucket(n):
    max_exact = NUM_BUCKETS // 2
    nf = jnp.maximum(n, max_exact).astype(jnp.float32)
    large = max_exact + (jnp.log(nf / max_exact) / math.log(MAX_DISTANCE / max_exact)
                         * (NUM_BUCKETS - max_exact)).astype(jnp.int32)
    large = jnp.minimum(large, NUM_BUCKETS - 1)
    return jnp.where(n < max_exact, n, large)


def _bias_kernel(rb_ref, o_ref):
    hd = pl.program_id(0)
    kl = lax.broadcasted_iota(jnp.int32, (MOBA_BLOCK, MOBA_BLOCK), 0)
    ql = lax.broadcasted_iota(jnp.int32, (MOBA_BLOCK, MOBA_BLOCK), 1)
    d_own = ql - kl
    for ti, dist in enumerate((d_own, d_own + MOBA_BLOCK)):
        bucket = _rel_bucket(jnp.maximum(dist, 0))
        tile = jnp.zeros((MOBA_BLOCK, MOBA_BLOCK), jnp.float32)
        for bi in range(NUM_BUCKETS):
            tile = jnp.where(bucket == bi, rb_ref[bi, hd], tile)
        if ti == 0:
            tile = jnp.where(dist >= 0, tile, _MASKED)
        o_ref[0, ti] = tile * LOG2E


def _bias_call(rel_bias):
    return pl.pallas_call(
        _bias_kernel,
        grid=(N_HEADS,),
        in_specs=[pl.BlockSpec(memory_space=pltpu.SMEM)],
        out_specs=pl.BlockSpec((1, 2, MOBA_BLOCK, MOBA_BLOCK), lambda h: (h, 0, 0, 0)),
        out_shape=jax.ShapeDtypeStruct((N_HEADS, 2, MOBA_BLOCK, MOBA_BLOCK), jnp.float32),
        compiler_params=pltpu.CompilerParams(dimension_semantics=("arbitrary",)),
        name="rel_bias_tiles",
    )(rel_bias)


def _sublane_all(op, x):
    shift = SUBLANES // 2
    while shift:
        x = op(x, pltpu.roll(x, shift, axis=0))
        shift //= 2
    return x


def _attn_kernel(rb_ref, x_ref, g_ref, wqt_ref, wo_ref, k_ref, vt_ref, km_ref, bias_ref,
                 o_ref, qh_ref, mask_ref, m_ref, l_ref, acc_ref):
    first_own = pl.program_id(1) * Q_SUBTILES
    nb = km_ref.shape[1]
    tq = MOBA_BLOCK
    D = x_ref.shape[2]
    row_groups = MOBA_BLOCK // SUBLANES
    head_groups = HEAD_DIM // SUBLANES
    x = x_ref[0]
    h = _rms(x, g_ref[...]).astype(jnp.bfloat16)
    qf = lax.dot_general(wqt_ref[...], h, _NT, preferred_element_type=jnp.float32)
    q_gate = (qf * HEAD_DIM ** -0.5).astype(jnp.bfloat16)
    q = (qf * (HEAD_DIM ** -0.5 * LOG2E)).astype(jnp.bfloat16)

    feat = lax.broadcasted_iota(jnp.int32, (LANES, 1), 0)
    for sub in range(Q_SUBTILES):
        for head in range(N_HEADS):
            p, hh = divmod(head, HEADS_PER_LANE_TILE)
            qp = q[p * LANES:(p + 1) * LANES, sub * tq:(sub + 1) * tq]
            in_head = (feat >= hh * HEAD_DIM) & (feat < (hh + 1) * HEAD_DIM)
            qh_ref[sub, head] = jnp.where(in_head, qp, jnp.zeros_like(qp))

    n_q = Q_SUBTILES * tq
    n_cand = nb - 1
    own_of_query = first_own + lax.broadcasted_iota(jnp.int32, (N_HEADS, n_q), 1) // tq
    past = [own_of_query > j for j in range(n_cand)]
    some_query_ranks = first_own + Q_SUBTILES - 1 > MOBA_TOPK

    @pl.when(some_query_ranks)
    def _():
        km = jnp.broadcast_to(km_ref[0, :n_cand][:, None, :], (n_cand, N_HEADS, D))
        km = km.reshape(n_cand * N_HEADS, D)
        row_head = lax.broadcasted_iota(jnp.int32, km.shape, 0) % N_HEADS
        col_head = lax.broadcasted_iota(jnp.int32, km.shape, 1) // HEAD_DIM
        km_bd = jnp.where(row_head == col_head, km, 0.0).astype(jnp.bfloat16)
        gate = jnp.dot(km_bd, q_gate, preferred_element_type=jnp.float32)
        gates = [jnp.where(past[j], gate[j * N_HEADS:(j + 1) * N_HEADS], _NEG_INF)
                 for j in range(n_cand)]
        rank = [jnp.zeros((N_HEADS, n_q), jnp.int32) for _ in range(n_cand)]
        for lo in range(n_cand):
            for hi in range(lo + 1, n_cand):
                lo_wins = jnp.where(gates[lo] >= gates[hi], 1, 0)
                rank[hi] = rank[hi] + lo_wins
                rank[lo] = rank[lo] + (1 - lo_wins)
        for j in range(n_cand):
            keep = jnp.where(past[j], 0.0, _MASKED)
            mask_ref[j] = jnp.where(rank[j] < MOBA_TOPK, keep, _MASKED)

    @pl.when(jnp.logical_not(some_query_ranks))
    def _():
        for j in range(n_cand):
            mask_ref[j] = jnp.where(past[j], 0.0, _MASKED)

    m_ref[...] = jnp.full(m_ref.shape, _MASKED, jnp.float32)
    l_ref[...] = jnp.zeros(l_ref.shape, jnp.float32)
    acc_ref[...] = jnp.zeros(acc_ref.shape, jnp.float32)

    def attend(blocks):
        units = []
        for j, kinds in blocks:
            keys = pl.ds(pl.multiple_of(j * MOBA_BLOCK, MOBA_BLOCK), MOBA_BLOCK)
            units += [(j, keys, kind, sub, head) for head in range(N_HEADS)
                      for sub, kind in enumerate(kinds) if kind is not None]

        def scores(unit):
            _, keys, _, sub, head = unit
            p = head // HEADS_PER_LANE_TILE
            kj = k_ref[0, keys, p * LANES:(p + 1) * LANES]
            return jnp.dot(kj, qh_ref[sub, head], preferred_element_type=jnp.float32)

        def softmax(unit, s):
            j, _, kind, sub, head = unit
            s = s.reshape(row_groups, SUBLANES, tq)
            if kind == "own":
                s = s + bias_ref[head, 0].reshape(row_groups, SUBLANES, tq)
            elif kind == "prev":
                s = s + bias_ref[head, 1].reshape(row_groups, SUBLANES, tq)
            block_max = _sublane_all(jnp.maximum, jnp.max(s, axis=0))
            m_old = m_ref[sub, head]
            if kind == "own":
                m_new = jnp.maximum(m_old, block_max)
                shift = m_new
            else:
                row = mask_ref[j, head:head + 1, sub * tq:(sub + 1) * tq]
                if kind == "far":
                    row = row + rb_ref[NUM_BUCKETS - 1, head] * LOG2E
                row = jnp.broadcast_to(row, (SUBLANES, tq))
                m_new = jnp.maximum(m_old, block_max + row)
                shift = jnp.maximum(m_new - row, block_max)
            alpha = jnp.exp2(m_old - m_new)
            pj = jnp.exp2(s - shift[None])
            m_ref[sub, head] = m_new
            return pj.reshape(MOBA_BLOCK, tq).astype(jnp.bfloat16), alpha

        def accumulate(unit, pv, alpha):
            _, _, _, sub, head = unit
            hs = slice(head * HEAD_DIM, (head + 1) * HEAD_DIM)
            l_ref[sub, head] = l_ref[sub, head] * alpha + pv[HEAD_DIM:HEAD_DIM + SUBLANES]
            acc = acc_ref[sub, hs, :].reshape(head_groups, SUBLANES, tq)
            acc = acc * alpha[None] + pv[:HEAD_DIM].reshape(head_groups, SUBLANES, tq)
            acc_ref[sub, hs, :] = acc.reshape(HEAD_DIM, tq)

        ones = jnp.ones((ONES_ROWS, MOBA_BLOCK), jnp.bfloat16)

        def weighted_values(unit, pj):
            _, keys, _, _, head = unit
            vj = vt_ref[0, head * HEAD_DIM:(head + 1) * HEAD_DIM, keys]
            return jnp.dot(jnp.concatenate([vj, ones], axis=0), pj,
                           preferred_element_type=jnp.float32)

        n = len(units)
        pending_scores = {i: scores(units[i]) for i in range(SCORE_LOOKAHEAD)}
        probs = {}
        products = {}
        for step in range(n + 2):
            if step + SCORE_LOOKAHEAD < n:
                pending_scores[step + SCORE_LOOKAHEAD] = scores(units[step + SCORE_LOOKAHEAD])
            if 0 <= step - 1 < n:
                pj, alpha = probs.pop(step - 1)
                products[step - 1] = (weighted_values(units[step - 1], pj), alpha)
            if step < n:
                probs[step] = softmax(units[step], pending_scores.pop(step))
            if 0 <= step - 2 < n:
                accumulate(units[step - 2], *products.pop(step - 2))

    assert Q_SUBTILES == 2 and FAR_UNROLL == 2
    all_far = ("far", "far")
    tail = [(first_own, ("own", "prev")), (first_own + 1, (None, "own"))]

    def far_pair(t, carry):
        attend([(FAR_UNROLL * t, all_far), (FAR_UNROLL * t + 1, all_far)])
        return carry

    lax.fori_loop(0, jnp.maximum(first_own - 1, 0) // FAR_UNROLL, far_pair, 0)

    @pl.when(first_own > 0)
    def _():
        attend([(first_own - 2, all_far), (first_own - 1, ("prev", "far"))] + tail)

    @pl.when(first_own == 0)
    def _():
        attend(tail)

    outs = []
    for sub in range(Q_SUBTILES):
        for head in range(N_HEADS):
            hs = slice(head * HEAD_DIM, (head + 1) * HEAD_DIM)
            inv = 1.0 / l_ref[sub, head]
            acc = acc_ref[sub, hs, :].reshape(head_groups, SUBLANES, tq) * inv[None]
            acc_ref[sub, hs, :] = acc.reshape(HEAD_DIM, tq)
        outs.append(acc_ref[sub].T.astype(jnp.bfloat16))
    o = jnp.concatenate(outs, axis=0)
    o_ref[0] = x + jnp.dot(o, wo_ref[...], preferred_element_type=jnp.float32)


def _attn_call(x, g, wqt, wo, layer, k, vt, km, bias_tiles, rel_bias):
    B, T, D = x.shape
    nb = T // MOBA_BLOCK
    tq = MOBA_BLOCK
    n_q = Q_SUBTILES * tq
    return pl.pallas_call(
        _attn_kernel,
        grid=(B, T // n_q),
        in_specs=[
            pl.BlockSpec(memory_space=pltpu.SMEM),
            pl.BlockSpec((1, n_q, D), lambda b, i: (b, i, 0)),
            _resident((1, D)),
            _resident_layer(wqt.shape, layer),
            _resident_layer(wo.shape, layer),
            pl.BlockSpec((1, T, D), lambda b, i: (b, 0, 0)),
            pl.BlockSpec((1, D, T), lambda b, i: (b, 0, 0)),
            pl.BlockSpec((1, nb, D), lambda b, i: (b, 0, 0)),
            _resident(bias_tiles.shape),
        ],
        out_specs=pl.BlockSpec((1, n_q, D), lambda b, i: (b, i, 0)),
        out_shape=jax.ShapeDtypeStruct(x.shape, x.dtype),
        scratch_shapes=[
            pltpu.VMEM((Q_SUBTILES, N_HEADS, LANES, tq), jnp.bfloat16),
            pltpu.VMEM((nb - 1, N_HEADS, n_q), jnp.float32),
            pltpu.VMEM((Q_SUBTILES, N_HEADS, SUBLANES, tq), jnp.float32),
            pltpu.VMEM((Q_SUBTILES, N_HEADS, SUBLANES, tq), jnp.float32),
            pltpu.VMEM((Q_SUBTILES, D, tq), jnp.float32),
        ],
        compiler_params=pltpu.CompilerParams(
            dimension_semantics=("arbitrary", "arbitrary"),
            vmem_limit_bytes=VMEM_LIMIT),
        name="moba_attention",
    )(rel_bias, x, g.reshape(1, D), wqt, wo, k, vt, km, bias_tiles)


def kernel(x, norm_mixer, norm_ffn, pool_w, pool_scale, kv_norm, w_kv, w_q, w_o,
           rel_bias, w_gate_up, w_down, final_norm):
    bf = jnp.bfloat16
    assert x.shape[1] % MOBA_BLOCK == 0 and x.shape[2] == D_MODEL
    pool_w = pool_w.astype(bf)
    w_gate_up = w_gate_up.astype(bf)
    w_down = w_down.astype(bf)
    w_qt = jnp.swapaxes(w_q, 1, 2).astype(bf)
    w_o = w_o.astype(bf)
    wk = w_kv[:, :D_MODEL].astype(bf)
    wvt = w_kv[:, D_MODEL:].T.astype(bf)

    k = vt = km = bias_tiles = None
    for layer in range(DEPTH):
        if layer < N_A_LAYERS:
            x = _pool_call(x, norm_mixer[layer], pool_w, layer, pool_scale[layer])
        else:
            j = layer - N_A_LAYERS
            if j == 0:
                k, vt, km = _kv_call(x, kv_norm, wk, wvt)
                bias_tiles = _bias_call(rel_bias)
            x = _attn_call(x, norm_mixer[layer], w_qt, w_o, j, k, vt, km, bias_tiles, rel_bias)
        x = _ffn_call(x, norm_ffn[layer], w_gate_up, w_down, layer, final_norm,
                      final_norm=(layer == DEPTH - 1))
    return x
```

```python
import functools
import math

import jax
import jax.numpy as jnp
from jax import lax
from jax.experimental import pallas as pl
from jax.experimental.pallas import tpu as pltpu

D_MODEL = 1024
DEPTH = 4
N_A_LAYERS = DEPTH // 2
POOL_WINDOWS = (2, 4, 8, 16)
POOL_GROUP = D_MODEL // len(POOL_WINDOWS)
HEAD_DIM = 64
N_HEADS = D_MODEL // HEAD_DIM
MOBA_BLOCK = 256
MOBA_TOPK = 3
NUM_BUCKETS = 32
MAX_DISTANCE = 128
D_FF = -(-8 * D_MODEL // (3 * 256)) * 256
EPS = 1e-6

LANES = 128
SUBLANES = 8
PACKED_ROWS = 16
POOL_HALO = 16
assert all(w & (w - 1) == 0 and w <= POOL_HALO for w in POOL_WINDOWS)
HEADS_PER_LANE_TILE = LANES // HEAD_DIM
VMEM_LIMIT = 56 * 1024 * 1024
SCORE_LOOKAHEAD = 4
FAR_UNROLL = 2
Q_SUBTILES = 2

_NT = (((1,), (1,)), ((), ()))
_NEG_INF = float("-inf")
_MASKED = -1e30
LOG2E = math.log2(math.e)
ONES_ROWS = 16


def _rms(x, g):
    return x * lax.rsqrt(jnp.mean(x * x, axis=-1, keepdims=True) + EPS) * g


def _resident(shape):
    zeros = (0,) * len(shape)
    return pl.BlockSpec(shape, lambda *_: zeros, pipeline_mode=pl.Buffered(1))


def _resident_layer(stacked_shape, layer):
    index = (layer,) + (0,) * (len(stacked_shape) - 1)
    return pl.BlockSpec((None,) + tuple(stacked_shape[1:]), lambda *_: index,
                        pipeline_mode=pl.Buffered(1))


def _pool_kernel(x_ref, halo_ref, g_ref, w_ref, scale_ref, o_ref, hh_ref, *, tm):
    i = pl.program_id(1)
    g = g_ref[...]
    x = x_ref[0]
    h = _rms(x, g)
    halo = _rms(halo_ref[0], g)
    hh_ref[0:POOL_HALO, :] = jnp.where(i > 0, halo, 0.0)
    hh_ref[POOL_HALO:, :] = h
    t = i * tm + lax.broadcasted_iota(jnp.int32, (tm, 1), 0)
    ys = []
    for gi, win in enumerate(POOL_WINDOWS):
        sl = slice(gi * POOL_GROUP, (gi + 1) * POOL_GROUP)
        wsum = hh_ref[:, sl]
        span = 1
        while span < win:
            wsum = wsum + pltpu.roll(wsum, span, axis=0)
            span *= 2
        wsum = wsum[POOL_HALO:]
        cnt = jnp.minimum(t + 1, win).astype(jnp.float32)
        d = wsum / cnt - h[:, sl]
        ys.append(jnp.dot(d.astype(jnp.bfloat16), w_ref[gi],
                          preferred_element_type=jnp.float32))
    y = jnp.concatenate(ys, axis=-1)
    o_ref[0] = x + y * scale_ref[...]


def _pool_call(x, g, w, layer, scale, tm=512):
    B, T, D = x.shape
    halo_blocks = tm // POOL_HALO
    return pl.pallas_call(
        functools.partial(_pool_kernel, tm=tm),
        grid=(B, T // tm),
        in_specs=[
            pl.BlockSpec((1, tm, D), lambda b, i: (b, i, 0)),
            pl.BlockSpec((1, POOL_HALO, D),
                         lambda b, i: (b, jnp.maximum(i * halo_blocks - 1, 0), 0)),
            _resident((1, D)),
            _resident_layer(w.shape, layer),
            _resident((1, D)),
        ],
        out_specs=pl.BlockSpec((1, tm, D), lambda b, i: (b, i, 0)),
        out_shape=jax.ShapeDtypeStruct(x.shape, x.dtype),
        scratch_shapes=[pltpu.VMEM((tm + POOL_HALO, D), jnp.float32)],
        compiler_params=pltpu.CompilerParams(
            dimension_semantics=("arbitrary", "arbitrary"),
            vmem_limit_bytes=VMEM_LIMIT),
        name="pool_mixer",
    )(x, x, g.reshape(1, D), w, scale.reshape(1, D))


def _ffn_kernel(x_ref, g_ref, wgu_ref, wd_ref, fg_ref, o_ref, act_ref, *, fc, final_norm):
    x = x_ref[...]
    h = _rms(x, g_ref[...]).astype(jnp.bfloat16)
    for c in range(D_FF // fc):
        gate = jnp.dot(h, wgu_ref[:, c * fc:(c + 1) * fc],
                       preferred_element_type=jnp.float32)
        up = jnp.dot(h, wgu_ref[:, D_FF + c * fc:D_FF + (c + 1) * fc],
                     preferred_element_type=jnp.float32)
        act = gate * (1.0 / (1.0 + jnp.exp(-gate))) * up
        act_ref[:, c * fc:(c + 1) * fc] = act.astype(jnp.bfloat16)
    y = x + jnp.dot(act_ref[...], wd_ref[...], preferred_element_type=jnp.float32)
    if final_norm:
        y = _rms(y, fg_ref[...])
    o_ref[...] = y


def _ffn_call(x, g, wgu, wd, layer, final_g, final_norm, tm=1024, fc=256):
    B, T, D = x.shape
    xf = x.reshape(B * T, D)
    out = pl.pallas_call(
        functools.partial(_ffn_kernel, fc=fc, final_norm=final_norm),
        grid=(B * T // tm,),
        in_specs=[
            pl.BlockSpec((tm, D), lambda i: (i, 0)),
            _resident((1, D)),
            _resident_layer(wgu.shape, layer),
            _resident_layer(wd.shape, layer),
            _resident((1, D)),
        ],
        out_specs=pl.BlockSpec((tm, D), lambda i: (i, 0)),
        out_shape=jax.ShapeDtypeStruct(xf.shape, xf.dtype),
        scratch_shapes=[pltpu.VMEM((tm, D_FF), jnp.bfloat16)],
        compiler_params=pltpu.CompilerParams(
            dimension_semantics=("arbitrary",),
            vmem_limit_bytes=VMEM_LIMIT),
        name="swiglu_ffn",
    )(xf, g.reshape(1, D), wgu, wd, final_g.reshape(1, D))
    return out.reshape(B, T, D)


def _kv_kernel(x_ref, g_ref, wk_ref, wvt_ref, k_ref, vt_ref, km_ref, *, blocks_per_tile):
    j = pl.program_id(1)
    h = _rms(x_ref[0], g_ref[...]).astype(jnp.bfloat16)
    k = jnp.dot(h, wk_ref[...], preferred_element_type=jnp.float32)
    k_ref[0] = k.astype(jnp.bfloat16)
    vt = lax.dot_general(wvt_ref[...], h, _NT, preferred_element_type=jnp.float32)
    vt_ref[0] = vt.astype(jnp.bfloat16)
    for i in range(blocks_per_tile):
        km_ref[0, pl.ds(j * blocks_per_tile + i, 1), :] = jnp.mean(
            k[i * MOBA_BLOCK:(i + 1) * MOBA_BLOCK], axis=0, keepdims=True)


def _kv_call(x, g, wk, wvt, tm=1024):
    B, T, D = x.shape
    nb = T // MOBA_BLOCK
    return pl.pallas_call(
        functools.partial(_kv_kernel, blocks_per_tile=tm // MOBA_BLOCK),
        grid=(B, T // tm),
        in_specs=[
            pl.BlockSpec((1, tm, D), lambda b, j: (b, j, 0)),
            _resident((1, D)),
            _resident(wk.shape),
            _resident(wvt.shape),
        ],
        out_specs=[
            pl.BlockSpec((1, tm, D), lambda b, j: (b, j, 0)),
            pl.BlockSpec((1, D, tm), lambda b, j: (b, 0, j)),
            pl.BlockSpec((1, nb, D), lambda b, j: (b, 0, 0)),
        ],
        out_shape=[
            jax.ShapeDtypeStruct((B, T, D), jnp.bfloat16),
            jax.ShapeDtypeStruct((B, D, T), jnp.bfloat16),
            jax.ShapeDtypeStruct((B, nb, D), jnp.float32),
        ],
        compiler_params=pltpu.CompilerParams(
            dimension_semantics=("arbitrary", "arbitrary"),
            vmem_limit_bytes=VMEM_LIMIT),
        name="shared_kv",
    )(x, g.reshape(1, D), wk, wvt)


def _rel_bucket(n):
    max_exact = NUM_BUCKETS // 2
    nf = jnp.maximum(n, max_exact).astype(jnp.float32)
    large = max_exact + (jnp.log(nf / max_exact) / math.log(MAX_DISTANCE / max_exact)
                         * (NUM_BUCKETS - max_exact)).astype(jnp.int32)
    large = jnp.minimum(large, NUM_BUCKETS - 1)
    return jnp.where(n < max_exact, n, large)


def _bias_kernel(rb_ref, o_ref):
    hd = pl.program_id(0)
    kl = lax.broadcasted_iota(jnp.int32, (MOBA_BLOCK, MOBA_BLOCK), 0)
    ql = lax.broadcasted_iota(jnp.int32, (MOBA_BLOCK, MOBA_BLOCK), 1)
    d_own = ql - kl
    for ti, dist in enumerate((d_own, d_own + MOBA_BLOCK)):
        bucket = _rel_bucket(jnp.maximum(dist, 0))
        tile = jnp.zeros((MOBA_BLOCK, MOBA_BLOCK), jnp.float32)
        for bi in range(NUM_BUCKETS):
            tile = jnp.where(bucket == bi, rb_ref[bi, hd], tile)
        if ti == 0:
            tile = jnp.where(dist >= 0, tile, _MASKED)
        o_ref[0, ti] = tile * LOG2E


def _bias_call(rel_bias):
    return pl.pallas_call(
        _bias_kernel,
        grid=(N_HEADS,),
        in_specs=[pl.BlockSpec(memory_space=pltpu.SMEM)],
        out_specs=pl.BlockSpec((1, 2, MOBA_BLOCK, MOBA_BLOCK), lambda h: (h, 0, 0, 0)),
        out_shape=jax.ShapeDtypeStruct((N_HEADS, 2, MOBA_BLOCK, MOBA_BLOCK), jnp.float32),
        compiler_params=pltpu.CompilerParams(dimension_semantics=("arbitrary",)),
        name="rel_bias_tiles",
    )(rel_bias)


def _sublane_all(op, x):
    shift = SUBLANES // 2
    while shift:
        x = op(x, pltpu.roll(x, shift, axis=0))
        shift //= 2
    return x


def _attn_kernel(rb_ref, x_ref, g_ref, wqt_ref, wo_ref, k_ref, vt_ref, km_ref, bias_ref,
                 o_ref, qh_ref, mask_ref, m_ref, l_ref, acc_ref):
    first_own = pl.program_id(1) * Q_SUBTILES
    nb = km_ref.shape[1]
    tq = MOBA_BLOCK
    D = x_ref.shape[2]
    row_groups = MOBA_BLOCK // SUBLANES
    head_groups = HEAD_DIM // SUBLANES
    x = x_ref[0]
    h = _rms(x, g_ref[...]).astype(jnp.bfloat16)
    qf = lax.dot_general(wqt_ref[...], h, _NT, preferred_element_type=jnp.float32)
    q_gate = (qf * HEAD_DIM ** -0.5).astype(jnp.bfloat16)
    q = (qf * (HEAD_DIM ** -0.5 * LOG2E)).astype(jnp.bfloat16)

    feat = lax.broadcasted_iota(jnp.int32, (LANES, 1), 0)
    for sub in range(Q_SUBTILES):
        for head in range(N_HEADS):
            p, hh = divmod(head, HEADS_PER_LANE_TILE)
            qp = q[p * LANES:(p + 1) * LANES, sub * tq:(sub + 1) * tq]
            in_head = (feat >= hh * HEAD_DIM) & (feat < (hh + 1) * HEAD_DIM)
            qh_ref[sub, head] = jnp.where(in_head, qp, jnp.zeros_like(qp))

    n_q = Q_SUBTILES * tq
    n_cand = nb - 1
    own_of_query = first_own + lax.broadcasted_iota(jnp.int32, (N_HEADS, n_q), 1) // tq
    past = [own_of_query > j for j in range(n_cand)]
    some_query_ranks = first_own + Q_SUBTILES - 1 > MOBA_TOPK

    @pl.when(some_query_ranks)
    def _():
        km = jnp.broadcast_to(km_ref[0, :n_cand][:, None, :], (n_cand, N_HEADS, D))
        km = km.reshape(n_cand * N_HEADS, D)
        row_head = lax.broadcasted_iota(jnp.int32, km.shape, 0) % N_HEADS
        col_head = lax.broadcasted_iota(jnp.int32, km.shape, 1) // HEAD_DIM
        km_bd = jnp.where(row_head == col_head, km, 0.0).astype(jnp.bfloat16)
        gate = jnp.dot(km_bd, q_gate, preferred_element_type=jnp.float32)
        gates = [jnp.where(past[j], gate[j * N_HEADS:(j + 1) * N_HEADS], _NEG_INF)
                 for j in range(n_cand)]
        rank = [jnp.zeros((N_HEADS, n_q), jnp.int32) for _ in range(n_cand)]
        for lo in range(n_cand):
            for hi in range(lo + 1, n_cand):
                lo_wins = jnp.where(gates[lo] >= gates[hi], 1, 0)
                rank[hi] = rank[hi] + lo_wins
                rank[lo] = rank[lo] + (1 - lo_wins)
        for j in range(n_cand):
            keep = jnp.where(past[j], 0.0, _MASKED)
            mask_ref[j] = jnp.where(rank[j] < MOBA_TOPK, keep, _MASKED)

    @pl.when(jnp.logical_not(some_query_ranks))
    def _():
        for j in range(n_cand):
            mask_ref[j] = jnp.where(past[j], 0.0, _MASKED)

    m_ref[...] = jnp.full(m_ref.shape, _MASKED, jnp.float32)
    l_ref[...] = jnp.zeros(l_ref.shape, jnp.float32)
    acc_ref[...] = jnp.zeros(acc_ref.shape, jnp.float32)

    def attend(blocks):
        units = []
        for j, kinds in blocks:
            keys = pl.ds(pl.multiple_of(j * MOBA_BLOCK, MOBA_BLOCK), MOBA_BLOCK)
            units += [(j, keys, kind, sub, head) for head in range(N_HEADS)
                      for sub, kind in enumerate(kinds) if kind is not None]

        def scores(unit):
            _, keys, _, sub, head = unit
            p = head // HEADS_PER_LANE_TILE
            kj = k_ref[0, keys, p * LANES:(p + 1) * LANES]
            return jnp.dot(kj, qh_ref[sub, head], preferred_element_type=jnp.float32)

        def softmax(unit, s):
            j, _, kind, sub, head = unit
            if kind != "far":
                tile = bias_ref[head, 0 if kind == "own" else 1]
                s = (s.reshape(row_groups, SUBLANES, tq)
                     + tile.reshape(row_groups, SUBLANES, tq)).reshape(MOBA_BLOCK, tq)
            sb = s.astype(jnp.bfloat16).reshape(MOBA_BLOCK // PACKED_ROWS, PACKED_ROWS, tq)
            bm = jnp.max(sb, axis=0).astype(jnp.float32)
            block_max = _sublane_all(jnp.maximum, jnp.maximum(bm[:SUBLANES], bm[SUBLANES:]))
            m_old = m_ref[sub, head]
            bias = rb_ref[NUM_BUCKETS - 1, head] * LOG2E if kind == "far" else 0.0
            top = block_max + bias
            if kind != "own":
                selected = jnp.broadcast_to(
                    mask_ref[j, head:head + 1, sub * tq:(sub + 1) * tq], (SUBLANES, tq)) == 0.0
                top = jnp.where(selected, top, _MASKED)
            shift = jnp.maximum(jnp.maximum(m_old, top) - bias, block_max)
            shift = shift.astype(jnp.bfloat16).astype(jnp.float32)
            shift_b = jnp.concatenate([shift, shift], axis=0).astype(jnp.bfloat16)
            pj = jnp.exp2(sb - shift_b[None])
            ref = shift + bias
            if kind != "own":
                m_new = jnp.where(selected, jnp.maximum(m_old, ref), m_old)
                scale_new = jnp.where(selected, jnp.exp2(ref - m_new), 0.0)
            else:
                m_new = jnp.maximum(m_old, ref)
                scale_new = jnp.exp2(ref - m_new)
            scale_old = jnp.exp2(m_old - m_new)
            m_ref[sub, head] = m_new
            return pj.reshape(MOBA_BLOCK, tq), (scale_old, scale_new)

        def accumulate(unit, pv, scales):
            _, _, _, sub, head = unit
            scale_old, scale_new = scales
            hs = slice(head * HEAD_DIM, (head + 1) * HEAD_DIM)
            l_ref[sub, head] = (l_ref[sub, head] * scale_old
                                + pv[HEAD_DIM:HEAD_DIM + SUBLANES] * scale_new)
            acc = acc_ref[sub, hs, :].reshape(head_groups, SUBLANES, tq)
            acc = (acc * scale_old[None]
                   + pv[:HEAD_DIM].reshape(head_groups, SUBLANES, tq) * scale_new[None])
            acc_ref[sub, hs, :] = acc.reshape(HEAD_DIM, tq)

        ones = jnp.ones((ONES_ROWS, MOBA_BLOCK), jnp.bfloat16)

        def weighted_values(unit, pj):
            _, keys, _, _, head = unit
            vj = vt_ref[0, head * HEAD_DIM:(head + 1) * HEAD_DIM, keys]
            return jnp.dot(jnp.concatenate([vj, ones], axis=0), pj,
                           preferred_element_type=jnp.float32)

        n = len(units)
        pending_scores = {i: scores(units[i]) for i in range(SCORE_LOOKAHEAD)}
        probs = {}
        products = {}
        for step in range(n + 2):
            if step + SCORE_LOOKAHEAD < n:
                pending_scores[step + SCORE_LOOKAHEAD] = scores(units[step + SCORE_LOOKAHEAD])
            if 0 <= step - 1 < n:
                pj, alpha = probs.pop(step - 1)
                products[step - 1] = (weighted_values(units[step - 1], pj), alpha)
            if step < n:
                probs[step] = softmax(units[step], pending_scores.pop(step))
            if 0 <= step - 2 < n:
                accumulate(units[step - 2], *products.pop(step - 2))

    assert Q_SUBTILES == 2 and FAR_UNROLL == 2
    all_far = ("far", "far")
    tail = [(first_own, ("own", "prev")), (first_own + 1, (None, "own"))]

    def far_pair(t, carry):
        attend([(FAR_UNROLL * t, all_far), (FAR_UNROLL * t + 1, all_far)])
        return carry

    lax.fori_loop(0, jnp.maximum(first_own - 1, 0) // FAR_UNROLL, far_pair, 0)

    @pl.when(first_own > 0)
    def _():
        attend([(first_own - 2, all_far), (first_own - 1, ("prev", "far"))] + tail)

    @pl.when(first_own == 0)
    def _():
        attend(tail)

    outs = []
    for sub in range(Q_SUBTILES):
        for head in range(N_HEADS):
            hs = slice(head * HEAD_DIM, (head + 1) * HEAD_DIM)
            inv = 1.0 / l_ref[sub, head]
            acc = acc_ref[sub, hs, :].reshape(head_groups, SUBLANES, tq) * inv[None]
            acc_ref[sub, hs, :] = acc.reshape(HEAD_DIM, tq)
        outs.append(acc_ref[sub].T.astype(jnp.bfloat16))
    o = jnp.concatenate(outs, axis=0)
    o_ref[0] = x + jnp.dot(o, wo_ref[...], preferred_element_type=jnp.float32)


def _attn_call(x, g, wqt, wo, layer, k, vt, km, bias_tiles, rel_bias):
    B, T, D = x.shape
    nb = T // MOBA_BLOCK
    tq = MOBA_BLOCK
    n_q = Q_SUBTILES * tq
    return pl.pallas_call(
        _attn_kernel,
        grid=(B, T // n_q),
        in_specs=[
            pl.BlockSpec(memory_space=pltpu.SMEM),
            pl.BlockSpec((1, n_q, D), lambda b, i: (b, i, 0)),
            _resident((1, D)),
            _resident_layer(wqt.shape, layer),
            _resident_layer(wo.shape, layer),
            pl.BlockSpec((1, T, D), lambda b, i: (b, 0, 0)),
            pl.BlockSpec((1, D, T), lambda b, i: (b, 0, 0)),
            pl.BlockSpec((1, nb, D), lambda b, i: (b, 0, 0)),
            _resident(bias_tiles.shape),
        ],
        out_specs=pl.BlockSpec((1, n_q, D), lambda b, i: (b, i, 0)),
        out_shape=jax.ShapeDtypeStruct(x.shape, x.dtype),
        scratch_shapes=[
            pltpu.VMEM((Q_SUBTILES, N_HEADS, LANES, tq), jnp.bfloat16),
            pltpu.VMEM((nb - 1, N_HEADS, n_q), jnp.float32),
            pltpu.VMEM((Q_SUBTILES, N_HEADS, SUBLANES, tq), jnp.float32),
            pltpu.VMEM((Q_SUBTILES, N_HEADS, SUBLANES, tq), jnp.float32),
            pltpu.VMEM((Q_SUBTILES, D, tq), jnp.float32),
        ],
        compiler_params=pltpu.CompilerParams(
            dimension_semantics=("arbitrary", "arbitrary"),
            vmem_limit_bytes=VMEM_LIMIT),
        name="moba_attention",
    )(rel_bias, x, g.reshape(1, D), wqt, wo, k, vt, km, bias_tiles)


def kernel(x, norm_mixer, norm_ffn, pool_w, pool_scale, kv_norm, w_kv, w_q, w_o,
           rel_bias, w_gate_up, w_down, final_norm):
    bf = jnp.bfloat16
    assert x.shape[1] % MOBA_BLOCK == 0 and x.shape[2] == D_MODEL
    pool_w = pool_w.astype(bf)
    w_gate_up = w_gate_up.astype(bf)
    w_down = w_down.astype(bf)
    w_qt = jnp.swapaxes(w_q, 1, 2).astype(bf)
    w_o = w_o.astype(bf)
    wk = w_kv[:, :D_MODEL].astype(bf)
    wvt = w_kv[:, D_MODEL:].T.astype(bf)

    k = vt = km = bias_tiles = None
    for layer in range(DEPTH):
        if layer < N_A_LAYERS:
            x = _pool_call(x, norm_mixer[layer], pool_w, layer, pool_scale[layer])
        else:
            j = layer - N_A_LAYERS
            if j == 0:
                k, vt, km = _kv_call(x, kv_norm, wk, wvt)
                bias_tiles = _bias_call(rel_bias)
            x = _attn_call(x, norm_mixer[layer], w_qt, w_o, j, k, vt, km, bias_tiles, rel_bias)
        x = _ffn_call(x, norm_ffn[layer], w_gate_up, w_down, layer, final_norm,
                      final_norm=(layer == DEPTH - 1))
    return x
```

```python
import functools
import math

import jax
import jax.numpy as jnp
from jax import lax
from jax.experimental import pallas as pl
from jax.experimental.pallas import tpu as pltpu

D_MODEL = 1024
DEPTH = 4
N_A_LAYERS = DEPTH // 2
POOL_WINDOWS = (2, 4, 8, 16)
POOL_GROUP = D_MODEL // len(POOL_WINDOWS)
HEAD_DIM = 64
N_HEADS = D_MODEL // HEAD_DIM
MOBA_BLOCK = 256
MOBA_TOPK = 3
NUM_BUCKETS = 32
MAX_DISTANCE = 128
D_FF = -(-8 * D_MODEL // (3 * 256)) * 256
EPS = 1e-6

LANES = 128
SUBLANES = 8
PACKED_ROWS = 16
POOL_HALO = 16
assert all(w & (w - 1) == 0 and w <= POOL_HALO for w in POOL_WINDOWS)
HEADS_PER_LANE_TILE = LANES // HEAD_DIM
VMEM_LIMIT = 56 * 1024 * 1024
SCORE_LOOKAHEAD = 4
FAR_UNROLL = 2
WEIGHT_RING = 2
Q_SUBTILES = 2

_NT = (((1,), (1,)), ((), ()))
_NEG_INF = float("-inf")
_MASKED = -1e30
LOG2E = math.log2(math.e)
ONES_ROWS = 16


def _rms(x, g):
    return x * lax.rsqrt(jnp.mean(x * x, axis=-1, keepdims=True) + EPS) * g


def _resident(shape):
    zeros = (0,) * len(shape)
    return pl.BlockSpec(shape, lambda *_: zeros, pipeline_mode=pl.Buffered(1))


def _resident_layer(stacked_shape, layer):
    index = (layer,) + (0,) * (len(stacked_shape) - 1)
    return pl.BlockSpec((None,) + tuple(stacked_shape[1:]), lambda *_: index,
                        pipeline_mode=pl.Buffered(1))


def _pool_kernel(x_ref, halo_ref, g_ref, w_ref, scale_ref, o_ref, hh_ref, *, tm):
    i = pl.program_id(1)
    g = g_ref[...]
    x = x_ref[0]
    h = _rms(x, g)
    halo = _rms(halo_ref[0], g)
    hh_ref[0:POOL_HALO, :] = jnp.where(i > 0, halo, 0.0)
    hh_ref[POOL_HALO:, :] = h
    t = i * tm + lax.broadcasted_iota(jnp.int32, (tm, 1), 0)
    ys = []
    for gi, win in enumerate(POOL_WINDOWS):
        sl = slice(gi * POOL_GROUP, (gi + 1) * POOL_GROUP)
        wsum = hh_ref[:, sl]
        span = 1
        while span < win:
            wsum = wsum + pltpu.roll(wsum, span, axis=0)
            span *= 2
        wsum = wsum[POOL_HALO:]
        cnt = jnp.minimum(t + 1, win).astype(jnp.float32)
        d = wsum / cnt - h[:, sl]
        ys.append(jnp.dot(d.astype(jnp.bfloat16), w_ref[gi],
                          preferred_element_type=jnp.float32))
    y = jnp.concatenate(ys, axis=-1)
    o_ref[0] = x + y * scale_ref[...]


def _pool_call(x, g, w, layer, scale, tm=512):
    B, T, D = x.shape
    halo_blocks = tm // POOL_HALO
    return pl.pallas_call(
        functools.partial(_pool_kernel, tm=tm),
        grid=(B, T // tm),
        in_specs=[
            pl.BlockSpec((1, tm, D), lambda b, i: (b, i, 0)),
            pl.BlockSpec((1, POOL_HALO, D),
                         lambda b, i: (b, jnp.maximum(i * halo_blocks - 1, 0), 0)),
            _resident((1, D)),
            _resident_layer(w.shape, layer),
            _resident((1, D)),
        ],
        out_specs=pl.BlockSpec((1, tm, D), lambda b, i: (b, i, 0)),
        out_shape=jax.ShapeDtypeStruct(x.shape, x.dtype),
        scratch_shapes=[pltpu.VMEM((tm + POOL_HALO, D), jnp.float32)],
        compiler_params=pltpu.CompilerParams(
            dimension_semantics=("arbitrary", "arbitrary"),
            vmem_limit_bytes=VMEM_LIMIT),
        name="pool_mixer",
    )(x, x, g.reshape(1, D), w, scale.reshape(1, D))


def _ffn_kernel(x_ref, g_ref, wgu_hbm, wd_hbm, fg_ref, o_ref,
                act_ref, wgu_ref, wd_ref, stage_gu_ref, stage_d_ref, sem_ref,
                *, layer, fc, final_norm):
    n_chunks = D_FF // fc
    x = x_ref[...]
    h = _rms(x, g_ref[...]).astype(jnp.bfloat16)

    def chunk_copies(c):
        slot = c % WEIGHT_RING
        cols = pl.ds(c * fc, fc)
        return (
            pltpu.make_async_copy(wgu_hbm.at[layer, :, cols],
                                  stage_gu_ref.at[slot, 0], sem_ref.at[slot, 0]),
            pltpu.make_async_copy(wgu_hbm.at[layer, :, pl.ds(D_FF + c * fc, fc)],
                                  stage_gu_ref.at[slot, 1], sem_ref.at[slot, 1]),
            pltpu.make_async_copy(wd_hbm.at[layer, cols, :],
                                  stage_d_ref.at[slot], sem_ref.at[slot, 2]),
        )

    def fetch_and_cast(c):
        if c + WEIGHT_RING - 1 < n_chunks:
            for copy in chunk_copies(c + WEIGHT_RING - 1):
                copy.start()
        for copy in chunk_copies(c):
            copy.wait()
        slot = c % WEIGHT_RING
        wgu_ref[:, c * fc:(c + 1) * fc] = stage_gu_ref[slot, 0].astype(jnp.bfloat16)
        wgu_ref[:, D_FF + c * fc:D_FF + (c + 1) * fc] = stage_gu_ref[slot, 1].astype(jnp.bfloat16)
        wd_ref[c * fc:(c + 1) * fc, :] = stage_d_ref[slot].astype(jnp.bfloat16)

    def ffn(prepare_chunk):
        for c in range(n_chunks):
            prepare_chunk(c)
            gate = jnp.dot(h, wgu_ref[:, c * fc:(c + 1) * fc],
                           preferred_element_type=jnp.float32)
            up = jnp.dot(h, wgu_ref[:, D_FF + c * fc:D_FF + (c + 1) * fc],
                         preferred_element_type=jnp.float32)
            act = gate * (1.0 / (1.0 + jnp.exp(-gate))) * up
            act_ref[:, c * fc:(c + 1) * fc] = act.astype(jnp.bfloat16)
        y = x + jnp.dot(act_ref[...], wd_ref[...], preferred_element_type=jnp.float32)
        if final_norm:
            y = _rms(y, fg_ref[...])
        o_ref[...] = y

    @pl.when(pl.program_id(0) == 0)
    def _():
        for c in range(WEIGHT_RING - 1):
            for copy in chunk_copies(c):
                copy.start()
        ffn(fetch_and_cast)

    @pl.when(pl.program_id(0) > 0)
    def _():
        ffn(lambda c: None)


def _ffn_call(x, g, wgu, wd, layer, final_g, final_norm, tm=1024, fc=256):
    B, T, D = x.shape
    xf = x.reshape(B * T, D)
    out = pl.pallas_call(
        functools.partial(_ffn_kernel, layer=layer, fc=fc, final_norm=final_norm),
        grid=(B * T // tm,),
        in_specs=[
            pl.BlockSpec((tm, D), lambda i: (i, 0)),
            _resident((1, D)),
            pl.BlockSpec(memory_space=pl.ANY),
            pl.BlockSpec(memory_space=pl.ANY),
            _resident((1, D)),
        ],
        out_specs=pl.BlockSpec((tm, D), lambda i: (i, 0)),
        out_shape=jax.ShapeDtypeStruct(xf.shape, xf.dtype),
        scratch_shapes=[
            pltpu.VMEM((tm, D_FF), jnp.bfloat16),
            pltpu.VMEM((D, 2 * D_FF), jnp.bfloat16),
            pltpu.VMEM((D_FF, D), jnp.bfloat16),
            pltpu.VMEM((WEIGHT_RING, 2, D, fc), jnp.float32),
            pltpu.VMEM((WEIGHT_RING, fc, D), jnp.float32),
            pltpu.SemaphoreType.DMA((WEIGHT_RING, 3)),
        ],
        compiler_params=pltpu.CompilerParams(
            dimension_semantics=("arbitrary",),
            vmem_limit_bytes=VMEM_LIMIT),
        name="swiglu_ffn",
    )(xf, g.reshape(1, D), wgu, wd, final_g.reshape(1, D))
    return out.reshape(B, T, D)


def _kv_kernel(x_ref, g_ref, wk_ref, wvt_ref, k_ref, vt_ref, km_ref, *, blocks_per_tile):
    j = pl.program_id(1)
    h = _rms(x_ref[0], g_ref[...]).astype(jnp.bfloat16)
    k = jnp.dot(h, wk_ref[...], preferred_element_type=jnp.float32)
    k_ref[0] = k.astype(jnp.bfloat16)
    vt = lax.dot_general(wvt_ref[...], h, _NT, preferred_element_type=jnp.float32)
    vt_ref[0] = vt.astype(jnp.bfloat16)
    for i in range(blocks_per_tile):
        km_ref[0, pl.ds(j * blocks_per_tile + i, 1), :] = jnp.mean(
            k[i * MOBA_BLOCK:(i + 1) * MOBA_BLOCK], axis=0, keepdims=True)


def _kv_call(x, g, wk, wvt, tm=1024):
    B, T, D = x.shape
    nb = T // MOBA_BLOCK
    return pl.pallas_call(
        functools.partial(_kv_kernel, blocks_per_tile=tm // MOBA_BLOCK),
        grid=(B, T // tm),
        in_specs=[
            pl.BlockSpec((1, tm, D), lambda b, j: (b, j, 0)),
            _resident((1, D)),
            _resident(wk.shape),
            _resident(wvt.shape),
        ],
        out_specs=[
            pl.BlockSpec((1, tm, D), lambda b, j: (b, j, 0)),
            pl.BlockSpec((1, D, tm), lambda b, j: (b, 0, j)),
            pl.BlockSpec((1, nb, D), lambda b, j: (b, 0, 0)),
        ],
        out_shape=[
            jax.ShapeDtypeStruct((B, T, D), jnp.bfloat16),
            jax.ShapeDtypeStruct((B, D, T), jnp.bfloat16),
            jax.ShapeDtypeStruct((B, nb, D), jnp.float32),
        ],
        compiler_params=pltpu.CompilerParams(
            dimension_semantics=("arbitrary", "arbitrary"),
            vmem_limit_bytes=VMEM_LIMIT),
        name="shared_kv",
    )(x, g.reshape(1, D), wk, wvt)


def _rel_bucket(n):
    max_exact = NUM_BUCKETS // 2
    nf = jnp.maximum(n, max_exact).astype(jnp.float32)
    large = max_exact + (jnp.log(nf / max_exact) / math.log(MAX_DISTANCE / max_exact)
                         * (NUM_BUCKETS - max_exact)).astype(jnp.int32)
    large = jnp.minimum(large, NUM_BUCKETS - 1)
    return jnp.where(n < max_exact, n, large)


def _bias_kernel(rb_ref, o_ref):
    hd = pl.program_id(0)
    kl = lax.broadcasted_iota(jnp.int32, (MOBA_BLOCK, MOBA_BLOCK), 0)
    ql = lax.broadcasted_iota(jnp.int32, (MOBA_BLOCK, MOBA_BLOCK), 1)
    d_own = ql - kl
    for ti, dist in enumerate((d_own, d_own + MOBA_BLOCK)):
        bucket = _rel_bucket(jnp.maximum(dist, 0))
        tile = jnp.zeros((MOBA_BLOCK, MOBA_BLOCK), jnp.float32)
        for bi in range(NUM_BUCKETS):
            tile = jnp.where(bucket == bi, rb_ref[bi, hd], tile)
        if ti == 0:
            tile = jnp.where(dist >= 0, tile, _MASKED)
        o_ref[0, ti] = tile * LOG2E


def _bias_call(rel_bias):
    return pl.pallas_call(
        _bias_kernel,
        grid=(N_HEADS,),
        in_specs=[pl.BlockSpec(memory_space=pltpu.SMEM)],
        out_specs=pl.BlockSpec((1, 2, MOBA_BLOCK, MOBA_BLOCK), lambda h: (h, 0, 0, 0)),
        out_shape=jax.ShapeDtypeStruct((N_HEADS, 2, MOBA_BLOCK, MOBA_BLOCK), jnp.float32),
        compiler_params=pltpu.CompilerParams(dimension_semantics=("arbitrary",)),
        name="rel_bias_tiles",
    )(rel_bias)


def _sublane_all(op, x):
    shift = SUBLANES // 2
    while shift:
        x = op(x, pltpu.roll(x, shift, axis=0))
        shift //= 2
    return x


def _attn_kernel(rb_ref, x_ref, g_ref, wqt_ref, wo_ref, k_ref, vt_ref, km_ref, bias_ref,
                 o_ref, qh_ref, mask_ref, m_ref, l_ref, acc_ref):
    first_own = pl.program_id(1) * Q_SUBTILES
    nb = km_ref.shape[1]
    tq = MOBA_BLOCK
    D = x_ref.shape[2]
    row_groups = MOBA_BLOCK // SUBLANES
    head_groups = HEAD_DIM // SUBLANES
    x = x_ref[0]
    h = _rms(x, g_ref[...]).astype(jnp.bfloat16)
    qf = lax.dot_general(wqt_ref[...], h, _NT, preferred_element_type=jnp.float32)
    q_gate = (qf * HEAD_DIM ** -0.5).astype(jnp.bfloat16)
    q = (qf * (HEAD_DIM ** -0.5 * LOG2E)).astype(jnp.bfloat16)

    feat = lax.broadcasted_iota(jnp.int32, (LANES, 1), 0)
    for sub in range(Q_SUBTILES):
        for head in range(N_HEADS):
            p, hh = divmod(head, HEADS_PER_LANE_TILE)
            qp = q[p * LANES:(p + 1) * LANES, sub * tq:(sub + 1) * tq]
            in_head = (feat >= hh * HEAD_DIM) & (feat < (hh + 1) * HEAD_DIM)
            qh_ref[sub, head] = jnp.where(in_head, qp, jnp.zeros_like(qp))

    n_q = Q_SUBTILES * tq
    n_cand = nb - 1
    own_of_query = first_own + lax.broadcasted_iota(jnp.int32, (N_HEADS, n_q), 1) // tq
    past = [own_of_query > j for j in range(n_cand)]
    some_query_ranks = first_own + Q_SUBTILES - 1 > MOBA_TOPK

    @pl.when(some_query_ranks)
    def _():
        km = jnp.broadcast_to(km_ref[0, :n_cand][:, None, :], (n_cand, N_HEADS, D))
        km = km.reshape(n_cand * N_HEADS, D)
        row_head = lax.broadcasted_iota(jnp.int32, km.shape, 0) % N_HEADS
        col_head = lax.broadcasted_iota(jnp.int32, km.shape, 1) // HEAD_DIM
        km_bd = jnp.where(row_head == col_head, km, 0.0).astype(jnp.bfloat16)
        gate = jnp.dot(km_bd, q_gate, preferred_element_type=jnp.float32)
        gates = [jnp.where(past[j], gate[j * N_HEADS:(j + 1) * N_HEADS], _NEG_INF)
                 for j in range(n_cand)]
        rank = [jnp.zeros((N_HEADS, n_q), jnp.int32) for _ in range(n_cand)]
        for lo in range(n_cand):
            for hi in range(lo + 1, n_cand):
                lo_wins = jnp.where(gates[lo] >= gates[hi], 1, 0)
                rank[hi] = rank[hi] + lo_wins
                rank[lo] = rank[lo] + (1 - lo_wins)
        for j in range(n_cand):
            keep = jnp.where(past[j], 0.0, _MASKED)
            mask_ref[j] = jnp.where(rank[j] < MOBA_TOPK, keep, _MASKED)

    @pl.when(jnp.logical_not(some_query_ranks))
    def _():
        for j in range(n_cand):
            mask_ref[j] = jnp.where(past[j], 0.0, _MASKED)

    m_ref[...] = jnp.full(m_ref.shape, _MASKED, jnp.float32)
    l_ref[...] = jnp.zeros(l_ref.shape, jnp.float32)
    acc_ref[...] = jnp.zeros(acc_ref.shape, jnp.float32)

    def attend(blocks):
        units = []
        for j, kinds in blocks:
            keys = pl.ds(pl.multiple_of(j * MOBA_BLOCK, MOBA_BLOCK), MOBA_BLOCK)
            units += [(j, keys, kind, sub, head) for head in range(N_HEADS)
                      for sub, kind in enumerate(kinds) if kind is not None]

        def scores(unit):
            _, keys, _, sub, head = unit
            p = head // HEADS_PER_LANE_TILE
            kj = k_ref[0, keys, p * LANES:(p + 1) * LANES]
            return jnp.dot(kj, qh_ref[sub, head], preferred_element_type=jnp.float32)

        def softmax(unit, s):
            j, _, kind, sub, head = unit
            if kind != "far":
                tile = bias_ref[head, 0 if kind == "own" else 1]
                s = (s.reshape(row_groups, SUBLANES, tq)
                     + tile.reshape(row_groups, SUBLANES, tq)).reshape(MOBA_BLOCK, tq)
            sb = s.astype(jnp.bfloat16).reshape(MOBA_BLOCK // PACKED_ROWS, PACKED_ROWS, tq)
            bm = jnp.max(sb, axis=0).astype(jnp.float32)
            block_max = _sublane_all(jnp.maximum, jnp.maximum(bm[:SUBLANES], bm[SUBLANES:]))
            m_old = m_ref[sub, head]
            bias = rb_ref[NUM_BUCKETS - 1, head] * LOG2E if kind == "far" else 0.0
            top = block_max + bias
            if kind != "own":
                selected = jnp.broadcast_to(
                    mask_ref[j, head:head + 1, sub * tq:(sub + 1) * tq], (SUBLANES, tq)) == 0.0
                top = jnp.where(selected, top, _MASKED)
            shift = jnp.maximum(jnp.maximum(m_old, top) - bias, block_max)
            shift = shift.astype(jnp.bfloat16).astype(jnp.float32)
            shift_b = jnp.concatenate([shift, shift], axis=0).astype(jnp.bfloat16)
            pj = jnp.exp2(sb - shift_b[None])
            ref = shift + bias
            if kind != "own":
                m_new = jnp.where(selected, jnp.maximum(m_old, ref), m_old)
                scale_new = jnp.where(selected, jnp.exp2(ref - m_new), 0.0)
            else:
                m_new = jnp.maximum(m_old, ref)
                scale_new = jnp.exp2(ref - m_new)
            scale_old = jnp.exp2(m_old - m_new)
            m_ref[sub, head] = m_new
            return pj.reshape(MOBA_BLOCK, tq), (scale_old, scale_new)

        def accumulate(unit, pv, scales):
            _, _, _, sub, head = unit
            scale_old, scale_new = scales
            hs = slice(head * HEAD_DIM, (head + 1) * HEAD_DIM)
            l_ref[sub, head] = (l_ref[sub, head] * scale_old
                                + pv[HEAD_DIM:HEAD_DIM + SUBLANES] * scale_new)
            acc = acc_ref[sub, hs, :].reshape(head_groups, SUBLANES, tq)
            acc = (acc * scale_old[None]
                   + pv[:HEAD_DIM].reshape(head_groups, SUBLANES, tq) * scale_new[None])
            acc_ref[sub, hs, :] = acc.reshape(HEAD_DIM, tq)

        ones = jnp.ones((ONES_ROWS, MOBA_BLOCK), jnp.bfloat16)

        def weighted_values(unit, pj):
            _, keys, _, _, head = unit
            vj = vt_ref[0, head * HEAD_DIM:(head + 1) * HEAD_DIM, keys]
            return jnp.dot(jnp.concatenate([vj, ones], axis=0), pj,
                           preferred_element_type=jnp.float32)

        n = len(units)
        pending_scores = {i: scores(units[i]) for i in range(SCORE_LOOKAHEAD)}
        probs = {}
        products = {}
        for step in range(n + 2):
            if step + SCORE_LOOKAHEAD < n:
                pending_scores[step + SCORE_LOOKAHEAD] = scores(units[step + SCORE_LOOKAHEAD])
            if 0 <= step - 1 < n:
                pj, alpha = probs.pop(step - 1)
                products[step - 1] = (weighted_values(units[step - 1], pj), alpha)
            if step < n:
                probs[step] = softmax(units[step], pending_scores.pop(step))
            if 0 <= step - 2 < n:
                accumulate(units[step - 2], *products.pop(step - 2))

    assert Q_SUBTILES == 2 and FAR_UNROLL == 2
    all_far = ("far", "far")
    tail = [(first_own, ("own", "prev")), (first_own + 1, (None, "own"))]

    def far_pair(t, carry):
        attend([(FAR_UNROLL * t, all_far), (FAR_UNROLL * t + 1, all_far)])
        return carry

    lax.fori_loop(0, jnp.maximum(first_own - 1, 0) // FAR_UNROLL, far_pair, 0)

    @pl.when(first_own > 0)
    def _():
        attend([(first_own - 2, all_far), (first_own - 1, ("prev", "far"))] + tail)

    @pl.when(first_own == 0)
    def _():
        attend(tail)

    outs = []
    for sub in range(Q_SUBTILES):
        for head in range(N_HEADS):
            hs = slice(head * HEAD_DIM, (head + 1) * HEAD_DIM)
            inv = 1.0 / l_ref[sub, head]
            acc = acc_ref[sub, hs, :].reshape(head_groups, SUBLANES, tq) * inv[None]
            acc_ref[sub, hs, :] = acc.reshape(HEAD_DIM, tq)
        outs.append(acc_ref[sub].T.astype(jnp.bfloat16))
    o = jnp.concatenate(outs, axis=0)
    o_ref[0] = x + jnp.dot(o, wo_ref[...], preferred_element_type=jnp.float32)


def _attn_call(x, g, wqt, wo, layer, k, vt, km, bias_tiles, rel_bias):
    B, T, D = x.shape
    nb = T // MOBA_BLOCK
    tq = MOBA_BLOCK
    n_q = Q_SUBTILES * tq
    return pl.pallas_call(
        _attn_kernel,
        grid=(B, T // n_q),
        in_specs=[
            pl.BlockSpec(memory_space=pltpu.SMEM),
            pl.BlockSpec((1, n_q, D), lambda b, i: (b, i, 0)),
            _resident((1, D)),
            _resident_layer(wqt.shape, layer),
            _resident_layer(wo.shape, layer),
            pl.BlockSpec((1, T, D), lambda b, i: (b, 0, 0)),
            pl.BlockSpec((1, D, T), lambda b, i: (b, 0, 0)),
            pl.BlockSpec((1, nb, D), lambda b, i: (b, 0, 0)),
            _resident(bias_tiles.shape),
        ],
        out_specs=pl.BlockSpec((1, n_q, D), lambda b, i: (b, i, 0)),
        out_shape=jax.ShapeDtypeStruct(x.shape, x.dtype),
        scratch_shapes=[
            pltpu.VMEM((Q_SUBTILES, N_HEADS, LANES, tq), jnp.bfloat16),
            pltpu.VMEM((nb - 1, N_HEADS, n_q), jnp.float32),
            pltpu.VMEM((Q_SUBTILES, N_HEADS, SUBLANES, tq), jnp.float32),
            pltpu.VMEM((Q_SUBTILES, N_HEADS, SUBLANES, tq), jnp.float32),
            pltpu.VMEM((Q_SUBTILES, D, tq), jnp.float32),
        ],
        compiler_params=pltpu.CompilerParams(
            dimension_semantics=("arbitrary", "arbitrary"),
            vmem_limit_bytes=VMEM_LIMIT),
        name="moba_attention",
    )(rel_bias, x, g.reshape(1, D), wqt, wo, k, vt, km, bias_tiles)


def kernel(x, norm_mixer, norm_ffn, pool_w, pool_scale, kv_norm, w_kv, w_q, w_o,
           rel_bias, w_gate_up, w_down, final_norm):
    bf = jnp.bfloat16
    assert x.shape[1] % MOBA_BLOCK == 0 and x.shape[2] == D_MODEL
    pool_w = pool_w.astype(bf)
    w_qt = jnp.swapaxes(w_q, 1, 2).astype(bf)
    w_o = w_o.astype(bf)
    wk = w_kv[:, :D_MODEL].astype(bf)
    wvt = w_kv[:, D_MODEL:].T.astype(bf)

    k = vt = km = bias_tiles = None
    for layer in range(DEPTH):
        if layer < N_A_LAYERS:
            x = _pool_call(x, norm_mixer[layer], pool_w, layer, pool_scale[layer])
        else:
            j = layer - N_A_LAYERS
            if j == 0:
                k, vt, km = _kv_call(x, kv_norm, wk, wvt)
                bias_tiles = _bias_call(rel_bias)
            x = _attn_call(x, norm_mixer[layer], w_qt, w_o, j, k, vt, km, bias_tiles, rel_bias)
        x = _ffn_call(x, norm_ffn[layer], w_gate_up, w_down, layer, final_norm,
                      final_norm=(layer == DEPTH - 1))
    return x
```

```python
import functools
import math

import jax
import jax.numpy as jnp
from jax import lax
from jax.experimental import pallas as pl
from jax.experimental.pallas import tpu as pltpu

D_MODEL = 1024
DEPTH = 4
N_A_LAYERS = DEPTH // 2
POOL_WINDOWS = (2, 4, 8, 16)
POOL_GROUP = D_MODEL // len(POOL_WINDOWS)
HEAD_DIM = 64
N_HEADS = D_MODEL // HEAD_DIM
MOBA_BLOCK = 256
MOBA_TOPK = 3
NUM_BUCKETS = 32
MAX_DISTANCE = 128
D_FF = -(-8 * D_MODEL // (3 * 256)) * 256
EPS = 1e-6

LANES = 128
SUBLANES = 8
PACKED_ROWS = 16
POOL_HALO = 16
assert all(w & (w - 1) == 0 and w <= POOL_HALO for w in POOL_WINDOWS)
HEADS_PER_LANE_TILE = LANES // HEAD_DIM
VMEM_LIMIT = 56 * 1024 * 1024
SCORE_LOOKAHEAD = 4
FAR_UNROLL = 2
Q_SUBTILES = 2

_NT = (((1,), (1,)), ((), ()))
_NEG_INF = float("-inf")
_MASKED = -1e30
LOG2E = math.log2(math.e)
ONES_ROWS = 16


def _rms(x, g):
    return x * lax.rsqrt(jnp.mean(x * x, axis=-1, keepdims=True) + EPS) * g


def _resident(shape):
    zeros = (0,) * len(shape)
    return pl.BlockSpec(shape, lambda *_: zeros, pipeline_mode=pl.Buffered(1))


def _resident_layer(stacked_shape, layer):
    index = (layer,) + (0,) * (len(stacked_shape) - 1)
    return pl.BlockSpec((None,) + tuple(stacked_shape[1:]), lambda *_: index,
                        pipeline_mode=pl.Buffered(1))


def _ffn_weight_views(w_gate_up, w_down):
    return w_gate_up, w_down.reshape(w_down.shape[0], -1, w_gate_up.shape[2])


def _cast_rider(stacks, layer, grid):
    n_steps = grid[0] * grid[1]
    in_specs, out_specs, out_shape = [], [], []
    for w in stacks:
        rows = w.shape[1] // n_steps
        assert rows * n_steps == w.shape[1] and rows % PACKED_ROWS == 0
        in_specs.append(pl.BlockSpec((None, rows, w.shape[2]),
                                     lambda b, i: (layer, b * grid[1] + i, 0)))
        out_specs.append(pl.BlockSpec((rows, w.shape[2]), lambda b, i: (b * grid[1] + i, 0)))
        out_shape.append(jax.ShapeDtypeStruct(w.shape[1:], jnp.bfloat16))
    return in_specs, out_specs, out_shape


def _cast_slabs(in_refs, out_refs):
    for src, dst in zip(in_refs, out_refs):
        dst[...] = src[...].astype(jnp.bfloat16)


def _pool_kernel(x_ref, halo_ref, g_ref, w_ref, scale_ref, wgu_f32_ref, wd_f32_ref,
                 o_ref, wgu_bf_ref, wd_bf_ref, hh_ref, *, tm):
    _cast_slabs((wgu_f32_ref, wd_f32_ref), (wgu_bf_ref, wd_bf_ref))
    i = pl.program_id(1)
    g = g_ref[...]
    x = x_ref[0]
    h = _rms(x, g)
    halo = _rms(halo_ref[0], g)
    hh_ref[0:POOL_HALO, :] = jnp.where(i > 0, halo, 0.0)
    hh_ref[POOL_HALO:, :] = h
    t = i * tm + lax.broadcasted_iota(jnp.int32, (tm, 1), 0)
    ys = []
    for gi, win in enumerate(POOL_WINDOWS):
        sl = slice(gi * POOL_GROUP, (gi + 1) * POOL_GROUP)
        wsum = hh_ref[:, sl]
        span = 1
        while span < win:
            wsum = wsum + pltpu.roll(wsum, span, axis=0)
            span *= 2
        wsum = wsum[POOL_HALO:]
        cnt = jnp.minimum(t + 1, win).astype(jnp.float32)
        d = wsum / cnt - h[:, sl]
        ys.append(jnp.dot(d.astype(jnp.bfloat16), w_ref[gi],
                          preferred_element_type=jnp.float32))
    y = jnp.concatenate(ys, axis=-1)
    o_ref[0] = x + y * scale_ref[...]


def _pool_call(x, g, w, layer, scale, ffn_stacks, tm=512):
    B, T, D = x.shape
    halo_blocks = tm // POOL_HALO
    grid = (B, T // tm)
    cast_in, cast_out, cast_shape = _cast_rider(ffn_stacks, layer, grid)
    return pl.pallas_call(
        functools.partial(_pool_kernel, tm=tm),
        grid=grid,
        in_specs=[
            pl.BlockSpec((1, tm, D), lambda b, i: (b, i, 0)),
            pl.BlockSpec((1, POOL_HALO, D),
                         lambda b, i: (b, jnp.maximum(i * halo_blocks - 1, 0), 0)),
            _resident((1, D)),
            _resident_layer(w.shape, layer),
            _resident((1, D)),
        ] + cast_in,
        out_specs=[pl.BlockSpec((1, tm, D), lambda b, i: (b, i, 0))] + cast_out,
        out_shape=[jax.ShapeDtypeStruct(x.shape, x.dtype)] + cast_shape,
        scratch_shapes=[pltpu.VMEM((tm + POOL_HALO, D), jnp.float32)],
        compiler_params=pltpu.CompilerParams(
            dimension_semantics=("arbitrary", "arbitrary"),
            vmem_limit_bytes=VMEM_LIMIT),
        name="pool_mixer",
    )(x, x, g.reshape(1, D), w, scale.reshape(1, D), *ffn_stacks)


def _ffn_kernel(x_ref, g_ref, wgu_ref, wd_ref, fg_ref, o_ref, act_ref, *, fc, final_norm):
    x = x_ref[...]
    h = _rms(x, g_ref[...]).astype(jnp.bfloat16)
    for c in range(D_FF // fc):
        gate = jnp.dot(h, wgu_ref[:, c * fc:(c + 1) * fc],
                       preferred_element_type=jnp.float32)
        up = jnp.dot(h, wgu_ref[:, D_FF + c * fc:D_FF + (c + 1) * fc],
                     preferred_element_type=jnp.float32)
        act = gate * (1.0 / (1.0 + jnp.exp(-gate))) * up
        act_ref[:, c * fc:(c + 1) * fc] = act.astype(jnp.bfloat16)
    y = x + jnp.dot(act_ref[...], wd_ref[...], preferred_element_type=jnp.float32)
    if final_norm:
        y = _rms(y, fg_ref[...])
    o_ref[...] = y


def _ffn_call(x, g, wgu, wd, final_g, final_norm, tm=1024, fc=256):
    B, T, D = x.shape
    xf = x.reshape(B * T, D)
    out = pl.pallas_call(
        functools.partial(_ffn_kernel, fc=fc, final_norm=final_norm),
        grid=(B * T // tm,),
        in_specs=[
            pl.BlockSpec((tm, D), lambda i: (i, 0)),
            _resident((1, D)),
            _resident(wgu.shape),
            _resident(wd.shape),
            _resident((1, D)),
        ],
        out_specs=pl.BlockSpec((tm, D), lambda i: (i, 0)),
        out_shape=jax.ShapeDtypeStruct(xf.shape, xf.dtype),
        scratch_shapes=[pltpu.VMEM((tm, D_FF), jnp.bfloat16)],
        compiler_params=pltpu.CompilerParams(
            dimension_semantics=("arbitrary",),
            vmem_limit_bytes=VMEM_LIMIT),
        name="swiglu_ffn",
    )(xf, g.reshape(1, D), wgu, wd, final_g.reshape(1, D))
    return out.reshape(B, T, D)


def _kv_kernel(x_ref, g_ref, wk_ref, wvt_ref, k_ref, vt_ref, km_ref, *, blocks_per_tile):
    j = pl.program_id(1)
    h = _rms(x_ref[0], g_ref[...]).astype(jnp.bfloat16)
    k = jnp.dot(h, wk_ref[...], preferred_element_type=jnp.float32)
    k_ref[0] = k.astype(jnp.bfloat16)
    vt = lax.dot_general(wvt_ref[...], h, _NT, preferred_element_type=jnp.float32)
    vt_ref[0] = vt.astype(jnp.bfloat16)
    for i in range(blocks_per_tile):
        km_ref[0, pl.ds(j * blocks_per_tile + i, 1), :] = jnp.mean(
            k[i * MOBA_BLOCK:(i + 1) * MOBA_BLOCK], axis=0, keepdims=True)


def _kv_call(x, g, wk, wvt, tm=1024):
    B, T, D = x.shape
    nb = T // MOBA_BLOCK
    return pl.pallas_call(
        functools.partial(_kv_kernel, blocks_per_tile=tm // MOBA_BLOCK),
        grid=(B, T // tm),
        in_specs=[
            pl.BlockSpec((1, tm, D), lambda b, j: (b, j, 0)),
            _resident((1, D)),
            _resident(wk.shape),
            _resident(wvt.shape),
        ],
        out_specs=[
            pl.BlockSpec((1, tm, D), lambda b, j: (b, j, 0)),
            pl.BlockSpec((1, D, tm), lambda b, j: (b, 0, j)),
            pl.BlockSpec((1, nb, D), lambda b, j: (b, 0, 0)),
        ],
        out_shape=[
            jax.ShapeDtypeStruct((B, T, D), jnp.bfloat16),
            jax.ShapeDtypeStruct((B, D, T), jnp.bfloat16),
            jax.ShapeDtypeStruct((B, nb, D), jnp.float32),
        ],
        compiler_params=pltpu.CompilerParams(
            dimension_semantics=("arbitrary", "arbitrary"),
            vmem_limit_bytes=VMEM_LIMIT),
        name="shared_kv",
    )(x, g.reshape(1, D), wk, wvt)


def _rel_bucket(n):
    max_exact = NUM_BUCKETS // 2
    nf = jnp.maximum(n, max_exact).astype(jnp.float32)
    large = max_exact + (jnp.log(nf / max_exact) / math.log(MAX_DISTANCE / max_exact)
                         * (NUM_BUCKETS - max_exact)).astype(jnp.int32)
    large = jnp.minimum(large, NUM_BUCKETS - 1)
    return jnp.where(n < max_exact, n, large)


def _bias_kernel(rb_ref, o_ref):
    hd = pl.program_id(0)
    kl = lax.broadcasted_iota(jnp.int32, (MOBA_BLOCK, MOBA_BLOCK), 0)
    ql = lax.broadcasted_iota(jnp.int32, (MOBA_BLOCK, MOBA_BLOCK), 1)
    d_own = ql - kl
    for ti, dist in enumerate((d_own, d_own + MOBA_BLOCK)):
        bucket = _rel_bucket(jnp.maximum(dist, 0))
        tile = jnp.zeros((MOBA_BLOCK, MOBA_BLOCK), jnp.float32)
        for bi in range(NUM_BUCKETS):
            tile = jnp.where(bucket == bi, rb_ref[bi, hd], tile)
        if ti == 0:
            tile = jnp.where(dist >= 0, tile, _MASKED)
        o_ref[0, ti] = tile * LOG2E


def _bias_call(rel_bias):
    return pl.pallas_call(
        _bias_kernel,
        grid=(N_HEADS,),
        in_specs=[pl.BlockSpec(memory_space=pltpu.SMEM)],
        out_specs=pl.BlockSpec((1, 2, MOBA_BLOCK, MOBA_BLOCK), lambda h: (h, 0, 0, 0)),
        out_shape=jax.ShapeDtypeStruct((N_HEADS, 2, MOBA_BLOCK, MOBA_BLOCK), jnp.float32),
        compiler_params=pltpu.CompilerParams(dimension_semantics=("arbitrary",)),
        name="rel_bias_tiles",
    )(rel_bias)


def _sublane_all(op, x):
    shift = SUBLANES // 2
    while shift:
        x = op(x, pltpu.roll(x, shift, axis=0))
        shift //= 2
    return x


def _attn_kernel(rb_ref, x_ref, g_ref, wqt_ref, wo_ref, k_ref, vt_ref, km_ref, bias_ref,
                 wgu_f32_ref, wd_f32_ref, o_ref, wgu_bf_ref, wd_bf_ref,
                 qh_ref, mask_ref, m_ref, l_ref, acc_ref):
    _cast_slabs((wgu_f32_ref, wd_f32_ref), (wgu_bf_ref, wd_bf_ref))
    first_own = pl.program_id(1) * Q_SUBTILES
    nb = km_ref.shape[1]
    tq = MOBA_BLOCK
    D = x_ref.shape[2]
    row_groups = MOBA_BLOCK // SUBLANES
    head_groups = HEAD_DIM // SUBLANES
    x = x_ref[0]
    h = _rms(x, g_ref[...]).astype(jnp.bfloat16)
    qf = lax.dot_general(wqt_ref[...], h, _NT, preferred_element_type=jnp.float32)
    q_gate = (qf * HEAD_DIM ** -0.5).astype(jnp.bfloat16)
    q = (qf * (HEAD_DIM ** -0.5 * LOG2E)).astype(jnp.bfloat16)

    feat = lax.broadcasted_iota(jnp.int32, (LANES, 1), 0)
    for sub in range(Q_SUBTILES):
        for head in range(N_HEADS):
            p, hh = divmod(head, HEADS_PER_LANE_TILE)
            qp = q[p * LANES:(p + 1) * LANES, sub * tq:(sub + 1) * tq]
            in_head = (feat >= hh * HEAD_DIM) & (feat < (hh + 1) * HEAD_DIM)
            qh_ref[sub, head] = jnp.where(in_head, qp, jnp.zeros_like(qp))

    n_q = Q_SUBTILES * tq
    n_cand = nb - 1
    own_of_query = first_own + lax.broadcasted_iota(jnp.int32, (N_HEADS, n_q), 1) // tq
    past = [own_of_query > j for j in range(n_cand)]
    some_query_ranks = first_own + Q_SUBTILES - 1 > MOBA_TOPK

    @pl.when(some_query_ranks)
    def _():
        km = jnp.broadcast_to(km_ref[0, :n_cand][:, None, :], (n_cand, N_HEADS, D))
        km = km.reshape(n_cand * N_HEADS, D)
        row_head = lax.broadcasted_iota(jnp.int32, km.shape, 0) % N_HEADS
        col_head = lax.broadcasted_iota(jnp.int32, km.shape, 1) // HEAD_DIM
        km_bd = jnp.where(row_head == col_head, km, 0.0).astype(jnp.bfloat16)
        gate = jnp.dot(km_bd, q_gate, preferred_element_type=jnp.float32)
        gates = [jnp.where(past[j], gate[j * N_HEADS:(j + 1) * N_HEADS], _NEG_INF)
                 for j in range(n_cand)]
        rank = [jnp.zeros((N_HEADS, n_q), jnp.int32) for _ in range(n_cand)]
        for lo in range(n_cand):
            for hi in range(lo + 1, n_cand):
                lo_wins = jnp.where(gates[lo] >= gates[hi], 1, 0)
                rank[hi] = rank[hi] + lo_wins
                rank[lo] = rank[lo] + (1 - lo_wins)
        for j in range(n_cand):
            keep = jnp.where(past[j], 0.0, _MASKED)
            mask_ref[j] = jnp.where(rank[j] < MOBA_TOPK, keep, _MASKED)

    @pl.when(jnp.logical_not(some_query_ranks))
    def _():
        for j in range(n_cand):
            mask_ref[j] = jnp.where(past[j], 0.0, _MASKED)

    m_ref[...] = jnp.full(m_ref.shape, _MASKED, jnp.float32)
    l_ref[...] = jnp.zeros(l_ref.shape, jnp.float32)
    acc_ref[...] = jnp.zeros(acc_ref.shape, jnp.float32)

    def attend(blocks):
        units = []
        for j, kinds in blocks:
            keys = pl.ds(pl.multiple_of(j * MOBA_BLOCK, MOBA_BLOCK), MOBA_BLOCK)
            units += [(j, keys, kind, sub, head) for head in range(N_HEADS)
                      for sub, kind in enumerate(kinds) if kind is not None]

        def scores(unit):
            _, keys, _, sub, head = unit
            p = head // HEADS_PER_LANE_TILE
            kj = k_ref[0, keys, p * LANES:(p + 1) * LANES]
            return jnp.dot(kj, qh_ref[sub, head], preferred_element_type=jnp.float32)

        def softmax(unit, s):
            j, _, kind, sub, head = unit
            if kind != "far":
                tile = bias_ref[head, 0 if kind == "own" else 1]
                s = (s.reshape(row_groups, SUBLANES, tq)
                     + tile.reshape(row_groups, SUBLANES, tq)).reshape(MOBA_BLOCK, tq)
            sb = s.astype(jnp.bfloat16).reshape(MOBA_BLOCK // PACKED_ROWS, PACKED_ROWS, tq)
            bm = jnp.max(sb, axis=0).astype(jnp.float32)
            block_max = _sublane_all(jnp.maximum, jnp.maximum(bm[:SUBLANES], bm[SUBLANES:]))
            m_old = m_ref[sub, head]
            bias = rb_ref[NUM_BUCKETS - 1, head] * LOG2E if kind == "far" else 0.0
            top = block_max + bias
            if kind != "own":
                selected = jnp.broadcast_to(
                    mask_ref[j, head:head + 1, sub * tq:(sub + 1) * tq], (SUBLANES, tq)) == 0.0
                top = jnp.where(selected, top, _MASKED)
            shift = jnp.maximum(jnp.maximum(m_old, top) - bias, block_max)
            shift = shift.astype(jnp.bfloat16).astype(jnp.float32)
            shift_b = jnp.concatenate([shift, shift], axis=0).astype(jnp.bfloat16)
            pj = jnp.exp2(sb - shift_b[None])
            ref = shift + bias
            if kind != "own":
                m_new = jnp.where(selected, jnp.maximum(m_old, ref), m_old)
                scale_new = jnp.where(selected, jnp.exp2(ref - m_new), 0.0)
            else:
                m_new = jnp.maximum(m_old, ref)
                scale_new = jnp.exp2(ref - m_new)
            scale_old = jnp.exp2(m_old - m_new)
            m_ref[sub, head] = m_new
            return pj.reshape(MOBA_BLOCK, tq), (scale_old, scale_new)

        def accumulate(unit, pv, scales):
            _, _, _, sub, head = unit
            scale_old, scale_new = scales
            hs = slice(head * HEAD_DIM, (head + 1) * HEAD_DIM)
            l_ref[sub, head] = (l_ref[sub, head] * scale_old
                                + pv[HEAD_DIM:HEAD_DIM + SUBLANES] * scale_new)
            acc = acc_ref[sub, hs, :].reshape(head_groups, SUBLANES, tq)
            acc = (acc * scale_old[None]
                   + pv[:HEAD_DIM].reshape(head_groups, SUBLANES, tq) * scale_new[None])
            acc_ref[sub, hs, :] = acc.reshape(HEAD_DIM, tq)

        ones = jnp.ones((ONES_ROWS, MOBA_BLOCK), jnp.bfloat16)

        def weighted_values(unit, pj):
            _, keys, _, _, head = unit
            vj = vt_ref[0, head * HEAD_DIM:(head + 1) * HEAD_DIM, keys]
            return jnp.dot(jnp.concatenate([vj, ones], axis=0), pj,
                           preferred_element_type=jnp.float32)

        n = len(units)
        pending_scores = {i: scores(units[i]) for i in range(SCORE_LOOKAHEAD)}
        probs = {}
        products = {}
        for step in range(n + 2):
            if step + SCORE_LOOKAHEAD < n:
                pending_scores[step + SCORE_LOOKAHEAD] = scores(units[step + SCORE_LOOKAHEAD])
            if 0 <= step - 1 < n:
                pj, alpha = probs.pop(step - 1)
                products[step - 1] = (weighted_values(units[step - 1], pj), alpha)
            if step < n:
                probs[step] = softmax(units[step], pending_scores.pop(step))
            if 0 <= step - 2 < n:
                accumulate(units[step - 2], *products.pop(step - 2))

    assert Q_SUBTILES == 2 and FAR_UNROLL == 2
    all_far = ("far", "far")
    tail = [(first_own, ("own", "prev")), (first_own + 1, (None, "own"))]

    def far_pair(t, carry):
        attend([(FAR_UNROLL * t, all_far), (FAR_UNROLL * t + 1, all_far)])
        return carry

    lax.fori_loop(0, jnp.maximum(first_own - 1, 0) // FAR_UNROLL, far_pair, 0)

    @pl.when(first_own > 0)
    def _():
        attend([(first_own - 2, all_far), (first_own - 1, ("prev", "far"))] + tail)

    @pl.when(first_own == 0)
    def _():
        attend(tail)

    outs = []
    for sub in range(Q_SUBTILES):
        for head in range(N_HEADS):
            hs = slice(head * HEAD_DIM, (head + 1) * HEAD_DIM)
            inv = 1.0 / l_ref[sub, head]
            acc = acc_ref[sub, hs, :].reshape(head_groups, SUBLANES, tq) * inv[None]
            acc_ref[sub, hs, :] = acc.reshape(HEAD_DIM, tq)
        outs.append(acc_ref[sub].T.astype(jnp.bfloat16))
    o = jnp.concatenate(outs, axis=0)
    o_ref[0] = x + jnp.dot(o, wo_ref[...], preferred_element_type=jnp.float32)


def _attn_call(x, g, wqt, wo, layer, k, vt, km, bias_tiles, rel_bias, ffn_stacks, ffn_layer):
    B, T, D = x.shape
    nb = T // MOBA_BLOCK
    tq = MOBA_BLOCK
    n_q = Q_SUBTILES * tq
    grid = (B, T // n_q)
    cast_in, cast_out, cast_shape = _cast_rider(ffn_stacks, ffn_layer, grid)
    return pl.pallas_call(
        _attn_kernel,
        grid=grid,
        in_specs=[
            pl.BlockSpec(memory_space=pltpu.SMEM),
            pl.BlockSpec((1, n_q, D), lambda b, i: (b, i, 0)),
            _resident((1, D)),
            _resident_layer(wqt.shape, layer),
            _resident_layer(wo.shape, layer),
            pl.BlockSpec((1, T, D), lambda b, i: (b, 0, 0)),
            pl.BlockSpec((1, D, T), lambda b, i: (b, 0, 0)),
            pl.BlockSpec((1, nb, D), lambda b, i: (b, 0, 0)),
            _resident(bias_tiles.shape),
        ] + cast_in,
        out_specs=[pl.BlockSpec((1, n_q, D), lambda b, i: (b, i, 0))] + cast_out,
        out_shape=[jax.ShapeDtypeStruct(x.shape, x.dtype)] + cast_shape,
        scratch_shapes=[
            pltpu.VMEM((Q_SUBTILES, N_HEADS, LANES, tq), jnp.bfloat16),
            pltpu.VMEM((nb - 1, N_HEADS, n_q), jnp.float32),
            pltpu.VMEM((Q_SUBTILES, N_HEADS, SUBLANES, tq), jnp.float32),
            pltpu.VMEM((Q_SUBTILES, N_HEADS, SUBLANES, tq), jnp.float32),
            pltpu.VMEM((Q_SUBTILES, D, tq), jnp.float32),
        ],
        compiler_params=pltpu.CompilerParams(
            dimension_semantics=("arbitrary", "arbitrary"),
            vmem_limit_bytes=VMEM_LIMIT),
        name="moba_attention",
    )(rel_bias, x, g.reshape(1, D), wqt, wo, k, vt, km, bias_tiles, *ffn_stacks)


def kernel(x, norm_mixer, norm_ffn, pool_w, pool_scale, kv_norm, w_kv, w_q, w_o,
           rel_bias, w_gate_up, w_down, final_norm):
    bf = jnp.bfloat16
    assert x.shape[1] % MOBA_BLOCK == 0 and x.shape[2] == D_MODEL
    pool_w = pool_w.astype(bf)
    w_qt = jnp.swapaxes(w_q, 1, 2).astype(bf)
    w_o = w_o.astype(bf)
    wk = w_kv[:, :D_MODEL].astype(bf)
    wvt = w_kv[:, D_MODEL:].T.astype(bf)

    ffn_stacks = _ffn_weight_views(w_gate_up, w_down)

    k = vt = km = bias_tiles = None
    for layer in range(DEPTH):
        if layer < N_A_LAYERS:
            x, wgu, wd = _pool_call(x, norm_mixer[layer], pool_w, layer, pool_scale[layer],
                                    ffn_stacks)
        else:
            j = layer - N_A_LAYERS
            if j == 0:
                k, vt, km = _kv_call(x, kv_norm, wk, wvt)
                bias_tiles = _bias_call(rel_bias)
            x, wgu, wd = _attn_call(x, norm_mixer[layer], w_qt, w_o, j, k, vt, km, bias_tiles,
                                    rel_bias, ffn_stacks, layer)
        x = _ffn_call(x, norm_ffn[layer], wgu, wd.reshape(w_down.shape[1:]), final_norm,
                      final_norm=(layer == DEPTH - 1))
    return x
```

```python
import functools
import math

import jax
import jax.numpy as jnp
from jax import lax
from jax.experimental import pallas as pl
from jax.experimental.pallas import tpu as pltpu

D_MODEL = 1024
DEPTH = 4
N_A_LAYERS = DEPTH // 2
POOL_WINDOWS = (2, 4, 8, 16)
POOL_GROUP = D_MODEL // len(POOL_WINDOWS)
HEAD_DIM = 64
N_HEADS = D_MODEL // HEAD_DIM
MOBA_BLOCK = 256
MOBA_TOPK = 3
NUM_BUCKETS = 32
MAX_DISTANCE = 128
D_FF = -(-8 * D_MODEL // (3 * 256)) * 256
EPS = 1e-6

LANES = 128
SUBLANES = 8
PACKED_ROWS = 16
POOL_HALO = 16
assert all(w & (w - 1) == 0 and w <= POOL_HALO for w in POOL_WINDOWS)
HEADS_PER_LANE_TILE = LANES // HEAD_DIM
VMEM_LIMIT = 56 * 1024 * 1024
SCORE_LOOKAHEAD = 4
FAR_UNROLL = 2
Q_SUBTILES = 2

_NT = (((1,), (1,)), ((), ()))
_NEG_INF = float("-inf")
_MASKED = -1e30
LOG2E = math.log2(math.e)
ONES_ROWS = 16


def _rms(x, g):
    return x * lax.rsqrt(jnp.mean(x * x, axis=-1, keepdims=True) + EPS) * g


def _resident(shape):
    zeros = (0,) * len(shape)
    return pl.BlockSpec(shape, lambda *_: zeros, pipeline_mode=pl.Buffered(1))


def _resident_layer(stacked_shape, layer):
    index = (layer,) + (0,) * (len(stacked_shape) - 1)
    return pl.BlockSpec((None,) + tuple(stacked_shape[1:]), lambda *_: index,
                        pipeline_mode=pl.Buffered(1))


def _cast_rider(stacks, layer, n_steps, step_of):
    in_specs, out_specs, out_shape = [], [], []
    for w in stacks:
        n_slabs = n_steps if (w.shape[1] // n_steps) % PACKED_ROWS == 0 else n_steps // 2
        rows = w.shape[1] // n_slabs
        assert rows * n_slabs == w.shape[1] and rows % PACKED_ROWS == 0 and n_steps % n_slabs == 0
        in_specs.append(pl.BlockSpec(
            (None, rows, w.shape[2]),
            lambda *idx, n_slabs=n_slabs: (layer, step_of(*idx) * n_slabs // n_steps, 0)))
        out_specs.append(pl.BlockSpec(
            (rows, w.shape[2]),
            lambda *idx, n_slabs=n_slabs: (step_of(*idx) * n_slabs // n_steps, 0)))
        out_shape.append(jax.ShapeDtypeStruct(w.shape[1:], jnp.bfloat16))
    return in_specs, out_specs, out_shape


def _cast_slabs(in_refs, out_refs):
    for src, dst in zip(in_refs, out_refs):
        dst[...] = src[...].astype(jnp.bfloat16)


def _pool_kernel(x_ref, halo_ref, g_ref, w_ref, scale_ref, *rest, tm, n_cast):
    cast_in, (o_ref, *cast_out, hh_ref) = rest[:n_cast], rest[n_cast:]
    _cast_slabs(cast_in, cast_out)
    i = pl.program_id(1)
    g = g_ref[...]
    x = x_ref[0]
    h = _rms(x, g)
    halo = _rms(halo_ref[0], g)
    hh_ref[0:POOL_HALO, :] = jnp.where(i > 0, halo, 0.0)
    hh_ref[POOL_HALO:, :] = h
    t = i * tm + lax.broadcasted_iota(jnp.int32, (tm, 1), 0)
    ys = []
    for gi, win in enumerate(POOL_WINDOWS):
        sl = slice(gi * POOL_GROUP, (gi + 1) * POOL_GROUP)
        wsum = hh_ref[:, sl]
        span = 1
        while span < win:
            wsum = wsum + pltpu.roll(wsum, span, axis=0)
            span *= 2
        wsum = wsum[POOL_HALO:]
        cnt = jnp.minimum(t + 1, win).astype(jnp.float32)
        d = wsum / cnt - h[:, sl]
        ys.append(jnp.dot(d.astype(jnp.bfloat16), w_ref[gi],
                          preferred_element_type=jnp.float32))
    y = jnp.concatenate(ys, axis=-1)
    o_ref[0] = x + y * scale_ref[...]


def _pool_call(x, g, w, layer, scale, cast_stacks=(), cast_layer=0, tm=512):
    B, T, D = x.shape
    halo_blocks = tm // POOL_HALO
    grid = (B, T // tm)
    cast_in, cast_out, cast_shape = _cast_rider(
        cast_stacks, cast_layer, grid[0] * grid[1], lambda b, i: b * grid[1] + i)
    return pl.pallas_call(
        functools.partial(_pool_kernel, tm=tm, n_cast=len(cast_stacks)),
        grid=grid,
        in_specs=[
            pl.BlockSpec((1, tm, D), lambda b, i: (b, i, 0)),
            pl.BlockSpec((1, POOL_HALO, D),
                         lambda b, i: (b, jnp.maximum(i * halo_blocks - 1, 0), 0)),
            _resident((1, D)),
            _resident_layer(w.shape, layer),
            _resident((1, D)),
        ] + cast_in,
        out_specs=[pl.BlockSpec((1, tm, D), lambda b, i: (b, i, 0))] + cast_out,
        out_shape=[jax.ShapeDtypeStruct(x.shape, x.dtype)] + cast_shape,
        scratch_shapes=[pltpu.VMEM((tm + POOL_HALO, D), jnp.float32)],
        compiler_params=pltpu.CompilerParams(
            dimension_semantics=("arbitrary", "arbitrary"),
            vmem_limit_bytes=VMEM_LIMIT),
        name="pool_mixer",
    )(x, x, g.reshape(1, D), w, scale.reshape(1, D), *cast_stacks)


def _ffn_kernel(x_ref, g_ref, wgu_ref, wd_ref, fg_ref, *rest, fc, final_norm, n_cast):
    cast_in, (o_ref, *cast_out, act_ref) = rest[:n_cast], rest[n_cast:]
    _cast_slabs(cast_in, cast_out)
    x = x_ref[...]
    h = _rms(x, g_ref[...]).astype(jnp.bfloat16)
    for c in range(D_FF // fc):
        gate = jnp.dot(h, wgu_ref[:, c * fc:(c + 1) * fc],
                       preferred_element_type=jnp.float32)
        up = jnp.dot(h, wgu_ref[:, D_FF + c * fc:D_FF + (c + 1) * fc],
                     preferred_element_type=jnp.float32)
        act = gate * (1.0 / (1.0 + jnp.exp(-gate))) * up
        act_ref[:, c * fc:(c + 1) * fc] = act.astype(jnp.bfloat16)
    y = x + jnp.dot(act_ref[...], wd_ref[...], preferred_element_type=jnp.float32)
    if final_norm:
        y = _rms(y, fg_ref[...])
    o_ref[...] = y


def _ffn_call(x, g, wgu, wd, final_g, final_norm, cast_stacks=(), cast_layer=0,
              tm=1024, fc=256):
    B, T, D = x.shape
    xf = x.reshape(B * T, D)
    n_steps = B * T // tm
    cast_in, cast_out, cast_shape = _cast_rider(cast_stacks, cast_layer, n_steps, lambda i: i)
    out, *cast = pl.pallas_call(
        functools.partial(_ffn_kernel, fc=fc, final_norm=final_norm, n_cast=len(cast_stacks)),
        grid=(n_steps,),
        in_specs=[
            pl.BlockSpec((tm, D), lambda i: (i, 0)),
            _resident((1, D)),
            _resident(wgu.shape),
            _resident(wd.shape),
            _resident((1, D)),
        ] + cast_in,
        out_specs=[pl.BlockSpec((tm, D), lambda i: (i, 0))] + cast_out,
        out_shape=[jax.ShapeDtypeStruct(xf.shape, xf.dtype)] + cast_shape,
        scratch_shapes=[pltpu.VMEM((tm, D_FF), jnp.bfloat16)],
        compiler_params=pltpu.CompilerParams(
            dimension_semantics=("arbitrary",),
            vmem_limit_bytes=VMEM_LIMIT),
        name="swiglu_ffn",
    )(xf, g.reshape(1, D), wgu, wd, final_g.reshape(1, D), *cast_stacks)
    return [out.reshape(B, T, D)] + cast


def _kv_kernel(x_ref, g_ref, wk_ref, wvt_ref, k_ref, vt_ref, km_ref, *, blocks_per_tile):
    j = pl.program_id(1)
    h = _rms(x_ref[0], g_ref[...]).astype(jnp.bfloat16)
    k = jnp.dot(h, wk_ref[...], preferred_element_type=jnp.float32)
    k_ref[0] = k.astype(jnp.bfloat16)
    vt = lax.dot_general(wvt_ref[...], h, _NT, preferred_element_type=jnp.float32)
    vt_ref[0] = vt.astype(jnp.bfloat16)
    for i in range(blocks_per_tile):
        km_ref[0, pl.ds(j * blocks_per_tile + i, 1), :] = jnp.mean(
            k[i * MOBA_BLOCK:(i + 1) * MOBA_BLOCK], axis=0, keepdims=True)


def _kv_call(x, g, wk, wvt, tm=1024):
    B, T, D = x.shape
    nb = T // MOBA_BLOCK
    return pl.pallas_call(
        functools.partial(_kv_kernel, blocks_per_tile=tm // MOBA_BLOCK),
        grid=(B, T // tm),
        in_specs=[
            pl.BlockSpec((1, tm, D), lambda b, j: (b, j, 0)),
            _resident((1, D)),
            _resident(wk.shape),
            _resident(wvt.shape),
        ],
        out_specs=[
            pl.BlockSpec((1, tm, D), lambda b, j: (b, j, 0)),
            pl.BlockSpec((1, D, tm), lambda b, j: (b, 0, j)),
            pl.BlockSpec((1, nb, D), lambda b, j: (b, 0, 0)),
        ],
        out_shape=[
            jax.ShapeDtypeStruct((B, T, D), jnp.bfloat16),
            jax.ShapeDtypeStruct((B, D, T), jnp.bfloat16),
            jax.ShapeDtypeStruct((B, nb, D), jnp.float32),
        ],
        compiler_params=pltpu.CompilerParams(
            dimension_semantics=("arbitrary", "arbitrary"),
            vmem_limit_bytes=VMEM_LIMIT),
        name="shared_kv",
    )(x, g.reshape(1, D), wk, wvt)


def _rel_bucket(n):
    max_exact = NUM_BUCKETS // 2
    nf = jnp.maximum(n, max_exact).astype(jnp.float32)
    large = max_exact + (jnp.log(nf / max_exact) / math.log(MAX_DISTANCE / max_exact)
                         * (NUM_BUCKETS - max_exact)).astype(jnp.int32)
    large = jnp.minimum(large, NUM_BUCKETS - 1)
    return jnp.where(n < max_exact, n, large)


def _bias_kernel(rb_ref, o_ref):
    hd = pl.program_id(0)
    kl = lax.broadcasted_iota(jnp.int32, (MOBA_BLOCK, MOBA_BLOCK), 0)
    ql = lax.broadcasted_iota(jnp.int32, (MOBA_BLOCK, MOBA_BLOCK), 1)
    d_own = ql - kl
    for ti, dist in enumerate((d_own, d_own + MOBA_BLOCK)):
        bucket = _rel_bucket(jnp.maximum(dist, 0))
        tile = jnp.zeros((MOBA_BLOCK, MOBA_BLOCK), jnp.float32)
        for bi in range(NUM_BUCKETS):
            tile = jnp.where(bucket == bi, rb_ref[bi, hd], tile)
        if ti == 0:
            tile = jnp.where(dist >= 0, tile, _MASKED)
        o_ref[0, ti] = tile * LOG2E


def _bias_call(rel_bias):
    return pl.pallas_call(
        _bias_kernel,
        grid=(N_HEADS,),
        in_specs=[pl.BlockSpec(memory_space=pltpu.SMEM)],
        out_specs=pl.BlockSpec((1, 2, MOBA_BLOCK, MOBA_BLOCK), lambda h: (h, 0, 0, 0)),
        out_shape=jax.ShapeDtypeStruct((N_HEADS, 2, MOBA_BLOCK, MOBA_BLOCK), jnp.float32),
        compiler_params=pltpu.CompilerParams(dimension_semantics=("arbitrary",)),
        name="rel_bias_tiles",
    )(rel_bias)


def _sublane_all(op, x):
    shift = SUBLANES // 2
    while shift:
        x = op(x, pltpu.roll(x, shift, axis=0))
        shift //= 2
    return x


def _attn_kernel(rb_ref, x_ref, g_ref, wqt_ref, wo_ref, k_ref, vt_ref, km_ref, bias_ref,
                 o_ref, qh_ref, mask_ref, m_ref, l_ref, acc_ref):
    first_own = pl.program_id(1) * Q_SUBTILES
    nb = km_ref.shape[1]
    tq = MOBA_BLOCK
    D = x_ref.shape[2]
    row_groups = MOBA_BLOCK // SUBLANES
    head_groups = HEAD_DIM // SUBLANES
    x = x_ref[0]
    h = _rms(x, g_ref[...]).astype(jnp.bfloat16)
    qf = lax.dot_general(wqt_ref[...], h, _NT, preferred_element_type=jnp.float32)
    q_gate = (qf * HEAD_DIM ** -0.5).astype(jnp.bfloat16)
    q = (qf * (HEAD_DIM ** -0.5 * LOG2E)).astype(jnp.bfloat16)

    feat = lax.broadcasted_iota(jnp.int32, (LANES, 1), 0)
    for sub in range(Q_SUBTILES):
        for head in range(N_HEADS):
            p, hh = divmod(head, HEADS_PER_LANE_TILE)
            qp = q[p * LANES:(p + 1) * LANES, sub * tq:(sub + 1) * tq]
            in_head = (feat >= hh * HEAD_DIM) & (feat < (hh + 1) * HEAD_DIM)
            qh_ref[sub, head] = jnp.where(in_head, qp, jnp.zeros_like(qp))

    n_q = Q_SUBTILES * tq
    n_cand = nb - 1
    own_of_query = first_own + lax.broadcasted_iota(jnp.int32, (N_HEADS, n_q), 1) // tq
    past = [own_of_query > j for j in range(n_cand)]
    some_query_ranks = first_own + Q_SUBTILES - 1 > MOBA_TOPK

    @pl.when(some_query_ranks)
    def _():
        km = jnp.broadcast_to(km_ref[0, :n_cand][:, None, :], (n_cand, N_HEADS, D))
        km = km.reshape(n_cand * N_HEADS, D)
        row_head = lax.broadcasted_iota(jnp.int32, km.shape, 0) % N_HEADS
        col_head = lax.broadcasted_iota(jnp.int32, km.shape, 1) // HEAD_DIM
        km_bd = jnp.where(row_head == col_head, km, 0.0).astype(jnp.bfloat16)
        gate = jnp.dot(km_bd, q_gate, preferred_element_type=jnp.float32)
        gates = [jnp.where(past[j], gate[j * N_HEADS:(j + 1) * N_HEADS], _NEG_INF)
                 for j in range(n_cand)]
        rank = [jnp.zeros((N_HEADS, n_q), jnp.int32) for _ in range(n_cand)]
        for lo in range(n_cand):
            for hi in range(lo + 1, n_cand):
                lo_wins = jnp.where(gates[lo] >= gates[hi], 1, 0)
                rank[hi] = rank[hi] + lo_wins
                rank[lo] = rank[lo] + (1 - lo_wins)
        for j in range(n_cand):
            keep = jnp.where(past[j], 0.0, _MASKED)
            mask_ref[j] = jnp.where(rank[j] < MOBA_TOPK, keep, _MASKED)

    @pl.when(jnp.logical_not(some_query_ranks))
    def _():
        for j in range(n_cand):
            mask_ref[j] = jnp.where(past[j], 0.0, _MASKED)

    m_ref[...] = jnp.full(m_ref.shape, _MASKED, jnp.float32)
    l_ref[...] = jnp.zeros(l_ref.shape, jnp.float32)
    acc_ref[...] = jnp.zeros(acc_ref.shape, jnp.float32)

    def attend(blocks):
        units = []
        for j, kinds in blocks:
            keys = pl.ds(pl.multiple_of(j * MOBA_BLOCK, MOBA_BLOCK), MOBA_BLOCK)
            units += [(j, keys, kind, sub, head) for head in range(N_HEADS)
                      for sub, kind in enumerate(kinds) if kind is not None]

        def scores(unit):
            _, keys, _, sub, head = unit
            p = head // HEADS_PER_LANE_TILE
            kj = k_ref[0, keys, p * LANES:(p + 1) * LANES]
            return jnp.dot(kj, qh_ref[sub, head], preferred_element_type=jnp.float32)

        def softmax(unit, s):
            j, _, kind, sub, head = unit
            if kind != "far":
                tile = bias_ref[head, 0 if kind == "own" else 1]
                s = (s.reshape(row_groups, SUBLANES, tq)
                     + tile.reshape(row_groups, SUBLANES, tq)).reshape(MOBA_BLOCK, tq)
            sb = s.astype(jnp.bfloat16).reshape(MOBA_BLOCK // PACKED_ROWS, PACKED_ROWS, tq)
            bm = jnp.max(sb, axis=0).astype(jnp.float32)
            block_max = _sublane_all(jnp.maximum, jnp.maximum(bm[:SUBLANES], bm[SUBLANES:]))
            m_old = m_ref[sub, head]
            bias = rb_ref[NUM_BUCKETS - 1, head] * LOG2E if kind == "far" else 0.0
            top = block_max + bias
            if kind != "own":
                selected = jnp.broadcast_to(
                    mask_ref[j, head:head + 1, sub * tq:(sub + 1) * tq], (SUBLANES, tq)) == 0.0
                top = jnp.where(selected, top, _MASKED)
            shift = jnp.maximum(jnp.maximum(m_old, top) - bias, block_max)
            shift = shift.astype(jnp.bfloat16).astype(jnp.float32)
            shift_b = jnp.concatenate([shift, shift], axis=0).astype(jnp.bfloat16)
            pj = jnp.exp2(sb - shift_b[None])
            ref = shift + bias
            if kind != "own":
                m_new = jnp.where(selected, jnp.maximum(m_old, ref), m_old)
                scale_new = jnp.where(selected, jnp.exp2(ref - m_new), 0.0)
            else:
                m_new = jnp.maximum(m_old, ref)
                scale_new = jnp.exp2(ref - m_new)
            scale_old = jnp.exp2(m_old - m_new)
            m_ref[sub, head] = m_new
            return pj.reshape(MOBA_BLOCK, tq), (scale_old, scale_new)

        def accumulate(unit, pv, scales):
            _, _, _, sub, head = unit
            scale_old, scale_new = scales
            hs = slice(head * HEAD_DIM, (head + 1) * HEAD_DIM)
            l_ref[sub, head] = (l_ref[sub, head] * scale_old
                                + pv[HEAD_DIM:HEAD_DIM + SUBLANES] * scale_new)
            acc = acc_ref[sub, hs, :].reshape(head_groups, SUBLANES, tq)
            acc = (acc * scale_old[None]
                   + pv[:HEAD_DIM].reshape(head_groups, SUBLANES, tq) * scale_new[None])
            acc_ref[sub, hs, :] = acc.reshape(HEAD_DIM, tq)

        ones = jnp.ones((ONES_ROWS, MOBA_BLOCK), jnp.bfloat16)

        def weighted_values(unit, pj):
            _, keys, _, _, head = unit
            vj = vt_ref[0, head * HEAD_DIM:(head + 1) * HEAD_DIM, keys]
            return jnp.dot(jnp.concatenate([vj, ones], axis=0), pj,
                           preferred_element_type=jnp.float32)

        n = len(units)
        pending_scores = {i: scores(units[i]) for i in range(SCORE_LOOKAHEAD)}
        probs = {}
        products = {}
        for step in range(n + 2):
            if step + SCORE_LOOKAHEAD < n:
                pending_scores[step + SCORE_LOOKAHEAD] = scores(units[step + SCORE_LOOKAHEAD])
            if 0 <= step - 1 < n:
                pj, alpha = probs.pop(step - 1)
                products[step - 1] = (weighted_values(units[step - 1], pj), alpha)
            if step < n:
                probs[step] = softmax(units[step], pending_scores.pop(step))
            if 0 <= step - 2 < n:
                accumulate(units[step - 2], *products.pop(step - 2))

    assert Q_SUBTILES == 2 and FAR_UNROLL == 2
    all_far = ("far", "far")
    tail = [(first_own, ("own", "prev")), (first_own + 1, (None, "own"))]

    def far_pair(t, carry):
        attend([(FAR_UNROLL * t, all_far), (FAR_UNROLL * t + 1, all_far)])
        return carry

    lax.fori_loop(0, jnp.maximum(first_own - 1, 0) // FAR_UNROLL, far_pair, 0)

    @pl.when(first_own > 0)
    def _():
        attend([(first_own - 2, all_far), (first_own - 1, ("prev", "far"))] + tail)

    @pl.when(first_own == 0)
    def _():
        attend(tail)

    outs = []
    for sub in range(Q_SUBTILES):
        for head in range(N_HEADS):
            hs = slice(head * HEAD_DIM, (head + 1) * HEAD_DIM)
            inv = 1.0 / l_ref[sub, head]
            acc = acc_ref[sub, hs, :].reshape(head_groups, SUBLANES, tq) * inv[None]
            acc_ref[sub, hs, :] = acc.reshape(HEAD_DIM, tq)
        outs.append(acc_ref[sub].T.astype(jnp.bfloat16))
    o = jnp.concatenate(outs, axis=0)
    o_ref[0] = x + jnp.dot(o, wo_ref[...], preferred_element_type=jnp.float32)


def _attn_call(x, g, wqt, wo, layer, k, vt, km, bias_tiles, rel_bias):
    B, T, D = x.shape
    nb = T // MOBA_BLOCK
    tq = MOBA_BLOCK
    n_q = Q_SUBTILES * tq
    return pl.pallas_call(
        _attn_kernel,
        grid=(B, T // n_q),
        in_specs=[
            pl.BlockSpec(memory_space=pltpu.SMEM),
            pl.BlockSpec((1, n_q, D), lambda b, i: (b, i, 0)),
            _resident((1, D)),
            _resident_layer(wqt.shape, layer),
            _resident_layer(wo.shape, layer),
            pl.BlockSpec((1, T, D), lambda b, i: (b, 0, 0)),
            pl.BlockSpec((1, D, T), lambda b, i: (b, 0, 0)),
            pl.BlockSpec((1, nb, D), lambda b, i: (b, 0, 0)),
            _resident(bias_tiles.shape),
        ],
        out_specs=pl.BlockSpec((1, n_q, D), lambda b, i: (b, i, 0)),
        out_shape=jax.ShapeDtypeStruct(x.shape, x.dtype),
        scratch_shapes=[
            pltpu.VMEM((Q_SUBTILES, N_HEADS, LANES, tq), jnp.bfloat16),
            pltpu.VMEM((nb - 1, N_HEADS, n_q), jnp.float32),
            pltpu.VMEM((Q_SUBTILES, N_HEADS, SUBLANES, tq), jnp.float32),
            pltpu.VMEM((Q_SUBTILES, N_HEADS, SUBLANES, tq), jnp.float32),
            pltpu.VMEM((Q_SUBTILES, D, tq), jnp.float32),
        ],
        compiler_params=pltpu.CompilerParams(
            dimension_semantics=("arbitrary", "arbitrary"),
            vmem_limit_bytes=VMEM_LIMIT),
        name="moba_attention",
    )(rel_bias, x, g.reshape(1, D), wqt, wo, k, vt, km, bias_tiles)


def kernel(x, norm_mixer, norm_ffn, pool_w, pool_scale, kv_norm, w_kv, w_q, w_o,
           rel_bias, w_gate_up, w_down, final_norm):
    bf = jnp.bfloat16
    assert x.shape[1] % MOBA_BLOCK == 0 and x.shape[2] == D_MODEL
    pool_w = pool_w.astype(bf)
    w_qt = jnp.swapaxes(w_q, 1, 2).astype(bf)
    w_o = w_o.astype(bf)
    wk = w_kv[:, :D_MODEL].astype(bf)
    wvt = w_kv[:, D_MODEL:].T.astype(bf)

    ffn_stacks = (w_gate_up, w_down)

    k = vt = km = bias_tiles = wgu = wd = None
    for layer in range(DEPTH):
        if layer < N_A_LAYERS:
            x, *cast = _pool_call(x, norm_mixer[layer], pool_w, layer, pool_scale[layer],
                                  cast_stacks=ffn_stacks if layer == 0 else ())
            if cast:
                wgu, wd = cast
        else:
            j = layer - N_A_LAYERS
            if j == 0:
                k, vt, km = _kv_call(x, kv_norm, wk, wvt)
                bias_tiles = _bias_call(rel_bias)
            x = _attn_call(x, norm_mixer[layer], w_qt, w_o, j, k, vt, km, bias_tiles, rel_bias)
        last = layer == DEPTH - 1
        x, *cast = _ffn_call(x, norm_ffn[layer], wgu, wd, final_norm, final_norm=last,
                             cast_stacks=() if last else ffn_stacks, cast_layer=layer + 1)
        if cast:
            wgu, wd = cast
    return x
```

```python
import functools
import math

import jax
import jax.numpy as jnp
from jax import lax
from jax.experimental import pallas as pl
from jax.experimental.pallas import tpu as pltpu

D_MODEL = 1024
DEPTH = 4
N_A_LAYERS = DEPTH // 2
POOL_WINDOWS = (2, 4, 8, 16)
POOL_GROUP = D_MODEL // len(POOL_WINDOWS)
HEAD_DIM = 64
N_HEADS = D_MODEL // HEAD_DIM
MOBA_BLOCK = 256
MOBA_TOPK = 3
NUM_BUCKETS = 32
MAX_DISTANCE = 128
D_FF = -(-8 * D_MODEL // (3 * 256)) * 256
EPS = 1e-6

LANES = 128
SUBLANES = 8
PACKED_ROWS = 16
POOL_HALO = 16
assert all(w & (w - 1) == 0 and w <= POOL_HALO for w in POOL_WINDOWS)
HEADS_PER_LANE_TILE = LANES // HEAD_DIM
VMEM_LIMIT = 56 * 1024 * 1024
SCORE_LOOKAHEAD = 4
FAR_UNROLL = 2
Q_SUBTILES = 2

_NT = (((1,), (1,)), ((), ()))
_NEG_INF = float("-inf")
_MASKED = -1e30
LOG2E = math.log2(math.e)
ONES_ROWS = 16


def _rms(x, g):
    return x * lax.rsqrt(jnp.mean(x * x, axis=-1, keepdims=True) + EPS) * g


def _resident(shape):
    zeros = (0,) * len(shape)
    return pl.BlockSpec(shape, lambda *_: zeros, pipeline_mode=pl.Buffered(1))


def _resident_layer(stacked_shape, layer):
    index = (layer,) + (0,) * (len(stacked_shape) - 1)
    return pl.BlockSpec((None,) + tuple(stacked_shape[1:]), lambda *_: index,
                        pipeline_mode=pl.Buffered(1))


def _cast_rider(stacks, layer, n_steps, step_of):
    in_specs, out_specs, out_shape = [], [], []
    for w in stacks:
        n_slabs = n_steps if (w.shape[1] // n_steps) % PACKED_ROWS == 0 else n_steps // 2
        rows = w.shape[1] // n_slabs
        assert rows * n_slabs == w.shape[1] and rows % PACKED_ROWS == 0 and n_steps % n_slabs == 0
        in_specs.append(pl.BlockSpec(
            (None, rows, w.shape[2]),
            lambda *idx, n_slabs=n_slabs: (layer, step_of(*idx) * n_slabs // n_steps, 0)))
        out_specs.append(pl.BlockSpec(
            (rows, w.shape[2]),
            lambda *idx, n_slabs=n_slabs: (step_of(*idx) * n_slabs // n_steps, 0)))
        out_shape.append(jax.ShapeDtypeStruct(w.shape[1:], jnp.bfloat16))
    return in_specs, out_specs, out_shape


def _cast_slabs(in_refs, out_refs):
    for src, dst in zip(in_refs, out_refs):
        dst[...] = src[...].astype(jnp.bfloat16)


def _pool_kernel(x_ref, halo_ref, g_ref, w_ref, scale_ref, *rest, tm, n_cast):
    cast_in, (o_ref, *cast_out, hh_ref) = rest[:n_cast], rest[n_cast:]
    _cast_slabs(cast_in, cast_out)
    i = pl.program_id(1)
    g = g_ref[...]
    x = x_ref[0]
    h = _rms(x, g)
    halo = _rms(halo_ref[0], g)
    hh_ref[0:POOL_HALO, :] = jnp.where(i > 0, halo, 0.0)
    hh_ref[POOL_HALO:, :] = h
    t = i * tm + lax.broadcasted_iota(jnp.int32, (tm, 1), 0)
    ys = []
    for gi, win in enumerate(POOL_WINDOWS):
        sl = slice(gi * POOL_GROUP, (gi + 1) * POOL_GROUP)
        wsum = hh_ref[:, sl]
        span = 1
        while span < win:
            wsum = wsum + pltpu.roll(wsum, span, axis=0)
            span *= 2
        wsum = wsum[POOL_HALO:]
        cnt = jnp.minimum(t + 1, win).astype(jnp.float32)
        d = wsum / cnt - h[:, sl]
        ys.append(jnp.dot(d.astype(jnp.bfloat16), w_ref[gi],
                          preferred_element_type=jnp.float32))
    y = jnp.concatenate(ys, axis=-1)
    o_ref[0] = x + y * scale_ref[...]


def _pool_call(x, g, w, layer, scale, cast_stacks=(), cast_layer=0, tm=512):
    B, T, D = x.shape
    halo_blocks = tm // POOL_HALO
    grid = (B, T // tm)
    cast_in, cast_out, cast_shape = _cast_rider(
        cast_stacks, cast_layer, grid[0] * grid[1], lambda b, i: b * grid[1] + i)
    return pl.pallas_call(
        functools.partial(_pool_kernel, tm=tm, n_cast=len(cast_stacks)),
        grid=grid,
        in_specs=[
            pl.BlockSpec((1, tm, D), lambda b, i: (b, i, 0)),
            pl.BlockSpec((1, POOL_HALO, D),
                         lambda b, i: (b, jnp.maximum(i * halo_blocks - 1, 0), 0)),
            _resident((1, D)),
            _resident_layer(w.shape, layer),
            _resident((1, D)),
        ] + cast_in,
        out_specs=[pl.BlockSpec((1, tm, D), lambda b, i: (b, i, 0))] + cast_out,
        out_shape=[jax.ShapeDtypeStruct(x.shape, x.dtype)] + cast_shape,
        scratch_shapes=[pltpu.VMEM((tm + POOL_HALO, D), jnp.float32)],
        compiler_params=pltpu.CompilerParams(
            dimension_semantics=("arbitrary", "arbitrary"),
            vmem_limit_bytes=VMEM_LIMIT),
        name="pool_mixer",
    )(x, x, g.reshape(1, D), w, scale.reshape(1, D), *cast_stacks)


def _ffn_kernel(x_ref, g_ref, wgu_ref, wd_ref, fg_ref, *rest, fc, final_norm, n_cast):
    cast_in, (o_ref, *cast_out, act_ref) = rest[:n_cast], rest[n_cast:]
    _cast_slabs(cast_in, cast_out)
    x = x_ref[...]
    h = _rms(x, g_ref[...]).astype(jnp.bfloat16)
    for c in range(D_FF // fc):
        gate = jnp.dot(h, wgu_ref[:, c * fc:(c + 1) * fc],
                       preferred_element_type=jnp.float32)
        up = jnp.dot(h, wgu_ref[:, D_FF + c * fc:D_FF + (c + 1) * fc],
                     preferred_element_type=jnp.float32)
        act = gate * (1.0 / (1.0 + jnp.exp(-gate))) * up
        act_ref[:, c * fc:(c + 1) * fc] = act.astype(jnp.bfloat16)
    y = x + jnp.dot(act_ref[...], wd_ref[...], preferred_element_type=jnp.float32)
    if final_norm:
        y = _rms(y, fg_ref[...])
    o_ref[...] = y


def _ffn_call(x, g, wgu, wd, final_g, final_norm, cast_stacks=(), cast_layer=0,
              tm=1024, fc=256):
    B, T, D = x.shape
    xf = x.reshape(B * T, D)
    n_steps = B * T // tm
    cast_in, cast_out, cast_shape = _cast_rider(cast_stacks, cast_layer, n_steps, lambda i: i)
    out, *cast = pl.pallas_call(
        functools.partial(_ffn_kernel, fc=fc, final_norm=final_norm, n_cast=len(cast_stacks)),
        grid=(n_steps,),
        in_specs=[
            pl.BlockSpec((tm, D), lambda i: (i, 0)),
            _resident((1, D)),
            _resident(wgu.shape),
            _resident(wd.shape),
            _resident((1, D)),
        ] + cast_in,
        out_specs=[pl.BlockSpec((tm, D), lambda i: (i, 0))] + cast_out,
        out_shape=[jax.ShapeDtypeStruct(xf.shape, xf.dtype)] + cast_shape,
        scratch_shapes=[pltpu.VMEM((tm, D_FF), jnp.bfloat16)],
        compiler_params=pltpu.CompilerParams(
            dimension_semantics=("arbitrary",),
            vmem_limit_bytes=VMEM_LIMIT),
        name="swiglu_ffn",
    )(xf, g.reshape(1, D), wgu, wd, final_g.reshape(1, D), *cast_stacks)
    return [out.reshape(B, T, D)] + cast


def _pool_ffn_kernel(x_ref, halo_ref, gm_ref, pw_ref, ps_ref, gf_ref, wgu_ref, wd_ref, *rest,
                     tm, fc, n_cast, tiles_per_seq):
    cast_in, (o_ref, *cast_out, act_ref, x1_ref) = rest[:n_cast], rest[n_cast:]
    _cast_slabs(cast_in, cast_out)
    seq_tile = pl.program_id(0) % tiles_per_seq
    half = tm // 2
    gm = gm_ref[...]
    gf = gf_ref[...]

    def pool_stages(r0, halo_h, state):
        rows = slice(r0, r0 + half)

        def norm():
            x = x_ref[rows, :]
            state["x"] = x
            state["h"] = _rms(x, gm)
            state["hh"] = jnp.concatenate([halo_h(), state["h"]], axis=0)

        def group(gi, win):
            sl = slice(gi * POOL_GROUP, (gi + 1) * POOL_GROUP)
            wsum = state["hh"][:, sl]
            span = 1
            while span < win:
                wsum = wsum + pltpu.roll(wsum, span, axis=0)
                span *= 2
            wsum = wsum[POOL_HALO:]
            t = seq_tile * tm + r0 + lax.broadcasted_iota(jnp.int32, (half, 1), 0)
            cnt = jnp.minimum(t + 1, win).astype(jnp.float32)
            d = wsum / cnt - state["h"][:, sl]
            y = jnp.dot(d.astype(jnp.bfloat16), pw_ref[gi], preferred_element_type=jnp.float32)
            x1_ref[rows, sl] = state["x"][:, sl] + y * ps_ref[:, sl]

        return [norm] + [functools.partial(group, gi, win)
                         for gi, win in enumerate(POOL_WINDOWS)]

    def ffn_half(r0, between):
        rows = slice(r0, r0 + half)
        x1 = x1_ref[rows, :]
        h = _rms(x1, gf).astype(jnp.bfloat16)
        for c in range(D_FF // fc):
            gate = jnp.dot(h, wgu_ref[:, c * fc:(c + 1) * fc],
                           preferred_element_type=jnp.float32)
            up = jnp.dot(h, wgu_ref[:, D_FF + c * fc:D_FF + (c + 1) * fc],
                         preferred_element_type=jnp.float32)
            act = gate * (1.0 / (1.0 + jnp.exp(-gate))) * up
            act_ref[rows, c * fc:(c + 1) * fc] = act.astype(jnp.bfloat16)
            if c < len(between):
                between[c]()
        o_ref[rows, :] = x1 + jnp.dot(act_ref[rows, :], wd_ref[...],
                                      preferred_element_type=jnp.float32)

    first, second = {}, {}
    for stage in pool_stages(
            0, lambda: jnp.where(seq_tile > 0, _rms(halo_ref[...], gm), 0.0), first):
        stage()
    ffn_half(0, pool_stages(half, lambda: first["h"][half - POOL_HALO:], second))
    ffn_half(half, [])


def _pool_ffn_call(x, g_mixer, pool_w, layer, pool_scale, g_ffn, wgu, wd,
                   cast_stacks=(), cast_layer=0, tm=512, fc=256):
    B, T, D = x.shape
    xf = x.reshape(B * T, D)
    n_steps = B * T // tm
    halo_blocks = tm // POOL_HALO
    cast_in, cast_out, cast_shape = _cast_rider(cast_stacks, cast_layer, n_steps, lambda i: i)
    out, *cast = pl.pallas_call(
        functools.partial(_pool_ffn_kernel, tm=tm, fc=fc, n_cast=len(cast_stacks),
                          tiles_per_seq=T // tm),
        grid=(n_steps,),
        in_specs=[
            pl.BlockSpec((tm, D), lambda i: (i, 0)),
            pl.BlockSpec((POOL_HALO, D), lambda i: (jnp.maximum(i * halo_blocks - 1, 0), 0)),
            _resident((1, D)),
            _resident_layer(pool_w.shape, layer),
            _resident((1, D)),
            _resident((1, D)),
            _resident(wgu.shape),
            _resident(wd.shape),
        ] + cast_in,
        out_specs=[pl.BlockSpec((tm, D), lambda i: (i, 0))] + cast_out,
        out_shape=[jax.ShapeDtypeStruct(xf.shape, xf.dtype)] + cast_shape,
        scratch_shapes=[
            pltpu.VMEM((tm, D_FF), jnp.bfloat16),
            pltpu.VMEM((tm, D), jnp.float32),
        ],
        compiler_params=pltpu.CompilerParams(
            dimension_semantics=("arbitrary",),
            vmem_limit_bytes=VMEM_LIMIT),
        name="pool_layer",
    )(xf, xf, g_mixer.reshape(1, D), pool_w, pool_scale.reshape(1, D), g_ffn.reshape(1, D),
      wgu, wd, *cast_stacks)
    return [out.reshape(B, T, D)] + cast


def _kv_kernel(x_ref, g_ref, wk_ref, wvt_ref, k_ref, vt_ref, km_ref, *, blocks_per_tile):
    j = pl.program_id(1)
    h = _rms(x_ref[0], g_ref[...]).astype(jnp.bfloat16)
    k = jnp.dot(h, wk_ref[...], preferred_element_type=jnp.float32)
    k_ref[0] = k.astype(jnp.bfloat16)
    vt = lax.dot_general(wvt_ref[...], h, _NT, preferred_element_type=jnp.float32)
    vt_ref[0] = vt.astype(jnp.bfloat16)
    for i in range(blocks_per_tile):
        km_ref[0, pl.ds(j * blocks_per_tile + i, 1), :] = jnp.mean(
            k[i * MOBA_BLOCK:(i + 1) * MOBA_BLOCK], axis=0, keepdims=True)


def _kv_call(x, g, wk, wvt, tm=1024):
    B, T, D = x.shape
    nb = T // MOBA_BLOCK
    return pl.pallas_call(
        functools.partial(_kv_kernel, blocks_per_tile=tm // MOBA_BLOCK),
        grid=(B, T // tm),
        in_specs=[
            pl.BlockSpec((1, tm, D), lambda b, j: (b, j, 0)),
            _resident((1, D)),
            _resident(wk.shape),
            _resident(wvt.shape),
        ],
        out_specs=[
            pl.BlockSpec((1, tm, D), lambda b, j: (b, j, 0)),
            pl.BlockSpec((1, D, tm), lambda b, j: (b, 0, j)),
            pl.BlockSpec((1, nb, D), lambda b, j: (b, 0, 0)),
        ],
        out_shape=[
            jax.ShapeDtypeStruct((B, T, D), jnp.bfloat16),
            jax.ShapeDtypeStruct((B, D, T), jnp.bfloat16),
            jax.ShapeDtypeStruct((B, nb, D), jnp.float32),
        ],
        compiler_params=pltpu.CompilerParams(
            dimension_semantics=("arbitrary", "arbitrary"),
            vmem_limit_bytes=VMEM_LIMIT),
        name="shared_kv",
    )(x, g.reshape(1, D), wk, wvt)


def _rel_bucket(n):
    max_exact = NUM_BUCKETS // 2
    nf = jnp.maximum(n, max_exact).astype(jnp.float32)
    large = max_exact + (jnp.log(nf / max_exact) / math.log(MAX_DISTANCE / max_exact)
                         * (NUM_BUCKETS - max_exact)).astype(jnp.int32)
    large = jnp.minimum(large, NUM_BUCKETS - 1)
    return jnp.where(n < max_exact, n, large)


def _bias_kernel(rb_ref, o_ref):
    hd = pl.program_id(0)
    kl = lax.broadcasted_iota(jnp.int32, (MOBA_BLOCK, MOBA_BLOCK), 0)
    ql = lax.broadcasted_iota(jnp.int32, (MOBA_BLOCK, MOBA_BLOCK), 1)
    d_own = ql - kl
    for ti, dist in enumerate((d_own, d_own + MOBA_BLOCK)):
        bucket = _rel_bucket(jnp.maximum(dist, 0))
        tile = jnp.zeros((MOBA_BLOCK, MOBA_BLOCK), jnp.float32)
        for bi in range(NUM_BUCKETS):
            tile = jnp.where(bucket == bi, rb_ref[bi, hd], tile)
        if ti == 0:
            tile = jnp.where(dist >= 0, tile, _MASKED)
        o_ref[0, ti] = tile * LOG2E


def _bias_call(rel_bias):
    return pl.pallas_call(
        _bias_kernel,
        grid=(N_HEADS,),
        in_specs=[pl.BlockSpec(memory_space=pltpu.SMEM)],
        out_specs=pl.BlockSpec((1, 2, MOBA_BLOCK, MOBA_BLOCK), lambda h: (h, 0, 0, 0)),
        out_shape=jax.ShapeDtypeStruct((N_HEADS, 2, MOBA_BLOCK, MOBA_BLOCK), jnp.float32),
        compiler_params=pltpu.CompilerParams(dimension_semantics=("arbitrary",)),
        name="rel_bias_tiles",
    )(rel_bias)


def _sublane_all(op, x):
    shift = SUBLANES // 2
    while shift:
        x = op(x, pltpu.roll(x, shift, axis=0))
        shift //= 2
    return x


def _attn_kernel(rb_ref, x_ref, g_ref, wqt_ref, wo_ref, k_ref, vt_ref, km_ref, bias_ref,
                 o_ref, qh_ref, mask_ref, m_ref, l_ref, acc_ref):
    first_own = pl.program_id(1) * Q_SUBTILES
    nb = km_ref.shape[1]
    tq = MOBA_BLOCK
    D = x_ref.shape[2]
    row_groups = MOBA_BLOCK // SUBLANES
    head_groups = HEAD_DIM // SUBLANES
    x = x_ref[0]
    h = _rms(x, g_ref[...]).astype(jnp.bfloat16)
    qf = lax.dot_general(wqt_ref[...], h, _NT, preferred_element_type=jnp.float32)
    q_gate = (qf * HEAD_DIM ** -0.5).astype(jnp.bfloat16)
    q = (qf * (HEAD_DIM ** -0.5 * LOG2E)).astype(jnp.bfloat16)

    feat = lax.broadcasted_iota(jnp.int32, (LANES, 1), 0)
    for sub in range(Q_SUBTILES):
        for head in range(N_HEADS):
            p, hh = divmod(head, HEADS_PER_LANE_TILE)
            qp = q[p * LANES:(p + 1) * LANES, sub * tq:(sub + 1) * tq]
            in_head = (feat >= hh * HEAD_DIM) & (feat < (hh + 1) * HEAD_DIM)
            qh_ref[sub, head] = jnp.where(in_head, qp, jnp.zeros_like(qp))

    n_q = Q_SUBTILES * tq
    n_cand = nb - 1
    own_of_query = first_own + lax.broadcasted_iota(jnp.int32, (N_HEADS, n_q), 1) // tq
    past = [own_of_query > j for j in range(n_cand)]
    some_query_ranks = first_own + Q_SUBTILES - 1 > MOBA_TOPK

    @pl.when(some_query_ranks)
    def _():
        km = jnp.broadcast_to(km_ref[0, :n_cand][:, None, :], (n_cand, N_HEADS, D))
        km = km.reshape(n_cand * N_HEADS, D)
        row_head = lax.broadcasted_iota(jnp.int32, km.shape, 0) % N_HEADS
        col_head = lax.broadcasted_iota(jnp.int32, km.shape, 1) // HEAD_DIM
        km_bd = jnp.where(row_head == col_head, km, 0.0).astype(jnp.bfloat16)
        gate = jnp.dot(km_bd, q_gate, preferred_element_type=jnp.float32)
        gates = [jnp.where(past[j], gate[j * N_HEADS:(j + 1) * N_HEADS], _NEG_INF)
                 for j in range(n_cand)]
        rank = [jnp.zeros((N_HEADS, n_q), jnp.int32) for _ in range(n_cand)]
        for lo in range(n_cand):
            for hi in range(lo + 1, n_cand):
                lo_wins = jnp.where(gates[lo] >= gates[hi], 1, 0)
                rank[hi] = rank[hi] + lo_wins
                rank[lo] = rank[lo] + (1 - lo_wins)
        for j in range(n_cand):
            keep = jnp.where(past[j], 0.0, _MASKED)
            mask_ref[j] = jnp.where(rank[j] < MOBA_TOPK, keep, _MASKED)

    @pl.when(jnp.logical_not(some_query_ranks))
    def _():
        for j in range(n_cand):
            mask_ref[j] = jnp.where(past[j], 0.0, _MASKED)

    m_ref[...] = jnp.full(m_ref.shape, _MASKED, jnp.float32)
    l_ref[...] = jnp.zeros(l_ref.shape, jnp.float32)
    acc_ref[...] = jnp.zeros(acc_ref.shape, jnp.float32)

    def attend(blocks):
        units = []
        for j, kinds in blocks:
            keys = pl.ds(pl.multiple_of(j * MOBA_BLOCK, MOBA_BLOCK), MOBA_BLOCK)
            units += [(j, keys, kind, sub, head) for head in range(N_HEADS)
                      for sub, kind in enumerate(kinds) if kind is not None]

        def scores(unit):
            _, keys, _, sub, head = unit
            p = head // HEADS_PER_LANE_TILE
            kj = k_ref[0, keys, p * LANES:(p + 1) * LANES]
            return jnp.dot(kj, qh_ref[sub, head], preferred_element_type=jnp.float32)

        def softmax(unit, s):
            j, _, kind, sub, head = unit
            if kind != "far":
                tile = bias_ref[head, 0 if kind == "own" else 1]
                s = (s.reshape(row_groups, SUBLANES, tq)
                     + tile.reshape(row_groups, SUBLANES, tq)).reshape(MOBA_BLOCK, tq)
            sb = s.astype(jnp.bfloat16).reshape(MOBA_BLOCK // PACKED_ROWS, PACKED_ROWS, tq)
            bm = jnp.max(sb, axis=0).astype(jnp.float32)
            block_max = _sublane_all(jnp.maximum, jnp.maximum(bm[:SUBLANES], bm[SUBLANES:]))
            m_old = m_ref[sub, head]
            bias = rb_ref[NUM_BUCKETS - 1, head] * LOG2E if kind == "far" else 0.0
            top = block_max + bias
            if kind != "own":
                selected = jnp.broadcast_to(
                    mask_ref[j, head:head + 1, sub * tq:(sub + 1) * tq], (SUBLANES, tq)) == 0.0
                top = jnp.where(selected, top, _MASKED)
            shift = jnp.maximum(jnp.maximum(m_old, top) - bias, block_max)
            shift = shift.astype(jnp.bfloat16).astype(jnp.float32)
            shift_b = jnp.concatenate([shift, shift], axis=0).astype(jnp.bfloat16)
            pj = jnp.exp2(sb - shift_b[None])
            ref = shift + bias
            if kind != "own":
                m_new = jnp.where(selected, jnp.maximum(m_old, ref), m_old)
                scale_new = jnp.where(selected, jnp.exp2(ref - m_new), 0.0)
            else:
                m_new = jnp.maximum(m_old, ref)
                scale_new = jnp.exp2(ref - m_new)
            scale_old = jnp.exp2(m_old - m_new)
            m_ref[sub, head] = m_new
            return pj.reshape(MOBA_BLOCK, tq), (scale_old, scale_new)

        def accumulate(unit, pv, scales):
            _, _, _, sub, head = unit
            scale_old, scale_new = scales
            hs = slice(head * HEAD_DIM, (head + 1) * HEAD_DIM)
            l_ref[sub, head] = (l_ref[sub, head] * scale_old
                                + pv[HEAD_DIM:HEAD_DIM + SUBLANES] * scale_new)
            acc = acc_ref[sub, hs, :].reshape(head_groups, SUBLANES, tq)
            acc = (acc * scale_old[None]
                   + pv[:HEAD_DIM].reshape(head_groups, SUBLANES, tq) * scale_new[None])
            acc_ref[sub, hs, :] = acc.reshape(HEAD_DIM, tq)

        ones = jnp.ones((ONES_ROWS, MOBA_BLOCK), jnp.bfloat16)

        def weighted_values(unit, pj):
            _, keys, _, _, head = unit
            vj = vt_ref[0, head * HEAD_DIM:(head + 1) * HEAD_DIM, keys]
            return jnp.dot(jnp.concatenate([vj, ones], axis=0), pj,
                           preferred_element_type=jnp.float32)

        n = len(units)
        pending_scores = {i: scores(units[i]) for i in range(SCORE_LOOKAHEAD)}
        probs = {}
        products = {}
        for step in range(n + 2):
            if step + SCORE_LOOKAHEAD < n:
                pending_scores[step + SCORE_LOOKAHEAD] = scores(units[step + SCORE_LOOKAHEAD])
            if 0 <= step - 1 < n:
                pj, alpha = probs.pop(step - 1)
                products[step - 1] = (weighted_values(units[step - 1], pj), alpha)
            if step < n:
                probs[step] = softmax(units[step], pending_scores.pop(step))
            if 0 <= step - 2 < n:
                accumulate(units[step - 2], *products.pop(step - 2))

    assert Q_SUBTILES == 2 and FAR_UNROLL == 2
    all_far = ("far", "far")
    tail = [(first_own, ("own", "prev")), (first_own + 1, (None, "own"))]

    def far_pair(t, carry):
        attend([(FAR_UNROLL * t, all_far), (FAR_UNROLL * t + 1, all_far)])
        return carry

    lax.fori_loop(0, jnp.maximum(first_own - 1, 0) // FAR_UNROLL, far_pair, 0)

    @pl.when(first_own > 0)
    def _():
        attend([(first_own - 2, all_far), (first_own - 1, ("prev", "far"))] + tail)

    @pl.when(first_own == 0)
    def _():
        attend(tail)

    outs = []
    for sub in range(Q_SUBTILES):
        for head in range(N_HEADS):
            hs = slice(head * HEAD_DIM, (head + 1) * HEAD_DIM)
            inv = 1.0 / l_ref[sub, head]
            acc = acc_ref[sub, hs, :].reshape(head_groups, SUBLANES, tq) * inv[None]
            acc_ref[sub, hs, :] = acc.reshape(HEAD_DIM, tq)
        outs.append(acc_ref[sub].T.astype(jnp.bfloat16))
    o = jnp.concatenate(outs, axis=0)
    o_ref[0] = x + jnp.dot(o, wo_ref[...], preferred_element_type=jnp.float32)


def _attn_call(x, g, wqt, wo, layer, k, vt, km, bias_tiles, rel_bias):
    B, T, D = x.shape
    nb = T // MOBA_BLOCK
    tq = MOBA_BLOCK
    n_q = Q_SUBTILES * tq
    return pl.pallas_call(
        _attn_kernel,
        grid=(B, T // n_q),
        in_specs=[
            pl.BlockSpec(memory_space=pltpu.SMEM),
            pl.BlockSpec((1, n_q, D), lambda b, i: (b, i, 0)),
            _resident((1, D)),
            _resident_layer(wqt.shape, layer),
            _resident_layer(wo.shape, layer),
            pl.BlockSpec((1, T, D), lambda b, i: (b, 0, 0)),
            pl.BlockSpec((1, D, T), lambda b, i: (b, 0, 0)),
            pl.BlockSpec((1, nb, D), lambda b, i: (b, 0, 0)),
            _resident(bias_tiles.shape),
        ],
        out_specs=pl.BlockSpec((1, n_q, D), lambda b, i: (b, i, 0)),
        out_shape=jax.ShapeDtypeStruct(x.shape, x.dtype),
        scratch_shapes=[
            pltpu.VMEM((Q_SUBTILES, N_HEADS, LANES, tq), jnp.bfloat16),
            pltpu.VMEM((nb - 1, N_HEADS, n_q), jnp.float32),
            pltpu.VMEM((Q_SUBTILES, N_HEADS, SUBLANES, tq), jnp.float32),
            pltpu.VMEM((Q_SUBTILES, N_HEADS, SUBLANES, tq), jnp.float32),
            pltpu.VMEM((Q_SUBTILES, D, tq), jnp.float32),
        ],
        compiler_params=pltpu.CompilerParams(
            dimension_semantics=("arbitrary", "arbitrary"),
            vmem_limit_bytes=VMEM_LIMIT),
        name="moba_attention",
    )(rel_bias, x, g.reshape(1, D), wqt, wo, k, vt, km, bias_tiles)


def kernel(x, norm_mixer, norm_ffn, pool_w, pool_scale, kv_norm, w_kv, w_q, w_o,
           rel_bias, w_gate_up, w_down, final_norm):
    bf = jnp.bfloat16
    assert x.shape[1] % MOBA_BLOCK == 0 and x.shape[2] == D_MODEL
    pool_w = pool_w.astype(bf)
    w_qt = jnp.swapaxes(w_q, 1, 2).astype(bf)
    w_o = w_o.astype(bf)
    wk = w_kv[:, :D_MODEL].astype(bf)
    wvt = w_kv[:, D_MODEL:].T.astype(bf)

    ffn_stacks = (w_gate_up, w_down)

    wgu, wd = w_gate_up[0].astype(bf), w_down[0].astype(bf)

    k = vt = km = bias_tiles = None
    for layer in range(DEPTH):
        last = layer == DEPTH - 1
        rider = dict(cast_stacks=() if last else ffn_stacks, cast_layer=layer + 1)
        if layer < N_A_LAYERS:
            x, *cast = _pool_ffn_call(x, norm_mixer[layer], pool_w, layer, pool_scale[layer],
                                      norm_ffn[layer], wgu, wd, **rider)
        else:
            j = layer - N_A_LAYERS
            if j == 0:
                k, vt, km = _kv_call(x, kv_norm, wk, wvt)
                bias_tiles = _bias_call(rel_bias)
            x = _attn_call(x, norm_mixer[layer], w_qt, w_o, j, k, vt, km, bias_tiles, rel_bias)
            x, *cast = _ffn_call(x, norm_ffn[layer], wgu, wd, final_norm, final_norm=last,
                                 **rider)
        if cast:
            wgu, wd = cast
    return x
```

```python
import functools
import math

import jax
import jax.numpy as jnp
from jax import lax
from jax.experimental import pallas as pl
from jax.experimental.pallas import tpu as pltpu

D_MODEL = 1024
DEPTH = 4
N_A_LAYERS = DEPTH // 2
POOL_WINDOWS = (2, 4, 8, 16)
POOL_GROUP = D_MODEL // len(POOL_WINDOWS)
HEAD_DIM = 64
N_HEADS = D_MODEL // HEAD_DIM
MOBA_BLOCK = 256
MOBA_TOPK = 3
NUM_BUCKETS = 32
MAX_DISTANCE = 128
D_FF = -(-8 * D_MODEL // (3 * 256)) * 256
EPS = 1e-6

LANES = 128
SUBLANES = 8
PACKED_ROWS = 16
POOL_HALO = 16
assert all(w & (w - 1) == 0 and w <= POOL_HALO for w in POOL_WINDOWS)
HEADS_PER_LANE_TILE = LANES // HEAD_DIM
VMEM_LIMIT = 56 * 1024 * 1024
SCORE_LOOKAHEAD = 4
FAR_UNROLL = 2
Q_SUBTILES = 2

_NT = (((1,), (1,)), ((), ()))
_NEG_INF = float("-inf")
_MASKED = -1e30
LOG2E = math.log2(math.e)
ONES_ROWS = 16


def _rms(x, g):
    return x * lax.rsqrt(jnp.mean(x * x, axis=-1, keepdims=True) + EPS) * g


def _resident(shape):
    zeros = (0,) * len(shape)
    return pl.BlockSpec(shape, lambda *_: zeros, pipeline_mode=pl.Buffered(1))


def _resident_layer(stacked_shape, layer):
    index = (layer,) + (0,) * (len(stacked_shape) - 1)
    return pl.BlockSpec((None,) + tuple(stacked_shape[1:]), lambda *_: index,
                        pipeline_mode=pl.Buffered(1))


def _cast_rider(stacks, layer, n_steps, step_of):
    in_specs, out_specs, out_shape = [], [], []
    for w in stacks:
        n_slabs = n_steps if (w.shape[1] // n_steps) % PACKED_ROWS == 0 else n_steps // 2
        rows = w.shape[1] // n_slabs
        assert rows * n_slabs == w.shape[1] and rows % PACKED_ROWS == 0 and n_steps % n_slabs == 0
        in_specs.append(pl.BlockSpec(
            (None, rows, w.shape[2]),
            lambda *idx, n_slabs=n_slabs: (layer, step_of(*idx) * n_slabs // n_steps, 0)))
        out_specs.append(pl.BlockSpec(
            (rows, w.shape[2]),
            lambda *idx, n_slabs=n_slabs: (step_of(*idx) * n_slabs // n_steps, 0)))
        out_shape.append(jax.ShapeDtypeStruct(w.shape[1:], jnp.bfloat16))
    return in_specs, out_specs, out_shape


def _cast_slabs(in_refs, out_refs):
    for src, dst in zip(in_refs, out_refs):
        dst[...] = src[...].astype(jnp.bfloat16)


def _ffn_kernel(x_ref, g_ref, wgu_ref, wd_ref, fg_ref, *rest, fc, final_norm, n_cast):
    cast_in, (o_ref, *cast_out, act_ref) = rest[:n_cast], rest[n_cast:]
    _cast_slabs(cast_in, cast_out)
    x = x_ref[...]
    h = _rms(x, g_ref[...]).astype(jnp.bfloat16)
    for c in range(D_FF // fc):
        gate = jnp.dot(h, wgu_ref[:, c * fc:(c + 1) * fc],
                       preferred_element_type=jnp.float32)
        up = jnp.dot(h, wgu_ref[:, D_FF + c * fc:D_FF + (c + 1) * fc],
                     preferred_element_type=jnp.float32)
        act = gate * (1.0 / (1.0 + jnp.exp(-gate))) * up
        act_ref[:, c * fc:(c + 1) * fc] = act.astype(jnp.bfloat16)
    y = x + jnp.dot(act_ref[...], wd_ref[...], preferred_element_type=jnp.float32)
    if final_norm:
        y = _rms(y, fg_ref[...])
    o_ref[...] = y


def _ffn_call(x, g, wgu, wd, final_g, final_norm, cast_stacks=(), cast_layer=0,
              tm=1024, fc=256):
    B, T, D = x.shape
    xf = x.reshape(B * T, D)
    n_steps = B * T // tm
    cast_in, cast_out, cast_shape = _cast_rider(cast_stacks, cast_layer, n_steps, lambda i: i)
    out, *cast = pl.pallas_call(
        functools.partial(_ffn_kernel, fc=fc, final_norm=final_norm, n_cast=len(cast_stacks)),
        grid=(n_steps,),
        in_specs=[
            pl.BlockSpec((tm, D), lambda i: (i, 0)),
            _resident((1, D)),
            _resident(wgu.shape),
            _resident(wd.shape),
            _resident((1, D)),
        ] + cast_in,
        out_specs=[pl.BlockSpec((tm, D), lambda i: (i, 0))] + cast_out,
        out_shape=[jax.ShapeDtypeStruct(xf.shape, xf.dtype)] + cast_shape,
        scratch_shapes=[pltpu.VMEM((tm, D_FF), jnp.bfloat16)],
        compiler_params=pltpu.CompilerParams(
            dimension_semantics=("arbitrary",),
            vmem_limit_bytes=VMEM_LIMIT),
        name="swiglu_ffn",
    )(xf, g.reshape(1, D), wgu, wd, final_g.reshape(1, D), *cast_stacks)
    return [out.reshape(B, T, D)] + cast


def _pool_ffn_kernel(x_ref, xn_ref, gm_ref, pw_ref, ps_ref, gf_ref, wgu_ref, wd_ref, *rest,
                     tm, fc, n_cast, tiles_per_seq):
    cast_in, (o_ref, *cast_out, act_ref, x1_ref, tail_ref) = rest[:n_cast], rest[n_cast:]
    _cast_slabs(cast_in, cast_out)
    step = pl.program_id(0)
    half = tm // 2
    gm = gm_ref[...]
    gf = gf_ref[...]

    def pool_stages(src_ref, r0, dst0, seq_pos, halo_h, state):
        rows = slice(r0, r0 + half)

        def norm():
            x = src_ref[rows, :]
            state["x"] = x
            state["h"] = _rms(x, gm)
            state["hh"] = jnp.concatenate([halo_h(), state["h"]], axis=0)

        def group(gi, win):
            sl = slice(gi * POOL_GROUP, (gi + 1) * POOL_GROUP)
            wsum = state["hh"][:, sl]
            span = 1
            while span < win:
                wsum = wsum + pltpu.roll(wsum, span, axis=0)
                span *= 2
            wsum = wsum[POOL_HALO:]
            t = seq_pos + lax.broadcasted_iota(jnp.int32, (half, 1), 0)
            cnt = jnp.minimum(t + 1, win).astype(jnp.float32)
            d = wsum / cnt - state["h"][:, sl]
            y = jnp.dot(d.astype(jnp.bfloat16), pw_ref[gi], preferred_element_type=jnp.float32)
            x1_ref[dst0:dst0 + half, sl] = state["x"][:, sl] + y * ps_ref[:, sl]

        return [norm] + [functools.partial(group, gi, win)
                         for gi, win in enumerate(POOL_WINDOWS)]

    def first_half_stages(src_ref, seq_tile, halo_h, state):
        def keep_tail():
            tail_ref[...] = state["h"][half - POOL_HALO:]
        return pool_stages(src_ref, 0, 0, seq_tile * tm, halo_h, state) + [keep_tail]

    def ffn_half(r0, between):
        rows = slice(r0, r0 + half)
        x1 = x1_ref[rows, :]
        h = _rms(x1, gf).astype(jnp.bfloat16)
        for c in range(D_FF // fc):
            gate = jnp.dot(h, wgu_ref[:, c * fc:(c + 1) * fc],
                           preferred_element_type=jnp.float32)
            up = jnp.dot(h, wgu_ref[:, D_FF + c * fc:D_FF + (c + 1) * fc],
                         preferred_element_type=jnp.float32)
            act = gate * (1.0 / (1.0 + jnp.exp(-gate))) * up
            act_ref[rows, c * fc:(c + 1) * fc] = act.astype(jnp.bfloat16)
            if c < len(between):
                between[c]()
        o_ref[rows, :] = x1 + jnp.dot(act_ref[rows, :], wd_ref[...],
                                      preferred_element_type=jnp.float32)

    seq_tile = step % tiles_per_seq
    next_seq_tile = (step + 1) % tiles_per_seq
    no_history = lambda: jnp.zeros((POOL_HALO, x_ref.shape[1]), jnp.float32)

    @pl.when(step == 0)
    def _():
        for stage in first_half_stages(x_ref, 0, no_history, {}):
            stage()

    second, upcoming = {}, {}
    ffn_half(0, pool_stages(x_ref, half, half, seq_tile * tm + half,
                            lambda: tail_ref[...], second))
    ffn_half(half, first_half_stages(
        xn_ref, next_seq_tile,
        lambda: jnp.where(next_seq_tile > 0, second["h"][half - POOL_HALO:], 0.0), upcoming))


def _pool_ffn_call(x, g_mixer, pool_w, layer, pool_scale, g_ffn, wgu, wd,
                   cast_stacks=(), cast_layer=0, tm=512, fc=256):
    B, T, D = x.shape
    xf = x.reshape(B * T, D)
    n_steps = B * T // tm
    half = tm // 2
    last_half = B * T // half - 1
    cast_in, cast_out, cast_shape = _cast_rider(cast_stacks, cast_layer, n_steps, lambda i: i)
    out, *cast = pl.pallas_call(
        functools.partial(_pool_ffn_kernel, tm=tm, fc=fc, n_cast=len(cast_stacks),
                          tiles_per_seq=T // tm),
        grid=(n_steps,),
        in_specs=[
            pl.BlockSpec((tm, D), lambda i: (i, 0)),
            pl.BlockSpec((half, D), lambda i: (jnp.minimum(2 * i + 2, last_half), 0)),
            _resident((1, D)),
            _resident_layer(pool_w.shape, layer),
            _resident((1, D)),
            _resident((1, D)),
            _resident(wgu.shape),
            _resident(wd.shape),
        ] + cast_in,
        out_specs=[pl.BlockSpec((tm, D), lambda i: (i, 0))] + cast_out,
        out_shape=[jax.ShapeDtypeStruct(xf.shape, xf.dtype)] + cast_shape,
        scratch_shapes=[
            pltpu.VMEM((tm, D_FF), jnp.bfloat16),
            pltpu.VMEM((tm, D), jnp.float32),
            pltpu.VMEM((POOL_HALO, D), jnp.float32),
        ],
        compiler_params=pltpu.CompilerParams(
            dimension_semantics=("arbitrary",),
            vmem_limit_bytes=VMEM_LIMIT),
        name="pool_layer",
    )(xf, xf, g_mixer.reshape(1, D), pool_w, pool_scale.reshape(1, D), g_ffn.reshape(1, D),
      wgu, wd, *cast_stacks)
    return [out.reshape(B, T, D)] + cast


def _kv_kernel(x_ref, g_ref, wk_ref, wvt_ref, k_ref, vt_ref, km_ref, *, blocks_per_tile):
    j = pl.program_id(1)
    h = _rms(x_ref[0], g_ref[...]).astype(jnp.bfloat16)
    k = jnp.dot(h, wk_ref[...], preferred_element_type=jnp.float32)
    k_ref[0] = k.astype(jnp.bfloat16)
    vt = lax.dot_general(wvt_ref[...], h, _NT, preferred_element_type=jnp.float32)
    vt_ref[0] = vt.astype(jnp.bfloat16)
    for i in range(blocks_per_tile):
        km_ref[0, pl.ds(j * blocks_per_tile + i, 1), :] = jnp.mean(
            k[i * MOBA_BLOCK:(i + 1) * MOBA_BLOCK], axis=0, keepdims=True)


def _kv_call(x, g, wk, wvt, tm=1024):
    B, T, D = x.shape
    nb = T // MOBA_BLOCK
    return pl.pallas_call(
        functools.partial(_kv_kernel, blocks_per_tile=tm // MOBA_BLOCK),
        grid=(B, T // tm),
        in_specs=[
            pl.BlockSpec((1, tm, D), lambda b, j: (b, j, 0)),
            _resident((1, D)),
            _resident(wk.shape),
            _resident(wvt.shape),
        ],
        out_specs=[
            pl.BlockSpec((1, tm, D), lambda b, j: (b, j, 0)),
            pl.BlockSpec((1, D, tm), lambda b, j: (b, 0, j)),
            pl.BlockSpec((1, nb, D), lambda b, j: (b, 0, 0)),
        ],
        out_shape=[
            jax.ShapeDtypeStruct((B, T, D), jnp.bfloat16),
            jax.ShapeDtypeStruct((B, D, T), jnp.bfloat16),
            jax.ShapeDtypeStruct((B, nb, D), jnp.float32),
        ],
        compiler_params=pltpu.CompilerParams(
            dimension_semantics=("arbitrary", "arbitrary"),
            vmem_limit_bytes=VMEM_LIMIT),
        name="shared_kv",
    )(x, g.reshape(1, D), wk, wvt)


def _rel_bucket(n):
    max_exact = NUM_BUCKETS // 2
    nf = jnp.maximum(n, max_exact).astype(jnp.float32)
    large = max_exact + (jnp.log(nf / max_exact) / math.log(MAX_DISTANCE / max_exact)
                         * (NUM_BUCKETS - max_exact)).astype(jnp.int32)
    large = jnp.minimum(large, NUM_BUCKETS - 1)
    return jnp.where(n < max_exact, n, large)


def _bias_kernel(rb_ref, *rest, n_cast):
    cast_in, (o_ref, *cast_out) = rest[:n_cast], rest[n_cast:]
    _cast_slabs(cast_in, cast_out)
    hd = pl.program_id(0)
    kl = lax.broadcasted_iota(jnp.int32, (MOBA_BLOCK, MOBA_BLOCK), 0)
    ql = lax.broadcasted_iota(jnp.int32, (MOBA_BLOCK, MOBA_BLOCK), 1)
    d_own = ql - kl
    for ti, dist in enumerate((d_own, d_own + MOBA_BLOCK)):
        bucket = _rel_bucket(jnp.maximum(dist, 0))
        tile = jnp.zeros((MOBA_BLOCK, MOBA_BLOCK), jnp.float32)
        for bi in range(NUM_BUCKETS):
            tile = jnp.where(bucket == bi, rb_ref[bi, hd], tile)
        if ti == 0:
            tile = jnp.where(dist >= 0, tile, _MASKED)
        o_ref[0, ti] = tile * LOG2E


def _bias_call(rel_bias, cast_stacks=(), cast_layer=0):
    cast_in, cast_out, cast_shape = _cast_rider(cast_stacks, cast_layer, N_HEADS, lambda h: h)
    return pl.pallas_call(
        functools.partial(_bias_kernel, n_cast=len(cast_stacks)),
        grid=(N_HEADS,),
        in_specs=[pl.BlockSpec(memory_space=pltpu.SMEM)] + cast_in,
        out_specs=[pl.BlockSpec((1, 2, MOBA_BLOCK, MOBA_BLOCK), lambda h: (h, 0, 0, 0))] + cast_out,
        out_shape=[jax.ShapeDtypeStruct((N_HEADS, 2, MOBA_BLOCK, MOBA_BLOCK), jnp.float32)]
        + cast_shape,
        compiler_params=pltpu.CompilerParams(dimension_semantics=("arbitrary",)),
        name="rel_bias_tiles",
    )(rel_bias, *cast_stacks)


def _sublane_all(op, x):
    shift = SUBLANES // 2
    while shift:
        x = op(x, pltpu.roll(x, shift, axis=0))
        shift //= 2
    return x


def _attn_kernel(rb_ref, x_ref, g_ref, wqt_ref, wo_ref, k_ref, vt_ref, km_ref, bias_ref,
                 o_ref, qh_ref, mask_ref, m_ref, l_ref, acc_ref):
    first_own = pl.program_id(1) * Q_SUBTILES
    nb = km_ref.shape[1]
    tq = MOBA_BLOCK
    D = x_ref.shape[2]
    row_groups = MOBA_BLOCK // SUBLANES
    head_groups = HEAD_DIM // SUBLANES
    m_ref[...] = jnp.full(m_ref.shape, _MASKED, jnp.float32)
    l_ref[...] = jnp.zeros(l_ref.shape, jnp.float32)
    acc_ref[...] = jnp.zeros(acc_ref.shape, jnp.float32)
    x = x_ref[0]
    h = _rms(x, g_ref[...]).astype(jnp.bfloat16)
    qf = lax.dot_general(wqt_ref[...], h, _NT, preferred_element_type=jnp.float32)
    q_gate = (qf * HEAD_DIM ** -0.5).astype(jnp.bfloat16)
    q = (qf * (HEAD_DIM ** -0.5 * LOG2E)).astype(jnp.bfloat16)

    feat = lax.broadcasted_iota(jnp.int32, (LANES, 1), 0)
    for sub in range(Q_SUBTILES):
        for head in range(N_HEADS):
            p, hh = divmod(head, HEADS_PER_LANE_TILE)
            qp = q[p * LANES:(p + 1) * LANES, sub * tq:(sub + 1) * tq]
            in_head = (feat >= hh * HEAD_DIM) & (feat < (hh + 1) * HEAD_DIM)
            qh_ref[sub, head] = jnp.where(in_head, qp, jnp.zeros_like(qp))

    n_q = Q_SUBTILES * tq
    n_cand = nb - 1
    own_of_query = first_own + lax.broadcasted_iota(jnp.int32, (N_HEADS, n_q), 1) // tq
    past = [own_of_query > j for j in range(n_cand)]
    some_query_ranks = first_own + Q_SUBTILES - 1 > MOBA_TOPK

    @pl.when(some_query_ranks)
    def _():
        km = jnp.broadcast_to(km_ref[0, :n_cand][:, None, :], (n_cand, N_HEADS, D))
        km = km.reshape(n_cand * N_HEADS, D)
        row_head = lax.broadcasted_iota(jnp.int32, km.shape, 0) % N_HEADS
        col_head = lax.broadcasted_iota(jnp.int32, km.shape, 1) // HEAD_DIM
        km_bd = jnp.where(row_head == col_head, km, 0.0).astype(jnp.bfloat16)
        gate = jnp.dot(km_bd, q_gate, preferred_element_type=jnp.float32)
        gates = [jnp.where(past[j], gate[j * N_HEADS:(j + 1) * N_HEADS], _NEG_INF)
                 for j in range(n_cand)]
        rank = [jnp.zeros((N_HEADS, n_q), jnp.int32) for _ in range(n_cand)]
        for lo in range(n_cand):
            for hi in range(lo + 1, n_cand):
                lo_wins = jnp.where(gates[lo] >= gates[hi], 1, 0)
                rank[hi] = rank[hi] + lo_wins
                rank[lo] = rank[lo] + (1 - lo_wins)
        for j in range(n_cand):
            keep = jnp.where(past[j], 0.0, _MASKED)
            mask_ref[j] = jnp.where(rank[j] < MOBA_TOPK, keep, _MASKED)

    @pl.when(jnp.logical_not(some_query_ranks))
    def _():
        for j in range(n_cand):
            mask_ref[j] = jnp.where(past[j], 0.0, _MASKED)


    def attend(blocks):
        units = []
        for j, kinds in blocks:
            keys = pl.ds(pl.multiple_of(j * MOBA_BLOCK, MOBA_BLOCK), MOBA_BLOCK)
            units += [(j, keys, kind, sub, head) for head in range(N_HEADS)
                      for sub, kind in enumerate(kinds) if kind is not None]

        def scores(unit):
            _, keys, _, sub, head = unit
            p = head // HEADS_PER_LANE_TILE
            kj = k_ref[0, keys, p * LANES:(p + 1) * LANES]
            return jnp.dot(kj, qh_ref[sub, head], preferred_element_type=jnp.float32)

        def softmax(unit, s):
            j, _, kind, sub, head = unit
            if kind != "far":
                tile = bias_ref[head, 0 if kind == "own" else 1]
                s = (s.reshape(row_groups, SUBLANES, tq)
                     + tile.reshape(row_groups, SUBLANES, tq)).reshape(MOBA_BLOCK, tq)
            sb = s.astype(jnp.bfloat16).reshape(MOBA_BLOCK // PACKED_ROWS, PACKED_ROWS, tq)
            bm = jnp.max(sb, axis=0).astype(jnp.float32)
            block_max = _sublane_all(jnp.maximum, jnp.maximum(bm[:SUBLANES], bm[SUBLANES:]))
            m_old = m_ref[sub, head]
            bias = rb_ref[NUM_BUCKETS - 1, head] * LOG2E if kind == "far" else 0.0
            top = block_max + bias
            if kind != "own":
                selected = jnp.broadcast_to(
                    mask_ref[j, head:head + 1, sub * tq:(sub + 1) * tq], (SUBLANES, tq)) == 0.0
                top = jnp.where(selected, top, _MASKED)
            shift = jnp.maximum(jnp.maximum(m_old, top) - bias, block_max)
            shift = shift.astype(jnp.bfloat16).astype(jnp.float32)
            shift_b = jnp.concatenate([shift, shift], axis=0).astype(jnp.bfloat16)
            pj = jnp.exp2(sb - shift_b[None])
            ref = shift + bias
            if kind != "own":
                m_new = jnp.where(selected, jnp.maximum(m_old, ref), m_old)
                scale_new = jnp.where(selected, jnp.exp2(ref - m_new), 0.0)
            else:
                m_new = jnp.maximum(m_old, ref)
                scale_new = jnp.exp2(ref - m_new)
            scale_old = jnp.exp2(m_old - m_new)
            m_ref[sub, head] = m_new
            return pj.reshape(MOBA_BLOCK, tq), (scale_old, scale_new)

        def accumulate(unit, pv, scales):
            _, _, _, sub, head = unit
            scale_old, scale_new = scales
            hs = slice(head * HEAD_DIM, (head + 1) * HEAD_DIM)
            l_ref[sub, head] = (l_ref[sub, head] * scale_old
                                + pv[HEAD_DIM:HEAD_DIM + SUBLANES] * scale_new)
            acc = acc_ref[sub, hs, :].reshape(head_groups, SUBLANES, tq)
            acc = (acc * scale_old[None]
                   + pv[:HEAD_DIM].reshape(head_groups, SUBLANES, tq) * scale_new[None])
            acc_ref[sub, hs, :] = acc.reshape(HEAD_DIM, tq)

        ones = jnp.ones((ONES_ROWS, MOBA_BLOCK), jnp.bfloat16)

        def weighted_values(unit, pj):
            _, keys, _, _, head = unit
            vj = vt_ref[0, head * HEAD_DIM:(head + 1) * HEAD_DIM, keys]
            return jnp.dot(jnp.concatenate([vj, ones], axis=0), pj,
                           preferred_element_type=jnp.float32)

        n = len(units)
        pending_scores = {i: scores(units[i]) for i in range(SCORE_LOOKAHEAD)}
        probs = {}
        products = {}
        for step in range(n + 2):
            if step + SCORE_LOOKAHEAD < n:
                pending_scores[step + SCORE_LOOKAHEAD] = scores(units[step + SCORE_LOOKAHEAD])
            if 0 <= step - 1 < n:
                pj, alpha = probs.pop(step - 1)
                products[step - 1] = (weighted_values(units[step - 1], pj), alpha)
            if step < n:
                probs[step] = softmax(units[step], pending_scores.pop(step))
            if 0 <= step - 2 < n:
                accumulate(units[step - 2], *products.pop(step - 2))

    assert Q_SUBTILES == 2 and FAR_UNROLL == 2
    all_far = ("far", "far")
    tail = [(first_own, ("own", "prev")), (first_own + 1, (None, "own"))]

    def far_pair(t, carry):
        attend([(FAR_UNROLL * t, all_far), (FAR_UNROLL * t + 1, all_far)])
        return carry

    lax.fori_loop(0, jnp.maximum(first_own - 1, 0) // FAR_UNROLL, far_pair, 0)

    @pl.when(first_own > 0)
    def _():
        attend([(first_own - 2, all_far), (first_own - 1, ("prev", "far"))] + tail)

    @pl.when(first_own == 0)
    def _():
        attend(tail)

    outs = []
    for sub in range(Q_SUBTILES):
        for head in range(N_HEADS):
            hs = slice(head * HEAD_DIM, (head + 1) * HEAD_DIM)
            inv = 1.0 / l_ref[sub, head]
            acc = acc_ref[sub, hs, :].reshape(head_groups, SUBLANES, tq) * inv[None]
            acc_ref[sub, hs, :] = acc.reshape(HEAD_DIM, tq)
        outs.append(acc_ref[sub].T.astype(jnp.bfloat16))
    o = jnp.concatenate(outs, axis=0)
    o_ref[0] = x + jnp.dot(o, wo_ref[...], preferred_element_type=jnp.float32)


def _attn_call(x, g, wqt, wo, layer, k, vt, km, bias_tiles, rel_bias):
    B, T, D = x.shape
    nb = T // MOBA_BLOCK
    tq = MOBA_BLOCK
    n_q = Q_SUBTILES * tq
    return pl.pallas_call(
        _attn_kernel,
        grid=(B, T // n_q),
        in_specs=[
            pl.BlockSpec(memory_space=pltpu.SMEM),
            pl.BlockSpec((1, n_q, D), lambda b, i: (b, i, 0)),
            _resident((1, D)),
            _resident_layer(wqt.shape, layer),
            _resident_layer(wo.shape, layer),
            pl.BlockSpec((1, T, D), lambda b, i: (b, 0, 0)),
            pl.BlockSpec((1, D, T), lambda b, i: (b, 0, 0)),
            pl.BlockSpec((1, nb, D), lambda b, i: (b, 0, 0)),
            _resident(bias_tiles.shape),
        ],
        out_specs=pl.BlockSpec((1, n_q, D), lambda b, i: (b, i, 0)),
        out_shape=jax.ShapeDtypeStruct(x.shape, x.dtype),
        scratch_shapes=[
            pltpu.VMEM((Q_SUBTILES, N_HEADS, LANES, tq), jnp.bfloat16),
            pltpu.VMEM((nb - 1, N_HEADS, n_q), jnp.float32),
            pltpu.VMEM((Q_SUBTILES, N_HEADS, SUBLANES, tq), jnp.float32),
            pltpu.VMEM((Q_SUBTILES, N_HEADS, SUBLANES, tq), jnp.float32),
            pltpu.VMEM((Q_SUBTILES, D, tq), jnp.float32),
        ],
        compiler_params=pltpu.CompilerParams(
            dimension_semantics=("arbitrary", "arbitrary"),
            vmem_limit_bytes=VMEM_LIMIT),
        name="moba_attention",
    )(rel_bias, x, g.reshape(1, D), wqt, wo, k, vt, km, bias_tiles)


def kernel(x, norm_mixer, norm_ffn, pool_w, pool_scale, kv_norm, w_kv, w_q, w_o,
           rel_bias, w_gate_up, w_down, final_norm):
    bf = jnp.bfloat16
    assert x.shape[1] % MOBA_BLOCK == 0 and x.shape[2] == D_MODEL
    pool_w = pool_w.astype(bf)
    w_qt = jnp.swapaxes(w_q, 1, 2).astype(bf)
    w_o = w_o.astype(bf)
    wk = w_kv[:, :D_MODEL].astype(bf)
    wvt = w_kv[:, D_MODEL:].T.astype(bf)

    ffn_stacks = (w_gate_up, w_down)

    bias_tiles, wgu, wd = _bias_call(rel_bias, ffn_stacks, 0)

    k = vt = km = None
    for layer in range(DEPTH):
        last = layer == DEPTH - 1
        rider = dict(cast_stacks=() if last else ffn_stacks, cast_layer=layer + 1)
        if layer < N_A_LAYERS:
            x, *cast = _pool_ffn_call(x, norm_mixer[layer], pool_w, layer, pool_scale[layer],
                                      norm_ffn[layer], wgu, wd, **rider)
        else:
            j = layer - N_A_LAYERS
            if j == 0:
                k, vt, km = _kv_call(x, kv_norm, wk, wvt)
            x = _attn_call(x, norm_mixer[layer], w_qt, w_o, j, k, vt, km, bias_tiles, rel_bias)
            x, *cast = _ffn_call(x, norm_ffn[layer], wgu, wd, final_norm, final_norm=last,
                                 **rider)
        if cast:
            wgu, wd = cast
    return x
```

```python
import functools
import math

import jax
import jax.numpy as jnp
from jax import lax
from jax.experimental import pallas as pl
from jax.experimental.pallas import tpu as pltpu

D_MODEL = 1024
DEPTH = 4
N_A_LAYERS = DEPTH // 2
POOL_WINDOWS = (2, 4, 8, 16)
POOL_GROUP = D_MODEL // len(POOL_WINDOWS)
HEAD_DIM = 64
N_HEADS = D_MODEL // HEAD_DIM
MOBA_BLOCK = 256
MOBA_TOPK = 3
NUM_BUCKETS = 32
MAX_DISTANCE = 128
D_FF = -(-8 * D_MODEL // (3 * 256)) * 256
EPS = 1e-6

LANES = 128
SUBLANES = 8
PACKED_ROWS = 16
POOL_HALO = 16
assert all(w & (w - 1) == 0 and w <= POOL_HALO for w in POOL_WINDOWS)
HEADS_PER_LANE_TILE = LANES // HEAD_DIM
VMEM_LIMIT = 56 * 1024 * 1024
SCORE_LOOKAHEAD = 4
FAR_UNROLL = 2
Q_SUBTILES = 2

_NT = (((1,), (1,)), ((), ()))
_NEG_INF = float("-inf")
_MASKED = -1e30
LOG2E = math.log2(math.e)
ONES_ROWS = 16


def _rms(x, g):
    return x * lax.rsqrt(jnp.mean(x * x, axis=-1, keepdims=True) + EPS) * g


def _resident(shape):
    zeros = (0,) * len(shape)
    return pl.BlockSpec(shape, lambda *_: zeros, pipeline_mode=pl.Buffered(1))


def _resident_layer(stacked_shape, layer):
    index = (layer,) + (0,) * (len(stacked_shape) - 1)
    return pl.BlockSpec((None,) + tuple(stacked_shape[1:]), lambda *_: index,
                        pipeline_mode=pl.Buffered(1))


def _cast_rider(stacks, layer, n_steps, step_of):
    in_specs, out_specs, out_shape = [], [], []
    for w in stacks:
        n_slabs = n_steps if (w.shape[1] // n_steps) % PACKED_ROWS == 0 else n_steps // 2
        rows = w.shape[1] // n_slabs
        assert rows * n_slabs == w.shape[1] and rows % PACKED_ROWS == 0 and n_steps % n_slabs == 0
        in_specs.append(pl.BlockSpec(
            (None, rows, w.shape[2]),
            lambda *idx, n_slabs=n_slabs: (layer, step_of(*idx) * n_slabs // n_steps, 0)))
        out_specs.append(pl.BlockSpec(
            (rows, w.shape[2]),
            lambda *idx, n_slabs=n_slabs: (step_of(*idx) * n_slabs // n_steps, 0)))
        out_shape.append(jax.ShapeDtypeStruct(w.shape[1:], jnp.bfloat16))
    return in_specs, out_specs, out_shape


def _cast_slabs(in_refs, out_refs):
    for src, dst in zip(in_refs, out_refs):
        dst[...] = src[...].astype(jnp.bfloat16)


def _ffn_kernel(x_ref, g_ref, wgu_ref, wd_ref, fg_ref, *rest, fc, final_norm, n_cast):
    cast_in, (o_ref, *cast_out, act_ref) = rest[:n_cast], rest[n_cast:]
    _cast_slabs(cast_in, cast_out)
    x = x_ref[...]
    h = _rms(x, g_ref[...]).astype(jnp.bfloat16)
    for c in range(D_FF // fc):
        gate = jnp.dot(h, wgu_ref[:, c * fc:(c + 1) * fc],
                       preferred_element_type=jnp.float32)
        up = jnp.dot(h, wgu_ref[:, D_FF + c * fc:D_FF + (c + 1) * fc],
                     preferred_element_type=jnp.float32)
        act = gate * (1.0 / (1.0 + jnp.exp(-gate))) * up
        act_ref[:, c * fc:(c + 1) * fc] = act.astype(jnp.bfloat16)
    y = x + jnp.dot(act_ref[...], wd_ref[...], preferred_element_type=jnp.float32)
    if final_norm:
        y = _rms(y, fg_ref[...])
    o_ref[...] = y


def _ffn_call(x, g, wgu, wd, final_g, final_norm, cast_stacks=(), cast_layer=0,
              tm=1024, fc=256):
    B, T, D = x.shape
    xf = x.reshape(B * T, D)
    n_steps = B * T // tm
    cast_in, cast_out, cast_shape = _cast_rider(cast_stacks, cast_layer, n_steps, lambda i: i)
    out, *cast = pl.pallas_call(
        functools.partial(_ffn_kernel, fc=fc, final_norm=final_norm, n_cast=len(cast_stacks)),
        grid=(n_steps,),
        in_specs=[
            pl.BlockSpec((tm, D), lambda i: (i, 0)),
            _resident((1, D)),
            _resident(wgu.shape),
            _resident(wd.shape),
            _resident((1, D)),
        ] + cast_in,
        out_specs=[pl.BlockSpec((tm, D), lambda i: (i, 0))] + cast_out,
        out_shape=[jax.ShapeDtypeStruct(xf.shape, xf.dtype)] + cast_shape,
        scratch_shapes=[pltpu.VMEM((tm, D_FF), jnp.bfloat16)],
        compiler_params=pltpu.CompilerParams(
            dimension_semantics=("arbitrary",),
            vmem_limit_bytes=VMEM_LIMIT),
        name="swiglu_ffn",
    )(xf, g.reshape(1, D), wgu, wd, final_g.reshape(1, D), *cast_stacks)
    return [out.reshape(B, T, D)] + cast


def _pool_ffn_kernel(x_ref, xn_ref, gm_ref, pw_ref, ps_ref, gf_ref, wgu_ref, wd_ref, *rest,
                     tm, fc, n_cast, tiles_per_seq):
    cast_in, (o_ref, *cast_out, act_ref, x1_ref, hf_ref, tail_ref) = rest[:n_cast], rest[n_cast:]
    _cast_slabs(cast_in, cast_out)
    step = pl.program_id(0)
    half = tm // 2
    gm = gm_ref[...]
    gf = gf_ref[...]

    def pool_stages(src_ref, r0, dst0, seq_pos, halo_h, state):
        rows = slice(r0, r0 + half)

        def norm():
            x = src_ref[rows, :]
            state["x"] = x
            state["h"] = _rms(x, gm)
            state["hh"] = jnp.concatenate([halo_h(), state["h"]], axis=0)

        def group(gi, win):
            sl = slice(gi * POOL_GROUP, (gi + 1) * POOL_GROUP)
            wsum = state["hh"][:, sl]
            span = 1
            while span < win:
                wsum = wsum + pltpu.roll(wsum, span, axis=0)
                span *= 2
            wsum = wsum[POOL_HALO:]
            t = seq_pos + lax.broadcasted_iota(jnp.int32, (half, 1), 0)
            cnt = jnp.minimum(t + 1, win).astype(jnp.float32)
            d = wsum / cnt - state["h"][:, sl]
            y = jnp.dot(d.astype(jnp.bfloat16), pw_ref[gi], preferred_element_type=jnp.float32)
            x1_ref[dst0:dst0 + half, sl] = state["x"][:, sl] + y * ps_ref[:, sl]

        def ffn_input():
            x1 = x1_ref[dst0:dst0 + half, :]
            hf_ref[dst0:dst0 + half, :] = _rms(x1, gf).astype(jnp.bfloat16)

        return ([norm] + [functools.partial(group, gi, win)
                          for gi, win in enumerate(POOL_WINDOWS)] + [ffn_input])

    def first_half_stages(src_ref, seq_tile, halo_h, state):
        def keep_tail():
            tail_ref[...] = state["h"][half - POOL_HALO:]
        return pool_stages(src_ref, 0, 0, seq_tile * tm, halo_h, state) + [keep_tail]

    def ffn_half(r0, between):
        rows = slice(r0, r0 + half)
        x1 = x1_ref[rows, :]
        h = hf_ref[rows, :]
        for c in range(D_FF // fc):
            gate = jnp.dot(h, wgu_ref[:, c * fc:(c + 1) * fc],
                           preferred_element_type=jnp.float32)
            up = jnp.dot(h, wgu_ref[:, D_FF + c * fc:D_FF + (c + 1) * fc],
                         preferred_element_type=jnp.float32)
            act = gate * (1.0 / (1.0 + jnp.exp(-gate))) * up
            act_ref[rows, c * fc:(c + 1) * fc] = act.astype(jnp.bfloat16)
            if c < len(between):
                between[c]()
        o_ref[rows, :] = x1 + jnp.dot(act_ref[rows, :], wd_ref[...],
                                      preferred_element_type=jnp.float32)

    seq_tile = step % tiles_per_seq
    next_seq_tile = (step + 1) % tiles_per_seq
    no_history = lambda: jnp.zeros((POOL_HALO, x_ref.shape[1]), jnp.float32)

    @pl.when(step == 0)
    def _():
        for stage in first_half_stages(x_ref, 0, no_history, {}):
            stage()

    second, upcoming = {}, {}
    ffn_half(0, pool_stages(x_ref, half, half, seq_tile * tm + half,
                            lambda: tail_ref[...], second))
    ffn_half(half, first_half_stages(
        xn_ref, next_seq_tile,
        lambda: jnp.where(next_seq_tile > 0, second["h"][half - POOL_HALO:], 0.0), upcoming))


def _pool_ffn_call(x, g_mixer, pool_w, layer, pool_scale, g_ffn, wgu, wd,
                   cast_stacks=(), cast_layer=0, tm=512, fc=256):
    B, T, D = x.shape
    xf = x.reshape(B * T, D)
    n_steps = B * T // tm
    half = tm // 2
    last_half = B * T // half - 1
    cast_in, cast_out, cast_shape = _cast_rider(cast_stacks, cast_layer, n_steps, lambda i: i)
    out, *cast = pl.pallas_call(
        functools.partial(_pool_ffn_kernel, tm=tm, fc=fc, n_cast=len(cast_stacks),
                          tiles_per_seq=T // tm),
        grid=(n_steps,),
        in_specs=[
            pl.BlockSpec((tm, D), lambda i: (i, 0)),
            pl.BlockSpec((half, D), lambda i: (jnp.minimum(2 * i + 2, last_half), 0)),
            _resident((1, D)),
            _resident_layer(pool_w.shape, layer),
            _resident((1, D)),
            _resident((1, D)),
            _resident(wgu.shape),
            _resident(wd.shape),
        ] + cast_in,
        out_specs=[pl.BlockSpec((tm, D), lambda i: (i, 0))] + cast_out,
        out_shape=[jax.ShapeDtypeStruct(xf.shape, xf.dtype)] + cast_shape,
        scratch_shapes=[
            pltpu.VMEM((tm, D_FF), jnp.bfloat16),
            pltpu.VMEM((tm, D), jnp.float32),
            pltpu.VMEM((tm, D), jnp.bfloat16),
            pltpu.VMEM((POOL_HALO, D), jnp.float32),
        ],
        compiler_params=pltpu.CompilerParams(
            dimension_semantics=("arbitrary",),
            vmem_limit_bytes=VMEM_LIMIT),
        name="pool_layer",
    )(xf, xf, g_mixer.reshape(1, D), pool_w, pool_scale.reshape(1, D), g_ffn.reshape(1, D),
      wgu, wd, *cast_stacks)
    return [out.reshape(B, T, D)] + cast


def _kv_kernel(x_ref, g_ref, wk_ref, wvt_ref, k_ref, vt_ref, km_ref, *, blocks_per_tile):
    j = pl.program_id(1)
    h = _rms(x_ref[0], g_ref[...]).astype(jnp.bfloat16)
    k = jnp.dot(h, wk_ref[...], preferred_element_type=jnp.float32)
    k_ref[0] = k.astype(jnp.bfloat16)
    vt = lax.dot_general(wvt_ref[...], h, _NT, preferred_element_type=jnp.float32)
    vt_ref[0] = vt.astype(jnp.bfloat16)
    for i in range(blocks_per_tile):
        km_ref[0, pl.ds(j * blocks_per_tile + i, 1), :] = jnp.mean(
            k[i * MOBA_BLOCK:(i + 1) * MOBA_BLOCK], axis=0, keepdims=True)


def _kv_call(x, g, wk, wvt, tm=1024):
    B, T, D = x.shape
    nb = T // MOBA_BLOCK
    return pl.pallas_call(
        functools.partial(_kv_kernel, blocks_per_tile=tm // MOBA_BLOCK),
        grid=(B, T // tm),
        in_specs=[
            pl.BlockSpec((1, tm, D), lambda b, j: (b, j, 0)),
            _resident((1, D)),
            _resident(wk.shape),
            _resident(wvt.shape),
        ],
        out_specs=[
            pl.BlockSpec((1, tm, D), lambda b, j: (b, j, 0)),
            pl.BlockSpec((1, D, tm), lambda b, j: (b, 0, j)),
            pl.BlockSpec((1, nb, D), lambda b, j: (b, 0, 0)),
        ],
        out_shape=[
            jax.ShapeDtypeStruct((B, T, D), jnp.bfloat16),
            jax.ShapeDtypeStruct((B, D, T), jnp.bfloat16),
            jax.ShapeDtypeStruct((B, nb, D), jnp.float32),
        ],
        compiler_params=pltpu.CompilerParams(
            dimension_semantics=("arbitrary", "arbitrary"),
            vmem_limit_bytes=VMEM_LIMIT),
        name="shared_kv",
    )(x, g.reshape(1, D), wk, wvt)


def _rel_bucket(n):
    max_exact = NUM_BUCKETS // 2
    nf = jnp.maximum(n, max_exact).astype(jnp.float32)
    large = max_exact + (jnp.log(nf / max_exact) / math.log(MAX_DISTANCE / max_exact)
                         * (NUM_BUCKETS - max_exact)).astype(jnp.int32)
    large = jnp.minimum(large, NUM_BUCKETS - 1)
    return jnp.where(n < max_exact, n, large)


def _bias_kernel(rb_ref, *rest, n_cast):
    cast_in, (o_ref, *cast_out) = rest[:n_cast], rest[n_cast:]
    _cast_slabs(cast_in, cast_out)
    hd = pl.program_id(0)
    quad = MOBA_BLOCK // 2
    kl = lax.broadcasted_iota(jnp.int32, (quad, quad), 0)
    ql = lax.broadcasted_iota(jnp.int32, (quad, quad), 1)
    for ti, offset in enumerate((0, MOBA_BLOCK)):
        for rk in range(2):
            for cq in range(2):
                d0 = offset + cq * quad - rk * quad
                d_min, d_max = d0 - (quad - 1), d0 + (quad - 1)
                if d_max < 0:
                    tile = jnp.full((quad, quad), _MASKED, jnp.float32)
                elif d_min >= MAX_DISTANCE:
                    tile = jnp.full((quad, quad), rb_ref[NUM_BUCKETS - 1, hd], jnp.float32)
                else:
                    dist = d0 + ql - kl
                    bucket = _rel_bucket(jnp.maximum(dist, 0))
                    tile = jnp.zeros((quad, quad), jnp.float32)
                    for bi in range(NUM_BUCKETS):
                        tile = jnp.where(bucket == bi, rb_ref[bi, hd], tile)
                    if d_min < 0:
                        tile = jnp.where(dist >= 0, tile, _MASKED)
                o_ref[0, ti, rk * quad:(rk + 1) * quad, cq * quad:(cq + 1) * quad] = tile * LOG2E


def _bias_call(rel_bias, cast_stacks=(), cast_layer=0):
    cast_in, cast_out, cast_shape = _cast_rider(cast_stacks, cast_layer, N_HEADS, lambda h: h)
    return pl.pallas_call(
        functools.partial(_bias_kernel, n_cast=len(cast_stacks)),
        grid=(N_HEADS,),
        in_specs=[pl.BlockSpec(memory_space=pltpu.SMEM)] + cast_in,
        out_specs=[pl.BlockSpec((1, 2, MOBA_BLOCK, MOBA_BLOCK), lambda h: (h, 0, 0, 0))] + cast_out,
        out_shape=[jax.ShapeDtypeStruct((N_HEADS, 2, MOBA_BLOCK, MOBA_BLOCK), jnp.float32)]
        + cast_shape,
        compiler_params=pltpu.CompilerParams(dimension_semantics=("arbitrary",)),
        name="rel_bias_tiles",
    )(rel_bias, *cast_stacks)


def _sublane_all(op, x):
    shift = SUBLANES // 2
    while shift:
        x = op(x, pltpu.roll(x, shift, axis=0))
        shift //= 2
    return x


def _attn_kernel(rb_ref, x_ref, g_ref, wqt_ref, wo_ref, k_ref, vt_ref, km_ref, bias_ref,
                 o_ref, qh_ref, mask_ref, m_ref, l_ref, acc_ref):
    first_own = pl.program_id(1) * Q_SUBTILES
    nb = km_ref.shape[1]
    tq = MOBA_BLOCK
    D = x_ref.shape[2]
    row_groups = MOBA_BLOCK // SUBLANES
    head_groups = HEAD_DIM // SUBLANES
    m_ref[...] = jnp.full(m_ref.shape, _MASKED, jnp.float32)
    l_ref[...] = jnp.zeros(l_ref.shape, jnp.float32)
    acc_ref[...] = jnp.zeros(acc_ref.shape, jnp.float32)
    x = x_ref[0]
    h = _rms(x, g_ref[...]).astype(jnp.bfloat16)
    qf = lax.dot_general(wqt_ref[...], h, _NT, preferred_element_type=jnp.float32)
    q_gate = (qf * HEAD_DIM ** -0.5).astype(jnp.bfloat16)
    q = (qf * (HEAD_DIM ** -0.5 * LOG2E)).astype(jnp.bfloat16)

    feat = lax.broadcasted_iota(jnp.int32, (LANES, 1), 0)
    for sub in range(Q_SUBTILES):
        for head in range(N_HEADS):
            p, hh = divmod(head, HEADS_PER_LANE_TILE)
            qp = q[p * LANES:(p + 1) * LANES, sub * tq:(sub + 1) * tq]
            in_head = (feat >= hh * HEAD_DIM) & (feat < (hh + 1) * HEAD_DIM)
            qh_ref[sub, head] = jnp.where(in_head, qp, jnp.zeros_like(qp))

    n_q = Q_SUBTILES * tq
    n_cand = nb - 1
    own_of_query = first_own + lax.broadcasted_iota(jnp.int32, (N_HEADS, n_q), 1) // tq
    past = [own_of_query > j for j in range(n_cand)]
    some_query_ranks = first_own + Q_SUBTILES - 1 > MOBA_TOPK

    @pl.when(some_query_ranks)
    def _():
        km = jnp.broadcast_to(km_ref[0, :n_cand][:, None, :], (n_cand, N_HEADS, D))
        km = km.reshape(n_cand * N_HEADS, D)
        row_head = lax.broadcasted_iota(jnp.int32, km.shape, 0) % N_HEADS
        col_head = lax.broadcasted_iota(jnp.int32, km.shape, 1) // HEAD_DIM
        km_bd = jnp.where(row_head == col_head, km, 0.0).astype(jnp.bfloat16)
        gate = jnp.dot(km_bd, q_gate, preferred_element_type=jnp.float32)
        gates = [jnp.where(past[j], gate[j * N_HEADS:(j + 1) * N_HEADS], _NEG_INF)
                 for j in range(n_cand)]
        rank = [jnp.zeros((N_HEADS, n_q), jnp.int32) for _ in range(n_cand)]
        for lo in range(n_cand):
            for hi in range(lo + 1, n_cand):
                lo_wins = jnp.where(gates[lo] >= gates[hi], 1, 0)
                rank[hi] = rank[hi] + lo_wins
                rank[lo] = rank[lo] + (1 - lo_wins)
        for j in range(n_cand):
            keep = jnp.where(past[j], 0.0, _MASKED)
            mask_ref[j] = jnp.where(rank[j] < MOBA_TOPK, keep, _MASKED)

    @pl.when(jnp.logical_not(some_query_ranks))
    def _():
        for j in range(n_cand):
            mask_ref[j] = jnp.where(past[j], 0.0, _MASKED)


    def attend(blocks):
        units = []
        for j, kinds in blocks:
            keys = pl.ds(pl.multiple_of(j * MOBA_BLOCK, MOBA_BLOCK), MOBA_BLOCK)
            units += [(j, keys, kind, sub, head) for head in range(N_HEADS)
                      for sub, kind in enumerate(kinds) if kind is not None]

        def scores(unit):
            _, keys, _, sub, head = unit
            p = head // HEADS_PER_LANE_TILE
            kj = k_ref[0, keys, p * LANES:(p + 1) * LANES]
            return jnp.dot(kj, qh_ref[sub, head], preferred_element_type=jnp.float32)

        def softmax(unit, s):
            j, _, kind, sub, head = unit
            if kind != "far":
                tile = bias_ref[head, 0 if kind == "own" else 1]
                s = (s.reshape(row_groups, SUBLANES, tq)
                     + tile.reshape(row_groups, SUBLANES, tq)).reshape(MOBA_BLOCK, tq)
            sb = s.astype(jnp.bfloat16).reshape(MOBA_BLOCK // PACKED_ROWS, PACKED_ROWS, tq)
            bm = jnp.max(sb, axis=0).astype(jnp.float32)
            block_max = _sublane_all(jnp.maximum, jnp.maximum(bm[:SUBLANES], bm[SUBLANES:]))
            m_old = m_ref[sub, head]
            bias = rb_ref[NUM_BUCKETS - 1, head] * LOG2E if kind == "far" else 0.0
            top = block_max + bias
            if kind != "own":
                selected = jnp.broadcast_to(
                    mask_ref[j, head:head + 1, sub * tq:(sub + 1) * tq], (SUBLANES, tq)) == 0.0
                top = jnp.where(selected, top, _MASKED)
            shift = jnp.maximum(jnp.maximum(m_old, top) - bias, block_max)
            shift = shift.astype(jnp.bfloat16).astype(jnp.float32)
            shift_b = jnp.concatenate([shift, shift], axis=0).astype(jnp.bfloat16)
            pj = jnp.exp2(sb - shift_b[None])
            ref = shift + bias
            if kind != "own":
                m_new = jnp.where(selected, jnp.maximum(m_old, ref), m_old)
                scale_new = jnp.where(selected, jnp.exp2(ref - m_new), 0.0)
            else:
                m_new = jnp.maximum(m_old, ref)
                scale_new = jnp.exp2(ref - m_new)
            scale_old = jnp.exp2(m_old - m_new)
            m_ref[sub, head] = m_new
            return pj.reshape(MOBA_BLOCK, tq), (scale_old, scale_new)

        def accumulate(unit, pv, scales):
            _, _, _, sub, head = unit
            scale_old, scale_new = scales
            hs = slice(head * HEAD_DIM, (head + 1) * HEAD_DIM)
            l_ref[sub, head] = (l_ref[sub, head] * scale_old
                                + pv[HEAD_DIM:HEAD_DIM + SUBLANES] * scale_new)
            acc = acc_ref[sub, hs, :].reshape(head_groups, SUBLANES, tq)
            acc = (acc * scale_old[None]
                   + pv[:HEAD_DIM].reshape(head_groups, SUBLANES, tq) * scale_new[None])
            acc_ref[sub, hs, :] = acc.reshape(HEAD_DIM, tq)

        ones = jnp.ones((ONES_ROWS, MOBA_BLOCK), jnp.bfloat16)

        def weighted_values(unit, pj):
            _, keys, _, _, head = unit
            vj = vt_ref[0, head * HEAD_DIM:(head + 1) * HEAD_DIM, keys]
            return jnp.dot(jnp.concatenate([vj, ones], axis=0), pj,
                           preferred_element_type=jnp.float32)

        n = len(units)
        pending_scores = {i: scores(units[i]) for i in range(SCORE_LOOKAHEAD)}
        probs = {}
        products = {}
        for step in range(n + 2):
            if step + SCORE_LOOKAHEAD < n:
                pending_scores[step + SCORE_LOOKAHEAD] = scores(units[step + SCORE_LOOKAHEAD])
            if 0 <= step - 1 < n:
                pj, alpha = probs.pop(step - 1)
                products[step - 1] = (weighted_values(units[step - 1], pj), alpha)
            if step < n:
                probs[step] = softmax(units[step], pending_scores.pop(step))
            if 0 <= step - 2 < n:
                accumulate(units[step - 2], *products.pop(step - 2))

    assert Q_SUBTILES == 2 and FAR_UNROLL == 2
    all_far = ("far", "far")
    tail = [(first_own, ("own", "prev")), (first_own + 1, (None, "own"))]

    def far_pair(t, carry):
        attend([(FAR_UNROLL * t, all_far), (FAR_UNROLL * t + 1, all_far)])
        return carry

    lax.fori_loop(0, jnp.maximum(first_own - 1, 0) // FAR_UNROLL, far_pair, 0)

    @pl.when(first_own > 0)
    def _():
        attend([(first_own - 2, all_far), (first_own - 1, ("prev", "far"))] + tail)

    @pl.when(first_own == 0)
    def _():
        attend(tail)

    outs = []
    for sub in range(Q_SUBTILES):
        for head in range(N_HEADS):
            hs = slice(head * HEAD_DIM, (head + 1) * HEAD_DIM)
            inv = 1.0 / l_ref[sub, head]
            acc = acc_ref[sub, hs, :].reshape(head_groups, SUBLANES, tq) * inv[None]
            acc_ref[sub, hs, :] = acc.reshape(HEAD_DIM, tq)
        outs.append(acc_ref[sub].T.astype(jnp.bfloat16))
    o = jnp.concatenate(outs, axis=0)
    o_ref[0] = x + jnp.dot(o, wo_ref[...], preferred_element_type=jnp.float32)


def _attn_call(x, g, wqt, wo, layer, k, vt, km, bias_tiles, rel_bias):
    B, T, D = x.shape
    nb = T // MOBA_BLOCK
    tq = MOBA_BLOCK
    n_q = Q_SUBTILES * tq
    return pl.pallas_call(
        _attn_kernel,
        grid=(B, T // n_q),
        in_specs=[
            pl.BlockSpec(memory_space=pltpu.SMEM),
            pl.BlockSpec((1, n_q, D), lambda b, i: (b, i, 0)),
            _resident((1, D)),
            _resident_layer(wqt.shape, layer),
            _resident_layer(wo.shape, layer),
            pl.BlockSpec((1, T, D), lambda b, i: (b, 0, 0)),
            pl.BlockSpec((1, D, T), lambda b, i: (b, 0, 0)),
            pl.BlockSpec((1, nb, D), lambda b, i: (b, 0, 0)),
            _resident(bias_tiles.shape),
        ],
        out_specs=pl.BlockSpec((1, n_q, D), lambda b, i: (b, i, 0)),
        out_shape=jax.ShapeDtypeStruct(x.shape, x.dtype),
        scratch_shapes=[
            pltpu.VMEM((Q_SUBTILES, N_HEADS, LANES, tq), jnp.bfloat16),
            pltpu.VMEM((nb - 1, N_HEADS, n_q), jnp.float32),
            pltpu.VMEM((Q_SUBTILES, N_HEADS, SUBLANES, tq), jnp.float32),
            pltpu.VMEM((Q_SUBTILES, N_HEADS, SUBLANES, tq), jnp.float32),
            pltpu.VMEM((Q_SUBTILES, D, tq), jnp.float32),
        ],
        compiler_params=pltpu.CompilerParams(
            dimension_semantics=("arbitrary", "arbitrary"),
            vmem_limit_bytes=VMEM_LIMIT),
        name="moba_attention",
    )(rel_bias, x, g.reshape(1, D), wqt, wo, k, vt, km, bias_tiles)


def kernel(x, norm_mixer, norm_ffn, pool_w, pool_scale, kv_norm, w_kv, w_q, w_o,
           rel_bias, w_gate_up, w_down, final_norm):
    bf = jnp.bfloat16
    assert x.shape[1] % MOBA_BLOCK == 0 and x.shape[2] == D_MODEL
    pool_w = pool_w.astype(bf)
    w_qt = jnp.swapaxes(w_q, 1, 2).astype(bf)
    w_o = w_o.astype(bf)
    wk = w_kv[:, :D_MODEL].astype(bf)
    wvt = w_kv[:, D_MODEL:].T.astype(bf)

    ffn_stacks = (w_gate_up, w_down)

    bias_tiles, wgu, wd = _bias_call(rel_bias, ffn_stacks, 0)

    k = vt = km = None
    for layer in range(DEPTH):
        last = layer == DEPTH - 1
        rider = dict(cast_stacks=() if last else ffn_stacks, cast_layer=layer + 1)
        if layer < N_A_LAYERS:
            x, *cast = _pool_ffn_call(x, norm_mixer[layer], pool_w, layer, pool_scale[layer],
                                      norm_ffn[layer], wgu, wd, **rider)
        else:
            j = layer - N_A_LAYERS
            if j == 0:
                k, vt, km = _kv_call(x, kv_norm, wk, wvt)
            x = _attn_call(x, norm_mixer[layer], w_qt, w_o, j, k, vt, km, bias_tiles, rel_bias)
            x, *cast = _ffn_call(x, norm_ffn[layer], wgu, wd, final_norm, final_norm=last,
                                 **rider)
        if cast:
            wgu, wd = cast
    return x
```

```python
import functools
import math

import jax
import jax.numpy as jnp
from jax import lax
from jax.experimental import pallas as pl
from jax.experimental.pallas import tpu as pltpu

D_MODEL = 1024
DEPTH = 4
N_A_LAYERS = DEPTH // 2
POOL_WINDOWS = (2, 4, 8, 16)
POOL_GROUP = D_MODEL // len(POOL_WINDOWS)
HEAD_DIM = 64
N_HEADS = D_MODEL // HEAD_DIM
MOBA_BLOCK = 256
MOBA_TOPK = 3
NUM_BUCKETS = 32
MAX_DISTANCE = 128
D_FF = -(-8 * D_MODEL // (3 * 256)) * 256
EPS = 1e-6

LANES = 128
SUBLANES = 8
PACKED_ROWS = 16
POOL_HALO = 16
assert all(w & (w - 1) == 0 and w <= POOL_HALO for w in POOL_WINDOWS)
HEADS_PER_LANE_TILE = LANES // HEAD_DIM
VMEM_LIMIT = 56 * 1024 * 1024
SCORE_LOOKAHEAD = 4
FAR_UNROLL = 2
Q_SUBTILES = 2

_NT = (((1,), (1,)), ((), ()))
_NEG_INF = float("-inf")
_MASKED = -1e30
LOG2E = math.log2(math.e)
ONES_ROWS = 16


def _rms(x, g):
    return x * lax.rsqrt(jnp.mean(x * x, axis=-1, keepdims=True) + EPS) * g


def _resident(shape):
    zeros = (0,) * len(shape)
    return pl.BlockSpec(shape, lambda *_: zeros, pipeline_mode=pl.Buffered(1))


def _resident_layer(stacked_shape, layer):
    index = (layer,) + (0,) * (len(stacked_shape) - 1)
    return pl.BlockSpec((None,) + tuple(stacked_shape[1:]), lambda *_: index,
                        pipeline_mode=pl.Buffered(1))


def _cast_rider(stacks, layer, n_steps, step_of):
    in_specs, out_specs, out_shape = [], [], []
    for w in stacks:
        n_slabs = n_steps if (w.shape[1] // n_steps) % PACKED_ROWS == 0 else n_steps // 2
        rows = w.shape[1] // n_slabs
        assert rows * n_slabs == w.shape[1] and rows % PACKED_ROWS == 0 and n_steps % n_slabs == 0
        in_specs.append(pl.BlockSpec(
            (None, rows, w.shape[2]),
            lambda *idx, n_slabs=n_slabs: (layer, step_of(*idx) * n_slabs // n_steps, 0)))
        out_specs.append(pl.BlockSpec(
            (rows, w.shape[2]),
            lambda *idx, n_slabs=n_slabs: (step_of(*idx) * n_slabs // n_steps, 0)))
        out_shape.append(jax.ShapeDtypeStruct(w.shape[1:], jnp.bfloat16))
    return in_specs, out_specs, out_shape


def _cast_slabs(in_refs, out_refs):
    for src, dst in zip(in_refs, out_refs):
        dst[...] = src[...].astype(jnp.bfloat16)


def _ffn_kernel(x_ref, g_ref, wgu_ref, wd_ref, fg_ref, *rest, fc, final_norm, n_cast):
    cast_in, (o_ref, *cast_out, act_ref) = rest[:n_cast], rest[n_cast:]
    _cast_slabs(cast_in, cast_out)
    x = x_ref[...]
    h = _rms(x, g_ref[...]).astype(jnp.bfloat16)
    for c in range(D_FF // fc):
        gate = jnp.dot(h, wgu_ref[:, c * fc:(c + 1) * fc],
                       preferred_element_type=jnp.float32)
        up = jnp.dot(h, wgu_ref[:, D_FF + c * fc:D_FF + (c + 1) * fc],
                     preferred_element_type=jnp.float32)
        act = gate * (1.0 / (1.0 + jnp.exp(-gate))) * up
        act_ref[:, c * fc:(c + 1) * fc] = act.astype(jnp.bfloat16)
    y = x + jnp.dot(act_ref[...], wd_ref[...], preferred_element_type=jnp.float32)
    if final_norm:
        y = _rms(y, fg_ref[...])
    o_ref[...] = y


def _ffn_call(x, g, wgu, wd, final_g, final_norm, cast_stacks=(), cast_layer=0,
              tm=1024, fc=256):
    B, T, D = x.shape
    xf = x.reshape(B * T, D)
    n_steps = B * T // tm
    cast_in, cast_out, cast_shape = _cast_rider(cast_stacks, cast_layer, n_steps, lambda i: i)
    out, *cast = pl.pallas_call(
        functools.partial(_ffn_kernel, fc=fc, final_norm=final_norm, n_cast=len(cast_stacks)),
        grid=(n_steps,),
        in_specs=[
            pl.BlockSpec((tm, D), lambda i: (i, 0)),
            _resident((1, D)),
            _resident(wgu.shape),
            _resident(wd.shape),
            _resident((1, D)),
        ] + cast_in,
        out_specs=[pl.BlockSpec((tm, D), lambda i: (i, 0))] + cast_out,
        out_shape=[jax.ShapeDtypeStruct(xf.shape, xf.dtype)] + cast_shape,
        scratch_shapes=[pltpu.VMEM((tm, D_FF), jnp.bfloat16)],
        compiler_params=pltpu.CompilerParams(
            dimension_semantics=("arbitrary",),
            vmem_limit_bytes=VMEM_LIMIT),
        name="swiglu_ffn",
    )(xf, g.reshape(1, D), wgu, wd, final_g.reshape(1, D), *cast_stacks)
    return [out.reshape(B, T, D)] + cast


def _pool_ffn_kernel(x_ref, xn_ref, gm_ref, pw_ref, ps_ref, gf_ref, wgu_ref, wd_ref, *rest,
                     tm, fc, n_cast, tiles_per_seq):
    cast_in, (o_ref, *cast_out, act_ref, x1_ref, hf_ref, tail_ref) = rest[:n_cast], rest[n_cast:]
    _cast_slabs(cast_in, cast_out)
    step = pl.program_id(0)
    half = tm // 2
    gm = gm_ref[...]
    gf = gf_ref[...]

    def pool_stages(src_ref, r0, dst0, seq_pos, halo_h, state):
        rows = slice(r0, r0 + half)

        def norm():
            x = src_ref[rows, :]
            state["x"] = x
            state["h"] = _rms(x, gm)
            state["hh"] = jnp.concatenate([halo_h(), state["h"]], axis=0)

        def group(gi, win):
            sl = slice(gi * POOL_GROUP, (gi + 1) * POOL_GROUP)
            wsum = state["hh"][:, sl]
            span = 1
            while span < win:
                wsum = wsum + pltpu.roll(wsum, span, axis=0)
                span *= 2
            wsum = wsum[POOL_HALO:]
            t = seq_pos + lax.broadcasted_iota(jnp.int32, (half, 1), 0)
            cnt = jnp.minimum(t + 1, win).astype(jnp.float32)
            d = wsum / cnt - state["h"][:, sl]
            y = jnp.dot(d.astype(jnp.bfloat16), pw_ref[gi], preferred_element_type=jnp.float32)
            x1_ref[dst0:dst0 + half, sl] = state["x"][:, sl] + y * ps_ref[:, sl]

        def ffn_input():
            x1 = x1_ref[dst0:dst0 + half, :]
            hf_ref[dst0:dst0 + half, :] = _rms(x1, gf).astype(jnp.bfloat16)

        return ([norm] + [functools.partial(group, gi, win)
                          for gi, win in enumerate(POOL_WINDOWS)] + [ffn_input])

    def first_half_stages(src_ref, seq_tile, halo_h, state):
        def keep_tail():
            tail_ref[...] = state["h"][half - POOL_HALO:]
        return pool_stages(src_ref, 0, 0, seq_tile * tm, halo_h, state) + [keep_tail]

    def ffn_half(r0, between):
        rows = slice(r0, r0 + half)
        x1 = x1_ref[rows, :]
        h = hf_ref[rows, :]
        for c in range(D_FF // fc):
            gate = jnp.dot(h, wgu_ref[:, c * fc:(c + 1) * fc],
                           preferred_element_type=jnp.float32)
            up = jnp.dot(h, wgu_ref[:, D_FF + c * fc:D_FF + (c + 1) * fc],
                         preferred_element_type=jnp.float32)
            act = gate * (1.0 / (1.0 + jnp.exp(-gate))) * up
            act_ref[rows, c * fc:(c + 1) * fc] = act.astype(jnp.bfloat16)
            if c < len(between):
                between[c]()
        o_ref[rows, :] = x1 + jnp.dot(act_ref[rows, :], wd_ref[...],
                                      preferred_element_type=jnp.float32)

    seq_tile = step % tiles_per_seq
    next_seq_tile = (step + 1) % tiles_per_seq
    no_history = lambda: jnp.zeros((POOL_HALO, x_ref.shape[1]), jnp.float32)

    @pl.when(step == 0)
    def _():
        for stage in first_half_stages(x_ref, 0, no_history, {}):
            stage()

    second, upcoming = {}, {}
    ffn_half(0, pool_stages(x_ref, half, half, seq_tile * tm + half,
                            lambda: tail_ref[...], second))
    ffn_half(half, first_half_stages(
        xn_ref, next_seq_tile,
        lambda: jnp.where(next_seq_tile > 0, second["h"][half - POOL_HALO:], 0.0), upcoming))


def _pool_ffn_call(x, g_mixer, pool_w, layer, pool_scale, g_ffn, wgu, wd,
                   cast_stacks=(), cast_layer=0, tm=512, fc=256):
    B, T, D = x.shape
    xf = x.reshape(B * T, D)
    n_steps = B * T // tm
    half = tm // 2
    last_half = B * T // half - 1
    cast_in, cast_out, cast_shape = _cast_rider(cast_stacks, cast_layer, n_steps, lambda i: i)
    out, *cast = pl.pallas_call(
        functools.partial(_pool_ffn_kernel, tm=tm, fc=fc, n_cast=len(cast_stacks),
                          tiles_per_seq=T // tm),
        grid=(n_steps,),
        in_specs=[
            pl.BlockSpec((tm, D), lambda i: (i, 0)),
            pl.BlockSpec((half, D), lambda i: (jnp.minimum(2 * i + 2, last_half), 0)),
            _resident((1, D)),
            _resident_layer(pool_w.shape, layer),
            _resident((1, D)),
            _resident((1, D)),
            _resident(wgu.shape),
            _resident(wd.shape),
        ] + cast_in,
        out_specs=[pl.BlockSpec((tm, D), lambda i: (i, 0))] + cast_out,
        out_shape=[jax.ShapeDtypeStruct(xf.shape, xf.dtype)] + cast_shape,
        scratch_shapes=[
            pltpu.VMEM((tm, D_FF), jnp.bfloat16),
            pltpu.VMEM((tm, D), jnp.float32),
            pltpu.VMEM((tm, D), jnp.bfloat16),
            pltpu.VMEM((POOL_HALO, D), jnp.float32),
        ],
        compiler_params=pltpu.CompilerParams(
            dimension_semantics=("arbitrary",),
            vmem_limit_bytes=VMEM_LIMIT),
        name="pool_layer",
    )(xf, xf, g_mixer.reshape(1, D), pool_w, pool_scale.reshape(1, D), g_ffn.reshape(1, D),
      wgu, wd, *cast_stacks)
    return [out.reshape(B, T, D)] + cast


def _kv_kernel(x_ref, g_ref, wk_ref, wvt_ref, k_ref, vt_ref, km_ref, *, blocks_per_tile):
    j = pl.program_id(1)
    h = _rms(x_ref[0], g_ref[...]).astype(jnp.bfloat16)
    k = jnp.dot(h, wk_ref[...], preferred_element_type=jnp.float32)
    k_ref[0] = k.astype(jnp.bfloat16)
    vt = lax.dot_general(wvt_ref[...], h, _NT, preferred_element_type=jnp.float32)
    vt_ref[0] = vt.astype(jnp.bfloat16)
    for i in range(blocks_per_tile):
        km_ref[0, pl.ds(j * blocks_per_tile + i, 1), :] = jnp.mean(
            k[i * MOBA_BLOCK:(i + 1) * MOBA_BLOCK], axis=0, keepdims=True)


def _kv_call(x, g, wk, wvt, tm=1024):
    B, T, D = x.shape
    nb = T // MOBA_BLOCK
    return pl.pallas_call(
        functools.partial(_kv_kernel, blocks_per_tile=tm // MOBA_BLOCK),
        grid=(B, T // tm),
        in_specs=[
            pl.BlockSpec((1, tm, D), lambda b, j: (b, j, 0)),
            _resident((1, D)),
            _resident(wk.shape),
            _resident(wvt.shape),
        ],
        out_specs=[
            pl.BlockSpec((1, tm, D), lambda b, j: (b, j, 0)),
            pl.BlockSpec((1, D, tm), lambda b, j: (b, 0, j)),
            pl.BlockSpec((1, nb, D), lambda b, j: (b, 0, 0)),
        ],
        out_shape=[
            jax.ShapeDtypeStruct((B, T, D), jnp.bfloat16),
            jax.ShapeDtypeStruct((B, D, T), jnp.bfloat16),
            jax.ShapeDtypeStruct((B, nb, D), jnp.float32),
        ],
        compiler_params=pltpu.CompilerParams(
            dimension_semantics=("arbitrary", "arbitrary"),
            vmem_limit_bytes=VMEM_LIMIT),
        name="shared_kv",
    )(x, g.reshape(1, D), wk, wvt)


def _rel_bucket(n):
    max_exact = NUM_BUCKETS // 2
    nf = jnp.maximum(n, max_exact).astype(jnp.float32)
    large = max_exact + (jnp.log(nf / max_exact) / math.log(MAX_DISTANCE / max_exact)
                         * (NUM_BUCKETS - max_exact)).astype(jnp.int32)
    large = jnp.minimum(large, NUM_BUCKETS - 1)
    return jnp.where(n < max_exact, n, large)


def _bias_kernel(rb_ref, *rest, n_cast):
    cast_in, (o_ref, *cast_out) = rest[:n_cast], rest[n_cast:]
    _cast_slabs(cast_in, cast_out)
    hd = pl.program_id(0)
    quad = MOBA_BLOCK // 2
    kl = lax.broadcasted_iota(jnp.int32, (quad, quad), 0)
    ql = lax.broadcasted_iota(jnp.int32, (quad, quad), 1)
    for ti, offset in enumerate((0, MOBA_BLOCK)):
        for rk in range(2):
            for cq in range(2):
                d0 = offset + cq * quad - rk * quad
                d_min, d_max = d0 - (quad - 1), d0 + (quad - 1)
                if d_max < 0:
                    tile = jnp.full((quad, quad), _MASKED, jnp.float32)
                elif d_min >= MAX_DISTANCE:
                    tile = jnp.full((quad, quad), rb_ref[NUM_BUCKETS - 1, hd], jnp.float32)
                else:
                    dist = d0 + ql - kl
                    bucket = _rel_bucket(jnp.maximum(dist, 0))
                    tile = jnp.zeros((quad, quad), jnp.float32)
                    for bi in range(NUM_BUCKETS):
                        tile = jnp.where(bucket == bi, rb_ref[bi, hd], tile)
                    if d_min < 0:
                        tile = jnp.where(dist >= 0, tile, _MASKED)
                o_ref[0, ti, rk * quad:(rk + 1) * quad, cq * quad:(cq + 1) * quad] = tile * LOG2E


def _bias_call(rel_bias, cast_stacks=(), cast_layer=0):
    cast_in, cast_out, cast_shape = _cast_rider(cast_stacks, cast_layer, N_HEADS, lambda h: h)
    return pl.pallas_call(
        functools.partial(_bias_kernel, n_cast=len(cast_stacks)),
        grid=(N_HEADS,),
        in_specs=[pl.BlockSpec(memory_space=pltpu.SMEM)] + cast_in,
        out_specs=[pl.BlockSpec((1, 2, MOBA_BLOCK, MOBA_BLOCK), lambda h: (h, 0, 0, 0))] + cast_out,
        out_shape=[jax.ShapeDtypeStruct((N_HEADS, 2, MOBA_BLOCK, MOBA_BLOCK), jnp.float32)]
        + cast_shape,
        compiler_params=pltpu.CompilerParams(dimension_semantics=("arbitrary",)),
        name="rel_bias_tiles",
    )(rel_bias, *cast_stacks)


def _sublane_all(op, x):
    shift = SUBLANES // 2
    while shift:
        x = op(x, pltpu.roll(x, shift, axis=0))
        shift //= 2
    return x


def _attn_kernel(rb_ref, x_ref, g_ref, wqt_ref, wo_ref, k_ref, vt_ref, km_ref, bias_ref,
                 o_ref, qh_ref, mask_ref, m_ref, l_ref, acc_ref):
    first_own = pl.program_id(1) * Q_SUBTILES
    nb = km_ref.shape[1]
    tq = MOBA_BLOCK
    D = x_ref.shape[2]
    row_groups = MOBA_BLOCK // SUBLANES
    head_groups = HEAD_DIM // SUBLANES

    def reset_softmax_state():
        m_ref[...] = jnp.full(m_ref.shape, _MASKED, jnp.float32)
        l_ref[...] = jnp.zeros(l_ref.shape, jnp.float32)
        acc_ref[...] = jnp.zeros(acc_ref.shape, jnp.float32)

    @pl.when((pl.program_id(0) == 0) & (pl.program_id(1) == 0))
    def _():
        reset_softmax_state()

    x = x_ref[0]
    h = _rms(x, g_ref[...]).astype(jnp.bfloat16)
    qf = lax.dot_general(wqt_ref[...], h, _NT, preferred_element_type=jnp.float32)
    q_gate = (qf * HEAD_DIM ** -0.5).astype(jnp.bfloat16)
    q = (qf * (HEAD_DIM ** -0.5 * LOG2E)).astype(jnp.bfloat16)

    feat = lax.broadcasted_iota(jnp.int32, (LANES, 1), 0)
    for sub in range(Q_SUBTILES):
        for head in range(N_HEADS):
            p, hh = divmod(head, HEADS_PER_LANE_TILE)
            qp = q[p * LANES:(p + 1) * LANES, sub * tq:(sub + 1) * tq]
            in_head = (feat >= hh * HEAD_DIM) & (feat < (hh + 1) * HEAD_DIM)
            qh_ref[sub, head] = jnp.where(in_head, qp, jnp.zeros_like(qp))

    n_q = Q_SUBTILES * tq
    n_cand = nb - 1
    own_of_query = first_own + lax.broadcasted_iota(jnp.int32, (N_HEADS, n_q), 1) // tq
    past = [own_of_query > j for j in range(n_cand)]
    some_query_ranks = first_own + Q_SUBTILES - 1 > MOBA_TOPK

    @pl.when(some_query_ranks)
    def _():
        km = jnp.broadcast_to(km_ref[0, :n_cand][:, None, :], (n_cand, N_HEADS, D))
        km = km.reshape(n_cand * N_HEADS, D)
        row_head = lax.broadcasted_iota(jnp.int32, km.shape, 0) % N_HEADS
        col_head = lax.broadcasted_iota(jnp.int32, km.shape, 1) // HEAD_DIM
        km_bd = jnp.where(row_head == col_head, km, 0.0).astype(jnp.bfloat16)
        gate = jnp.dot(km_bd, q_gate, preferred_element_type=jnp.float32)
        gates = [jnp.where(past[j], gate[j * N_HEADS:(j + 1) * N_HEADS], _NEG_INF)
                 for j in range(n_cand)]
        rank = [jnp.zeros((N_HEADS, n_q), jnp.int32) for _ in range(n_cand)]
        for lo in range(n_cand):
            for hi in range(lo + 1, n_cand):
                lo_wins = jnp.where(gates[lo] >= gates[hi], 1, 0)
                rank[hi] = rank[hi] + lo_wins
                rank[lo] = rank[lo] + (1 - lo_wins)
        for j in range(n_cand):
            keep = jnp.where(past[j], 0.0, _MASKED)
            mask_ref[j] = jnp.where(rank[j] < MOBA_TOPK, keep, _MASKED)

    @pl.when(jnp.logical_not(some_query_ranks))
    def _():
        for j in range(n_cand):
            mask_ref[j] = jnp.where(past[j], 0.0, _MASKED)


    def attend(blocks):
        units = []
        for j, kinds in blocks:
            keys = pl.ds(pl.multiple_of(j * MOBA_BLOCK, MOBA_BLOCK), MOBA_BLOCK)
            units += [(j, keys, kind, sub, head) for head in range(N_HEADS)
                      for sub, kind in enumerate(kinds) if kind is not None]

        def scores(unit):
            _, keys, _, sub, head = unit
            p = head // HEADS_PER_LANE_TILE
            kj = k_ref[0, keys, p * LANES:(p + 1) * LANES]
            return jnp.dot(kj, qh_ref[sub, head], preferred_element_type=jnp.float32)

        def softmax(unit, s):
            j, _, kind, sub, head = unit
            if kind != "far":
                tile = bias_ref[head, 0 if kind == "own" else 1]
                s = (s.reshape(row_groups, SUBLANES, tq)
                     + tile.reshape(row_groups, SUBLANES, tq)).reshape(MOBA_BLOCK, tq)
            sb = s.astype(jnp.bfloat16).reshape(MOBA_BLOCK // PACKED_ROWS, PACKED_ROWS, tq)
            bm = jnp.max(sb, axis=0).astype(jnp.float32)
            block_max = _sublane_all(jnp.maximum, jnp.maximum(bm[:SUBLANES], bm[SUBLANES:]))
            m_old = m_ref[sub, head]
            bias = rb_ref[NUM_BUCKETS - 1, head] * LOG2E if kind == "far" else 0.0
            top = block_max + bias
            if kind != "own":
                selected = jnp.broadcast_to(
                    mask_ref[j, head:head + 1, sub * tq:(sub + 1) * tq], (SUBLANES, tq)) == 0.0
                top = jnp.where(selected, top, _MASKED)
            shift = jnp.maximum(jnp.maximum(m_old, top) - bias, block_max)
            shift = shift.astype(jnp.bfloat16).astype(jnp.float32)
            shift_b = jnp.concatenate([shift, shift], axis=0).astype(jnp.bfloat16)
            pj = jnp.exp2(sb - shift_b[None])
            ref = shift + bias
            if kind != "own":
                m_new = jnp.where(selected, jnp.maximum(m_old, ref), m_old)
                scale_new = jnp.where(selected, jnp.exp2(ref - m_new), 0.0)
            else:
                m_new = jnp.maximum(m_old, ref)
                scale_new = jnp.exp2(ref - m_new)
            scale_old = jnp.exp2(m_old - m_new)
            m_ref[sub, head] = m_new
            return pj.reshape(MOBA_BLOCK, tq), (scale_old, scale_new)

        def accumulate(unit, pv, scales):
            _, _, _, sub, head = unit
            scale_old, scale_new = scales
            hs = slice(head * HEAD_DIM, (head + 1) * HEAD_DIM)
            l_ref[sub, head] = (l_ref[sub, head] * scale_old
                                + pv[HEAD_DIM:HEAD_DIM + SUBLANES] * scale_new)
            acc = acc_ref[sub, hs, :].reshape(head_groups, SUBLANES, tq)
            acc = (acc * scale_old[None]
                   + pv[:HEAD_DIM].reshape(head_groups, SUBLANES, tq) * scale_new[None])
            acc_ref[sub, hs, :] = acc.reshape(HEAD_DIM, tq)

        ones = jnp.ones((ONES_ROWS, MOBA_BLOCK), jnp.bfloat16)

        def weighted_values(unit, pj):
            _, keys, _, _, head = unit
            vj = vt_ref[0, head * HEAD_DIM:(head + 1) * HEAD_DIM, keys]
            return jnp.dot(jnp.concatenate([vj, ones], axis=0), pj,
                           preferred_element_type=jnp.float32)

        n = len(units)
        pending_scores = {i: scores(units[i]) for i in range(SCORE_LOOKAHEAD)}
        probs = {}
        products = {}
        for step in range(n + 2):
            if step + SCORE_LOOKAHEAD < n:
                pending_scores[step + SCORE_LOOKAHEAD] = scores(units[step + SCORE_LOOKAHEAD])
            if 0 <= step - 1 < n:
                pj, alpha = probs.pop(step - 1)
                products[step - 1] = (weighted_values(units[step - 1], pj), alpha)
            if step < n:
                probs[step] = softmax(units[step], pending_scores.pop(step))
            if 0 <= step - 2 < n:
                accumulate(units[step - 2], *products.pop(step - 2))

    assert Q_SUBTILES == 2 and FAR_UNROLL == 2
    all_far = ("far", "far")
    tail = [(first_own, ("own", "prev")), (first_own + 1, (None, "own"))]

    def far_pair(t, carry):
        attend([(FAR_UNROLL * t, all_far), (FAR_UNROLL * t + 1, all_far)])
        return carry

    lax.fori_loop(0, jnp.maximum(first_own - 1, 0) // FAR_UNROLL, far_pair, 0)

    @pl.when(first_own > 0)
    def _():
        attend([(first_own - 2, all_far), (first_own - 1, ("prev", "far"))] + tail)

    @pl.when(first_own == 0)
    def _():
        attend(tail)

    outs = []
    for sub in range(Q_SUBTILES):
        for head in range(N_HEADS):
            hs = slice(head * HEAD_DIM, (head + 1) * HEAD_DIM)
            inv = 1.0 / l_ref[sub, head]
            acc = acc_ref[sub, hs, :].reshape(head_groups, SUBLANES, tq) * inv[None]
            acc_ref[sub, hs, :] = acc.reshape(HEAD_DIM, tq)
        outs.append(acc_ref[sub].T.astype(jnp.bfloat16))
    o = jnp.concatenate(outs, axis=0)
    o_ref[0] = x + jnp.dot(o, wo_ref[...], preferred_element_type=jnp.float32)
    reset_softmax_state()


def _attn_call(x, g, wqt, wo, layer, k, vt, km, bias_tiles, rel_bias):
    B, T, D = x.shape
    nb = T // MOBA_BLOCK
    tq = MOBA_BLOCK
    n_q = Q_SUBTILES * tq
    return pl.pallas_call(
        _attn_kernel,
        grid=(B, T // n_q),
        in_specs=[
            pl.BlockSpec(memory_space=pltpu.SMEM),
            pl.BlockSpec((1, n_q, D), lambda b, i: (b, i, 0)),
            _resident((1, D)),
            _resident_layer(wqt.shape, layer),
            _resident_layer(wo.shape, layer),
            pl.BlockSpec((1, T, D), lambda b, i: (b, 0, 0)),
            pl.BlockSpec((1, D, T), lambda b, i: (b, 0, 0)),
            pl.BlockSpec((1, nb, D), lambda b, i: (b, 0, 0)),
            _resident(bias_tiles.shape),
        ],
        out_specs=pl.BlockSpec((1, n_q, D), lambda b, i: (b, i, 0)),
        out_shape=jax.ShapeDtypeStruct(x.shape, x.dtype),
        scratch_shapes=[
            pltpu.VMEM((Q_SUBTILES, N_HEADS, LANES, tq), jnp.bfloat16),
            pltpu.VMEM((nb - 1, N_HEADS, n_q), jnp.float32),
            pltpu.VMEM((Q_SUBTILES, N_HEADS, SUBLANES, tq), jnp.float32),
            pltpu.VMEM((Q_SUBTILES, N_HEADS, SUBLANES, tq), jnp.float32),
            pltpu.VMEM((Q_SUBTILES, D, tq), jnp.float32),
        ],
        compiler_params=pltpu.CompilerParams(
            dimension_semantics=("arbitrary", "arbitrary"),
            vmem_limit_bytes=VMEM_LIMIT),
        name="moba_attention",
    )(rel_bias, x, g.reshape(1, D), wqt, wo, k, vt, km, bias_tiles)


def kernel(x, norm_mixer, norm_ffn, pool_w, pool_scale, kv_norm, w_kv, w_q, w_o,
           rel_bias, w_gate_up, w_down, final_norm):
    bf = jnp.bfloat16
    assert x.shape[1] % MOBA_BLOCK == 0 and x.shape[2] == D_MODEL
    pool_w = pool_w.astype(bf)
    w_qt = jnp.swapaxes(w_q, 1, 2).astype(bf)
    w_o = w_o.astype(bf)
    wk = w_kv[:, :D_MODEL].astype(bf)
    wvt = w_kv[:, D_MODEL:].T.astype(bf)

    ffn_stacks = (w_gate_up, w_down)

    bias_tiles, wgu, wd = _bias_call(rel_bias, ffn_stacks, 0)

    k = vt = km = None
    for layer in range(DEPTH):
        last = layer == DEPTH - 1
        rider = dict(cast_stacks=() if last else ffn_stacks, cast_layer=layer + 1)
        if layer < N_A_LAYERS:
            x, *cast = _pool_ffn_call(x, norm_mixer[layer], pool_w, layer, pool_scale[layer],
                                      norm_ffn[layer], wgu, wd, **rider)
        else:
            j = layer - N_A_LAYERS
            if j == 0:
                k, vt, km = _kv_call(x, kv_norm, wk, wvt)
            x = _attn_call(x, norm_mixer[layer], w_qt, w_o, j, k, vt, km, bias_tiles, rel_bias)
            x, *cast = _ffn_call(x, norm_ffn[layer], wgu, wd, final_norm, final_norm=last,
                                 **rider)
        if cast:
            wgu, wd = cast
    return x
```

```python
import functools
import math

import jax
import jax.numpy as jnp
from jax import lax
from jax.experimental import pallas as pl
from jax.experimental.pallas import tpu as pltpu

D_MODEL = 1024
DEPTH = 4
N_A_LAYERS = DEPTH // 2
POOL_WINDOWS = (2, 4, 8, 16)
POOL_GROUP = D_MODEL // len(POOL_WINDOWS)
HEAD_DIM = 64
N_HEADS = D_MODEL // HEAD_DIM
MOBA_BLOCK = 256
MOBA_TOPK = 3
NUM_BUCKETS = 32
MAX_DISTANCE = 128
D_FF = -(-8 * D_MODEL // (3 * 256)) * 256
EPS = 1e-6

LANES = 128
SUBLANES = 8
PACKED_ROWS = 16
POOL_HALO = 16
assert all(w & (w - 1) == 0 and w <= POOL_HALO for w in POOL_WINDOWS)
HEADS_PER_LANE_TILE = LANES // HEAD_DIM
VMEM_LIMIT = 56 * 1024 * 1024
SCORE_LOOKAHEAD = 4
FAR_UNROLL = 2
Q_SUBTILES = 2

_NT = (((1,), (1,)), ((), ()))
_NEG_INF = float("-inf")
_MASKED = -1e30
LOG2E = math.log2(math.e)
ONES_ROWS = 16


def _rms(x, g):
    return x * lax.rsqrt(jnp.mean(x * x, axis=-1, keepdims=True) + EPS) * g


def _resident(shape):
    zeros = (0,) * len(shape)
    return pl.BlockSpec(shape, lambda *_: zeros, pipeline_mode=pl.Buffered(1))


def _resident_layer(stacked_shape, layer):
    index = (layer,) + (0,) * (len(stacked_shape) - 1)
    return pl.BlockSpec((None,) + tuple(stacked_shape[1:]), lambda *_: index,
                        pipeline_mode=pl.Buffered(1))


def _cast_rider(stacks, layer, n_steps, step_of):
    in_specs, out_specs, out_shape = [], [], []
    layers = layer if isinstance(layer, tuple) else (layer,) * len(stacks)
    for w, layer in zip(stacks, layers):
        n_slabs = n_steps if (w.shape[1] // n_steps) % PACKED_ROWS == 0 else n_steps // 2
        rows = w.shape[1] // n_slabs
        assert rows * n_slabs == w.shape[1] and rows % PACKED_ROWS == 0 and n_steps % n_slabs == 0
        in_specs.append(pl.BlockSpec(
            (None, rows, w.shape[2]),
            lambda *idx, n_slabs=n_slabs, layer=layer: (
                layer, step_of(*idx) * n_slabs // n_steps, 0)))
        out_specs.append(pl.BlockSpec(
            (rows, w.shape[2]),
            lambda *idx, n_slabs=n_slabs: (step_of(*idx) * n_slabs // n_steps, 0)))
        out_shape.append(jax.ShapeDtypeStruct(w.shape[1:], jnp.bfloat16))
    return in_specs, out_specs, out_shape


def _cast_slabs(in_refs, out_refs):
    for src, dst in zip(in_refs, out_refs):
        dst[...] = src[...].astype(jnp.bfloat16)


def _ffn_kernel(x_ref, g_ref, wgu_ref, wd_ref, fg_ref, *rest, fc, final_norm, n_cast):
    cast_in, (o_ref, *cast_out, act_ref) = rest[:n_cast], rest[n_cast:]
    _cast_slabs(cast_in, cast_out)
    x = x_ref[...]
    h = _rms(x, g_ref[...]).astype(jnp.bfloat16)
    for c in range(D_FF // fc):
        gate = jnp.dot(h, wgu_ref[:, c * fc:(c + 1) * fc],
                       preferred_element_type=jnp.float32)
        up = jnp.dot(h, wgu_ref[:, D_FF + c * fc:D_FF + (c + 1) * fc],
                     preferred_element_type=jnp.float32)
        act = gate * (1.0 / (1.0 + jnp.exp(-gate))) * up
        act_ref[:, c * fc:(c + 1) * fc] = act.astype(jnp.bfloat16)
    y = x + jnp.dot(act_ref[...], wd_ref[...], preferred_element_type=jnp.float32)
    if final_norm:
        y = _rms(y, fg_ref[...])
    o_ref[...] = y


def _ffn_call(x, g, wgu, wd, final_g, final_norm, cast_stacks=(), cast_layer=0,
              tm=1024, fc=256):
    B, T, D = x.shape
    xf = x.reshape(B * T, D)
    n_steps = B * T // tm
    cast_in, cast_out, cast_shape = _cast_rider(cast_stacks, cast_layer, n_steps, lambda i: i)
    out, *cast = pl.pallas_call(
        functools.partial(_ffn_kernel, fc=fc, final_norm=final_norm, n_cast=len(cast_stacks)),
        grid=(n_steps,),
        in_specs=[
            pl.BlockSpec((tm, D), lambda i: (i, 0)),
            _resident((1, D)),
            _resident(wgu.shape),
            _resident(wd.shape),
            _resident((1, D)),
        ] + cast_in,
        out_specs=[pl.BlockSpec((tm, D), lambda i: (i, 0))] + cast_out,
        out_shape=[jax.ShapeDtypeStruct(xf.shape, xf.dtype)] + cast_shape,
        scratch_shapes=[pltpu.VMEM((tm, D_FF), jnp.bfloat16)],
        compiler_params=pltpu.CompilerParams(
            dimension_semantics=("arbitrary",),
            vmem_limit_bytes=VMEM_LIMIT),
        name="swiglu_ffn",
    )(xf, g.reshape(1, D), wgu, wd, final_g.reshape(1, D), *cast_stacks)
    return [out.reshape(B, T, D)] + cast


def _pool_ffn_kernel(x_ref, xn_ref, gm_ref, pw_ref, ps_ref, gf_ref, wgu_ref, wd_ref, *rest,
                     tm, fc, n_cast, tiles_per_seq):
    cast_in, (o_ref, *cast_out, act_ref, x1_ref, hf_ref, tail_ref) = rest[:n_cast], rest[n_cast:]
    _cast_slabs(cast_in, cast_out)
    step = pl.program_id(0)
    half = tm // 2
    gm = gm_ref[...]
    gf = gf_ref[...]

    def pool_stages(src_ref, r0, dst0, seq_pos, halo_h, state):
        rows = slice(r0, r0 + half)

        def norm():
            x = src_ref[rows, :]
            state["x"] = x
            state["h"] = _rms(x, gm)
            state["hh"] = jnp.concatenate([halo_h(), state["h"]], axis=0)

        def group(gi, win):
            sl = slice(gi * POOL_GROUP, (gi + 1) * POOL_GROUP)
            wsum = state["hh"][:, sl]
            span = 1
            while span < win:
                wsum = wsum + pltpu.roll(wsum, span, axis=0)
                span *= 2
            wsum = wsum[POOL_HALO:]
            t = seq_pos + lax.broadcasted_iota(jnp.int32, (half, 1), 0)
            cnt = jnp.minimum(t + 1, win).astype(jnp.float32)
            d = wsum / cnt - state["h"][:, sl]
            y = jnp.dot(d.astype(jnp.bfloat16), pw_ref[gi], preferred_element_type=jnp.float32)
            x1_ref[dst0:dst0 + half, sl] = state["x"][:, sl] + y * ps_ref[:, sl]

        def ffn_input():
            x1 = x1_ref[dst0:dst0 + half, :]
            hf_ref[dst0:dst0 + half, :] = _rms(x1, gf).astype(jnp.bfloat16)

        return ([norm] + [functools.partial(group, gi, win)
                          for gi, win in enumerate(POOL_WINDOWS)] + [ffn_input])

    def first_half_stages(src_ref, seq_tile, halo_h, state):
        def keep_tail():
            tail_ref[...] = state["h"][half - POOL_HALO:]
        return pool_stages(src_ref, 0, 0, seq_tile * tm, halo_h, state) + [keep_tail]

    def ffn_half(r0, between):
        rows = slice(r0, r0 + half)
        x1 = x1_ref[rows, :]
        h = hf_ref[rows, :]
        for c in range(D_FF // fc):
            gate = jnp.dot(h, wgu_ref[:, c * fc:(c + 1) * fc],
                           preferred_element_type=jnp.float32)
            up = jnp.dot(h, wgu_ref[:, D_FF + c * fc:D_FF + (c + 1) * fc],
                         preferred_element_type=jnp.float32)
            act = gate * (1.0 / (1.0 + jnp.exp(-gate))) * up
            act_ref[rows, c * fc:(c + 1) * fc] = act.astype(jnp.bfloat16)
            if c < len(between):
                between[c]()
        o_ref[rows, :] = x1 + jnp.dot(act_ref[rows, :], wd_ref[...],
                                      preferred_element_type=jnp.float32)

    seq_tile = step % tiles_per_seq
    next_seq_tile = (step + 1) % tiles_per_seq
    no_history = lambda: jnp.zeros((POOL_HALO, x_ref.shape[1]), jnp.float32)

    @pl.when(step == 0)
    def _():
        for stage in first_half_stages(x_ref, 0, no_history, {}):
            stage()

    second, upcoming = {}, {}
    ffn_half(0, pool_stages(x_ref, half, half, seq_tile * tm + half,
                            lambda: tail_ref[...], second))
    ffn_half(half, first_half_stages(
        xn_ref, next_seq_tile,
        lambda: jnp.where(next_seq_tile > 0, second["h"][half - POOL_HALO:], 0.0), upcoming))


def _pool_ffn_call(x, g_mixer, pool_w, layer, pool_scale, g_ffn, wgu, wd,
                   cast_stacks=(), cast_layer=0, tm=512, fc=256):
    B, T, D = x.shape
    xf = x.reshape(B * T, D)
    n_steps = B * T // tm
    half = tm // 2
    last_half = B * T // half - 1
    cast_in, cast_out, cast_shape = _cast_rider(cast_stacks, cast_layer, n_steps, lambda i: i)
    out, *cast = pl.pallas_call(
        functools.partial(_pool_ffn_kernel, tm=tm, fc=fc, n_cast=len(cast_stacks),
                          tiles_per_seq=T // tm),
        grid=(n_steps,),
        in_specs=[
            pl.BlockSpec((tm, D), lambda i: (i, 0)),
            pl.BlockSpec((half, D), lambda i: (jnp.minimum(2 * i + 2, last_half), 0)),
            _resident((1, D)),
            _resident_layer(pool_w.shape, layer),
            _resident((1, D)),
            _resident((1, D)),
            _resident(wgu.shape),
            _resident(wd.shape),
        ] + cast_in,
        out_specs=[pl.BlockSpec((tm, D), lambda i: (i, 0))] + cast_out,
        out_shape=[jax.ShapeDtypeStruct(xf.shape, xf.dtype)] + cast_shape,
        scratch_shapes=[
            pltpu.VMEM((tm, D_FF), jnp.bfloat16),
            pltpu.VMEM((tm, D), jnp.float32),
            pltpu.VMEM((tm, D), jnp.bfloat16),
            pltpu.VMEM((POOL_HALO, D), jnp.float32),
        ],
        compiler_params=pltpu.CompilerParams(
            dimension_semantics=("arbitrary",),
            vmem_limit_bytes=VMEM_LIMIT),
        name="pool_layer",
    )(xf, xf, g_mixer.reshape(1, D), pool_w, pool_scale.reshape(1, D), g_ffn.reshape(1, D),
      wgu, wd, *cast_stacks)
    return [out.reshape(B, T, D)] + cast


def _kv_kernel(x_ref, g_ref, wk_ref, wvt_ref, k_ref, vt_ref, km_ref, *, blocks_per_tile):
    j = pl.program_id(1)
    h = _rms(x_ref[0], g_ref[...]).astype(jnp.bfloat16)
    k = jnp.dot(h, wk_ref[...], preferred_element_type=jnp.float32)
    k_ref[0] = k.astype(jnp.bfloat16)
    vt = lax.dot_general(wvt_ref[...], h, _NT, preferred_element_type=jnp.float32)
    vt_ref[0] = vt.astype(jnp.bfloat16)
    for i in range(blocks_per_tile):
        km_ref[0, pl.ds(j * blocks_per_tile + i, 1), :] = jnp.mean(
            k[i * MOBA_BLOCK:(i + 1) * MOBA_BLOCK], axis=0, keepdims=True)


def _kv_call(x, g, wk, wvt, tm=1024):
    B, T, D = x.shape
    nb = T // MOBA_BLOCK
    return pl.pallas_call(
        functools.partial(_kv_kernel, blocks_per_tile=tm // MOBA_BLOCK),
        grid=(B, T // tm),
        in_specs=[
            pl.BlockSpec((1, tm, D), lambda b, j: (b, j, 0)),
            _resident((1, D)),
            _resident(wk.shape),
            _resident(wvt.shape),
        ],
        out_specs=[
            pl.BlockSpec((1, tm, D), lambda b, j: (b, j, 0)),
            pl.BlockSpec((1, D, tm), lambda b, j: (b, 0, j)),
            pl.BlockSpec((1, nb, D), lambda b, j: (b, 0, 0)),
        ],
        out_shape=[
            jax.ShapeDtypeStruct((B, T, D), jnp.bfloat16),
            jax.ShapeDtypeStruct((B, D, T), jnp.bfloat16),
            jax.ShapeDtypeStruct((B, nb, D), jnp.float32),
        ],
        compiler_params=pltpu.CompilerParams(
            dimension_semantics=("arbitrary", "arbitrary"),
            vmem_limit_bytes=VMEM_LIMIT),
        name="shared_kv",
    )(x, g.reshape(1, D), wk, wvt)


def _rel_bucket(n):
    max_exact = NUM_BUCKETS // 2
    nf = jnp.maximum(n, max_exact).astype(jnp.float32)
    large = max_exact + (jnp.log(nf / max_exact) / math.log(MAX_DISTANCE / max_exact)
                         * (NUM_BUCKETS - max_exact)).astype(jnp.int32)
    large = jnp.minimum(large, NUM_BUCKETS - 1)
    return jnp.where(n < max_exact, n, large)


def _bias_kernel(rb_ref, *rest, n_cast):
    cast_in, (o_ref, *cast_out) = rest[:n_cast], rest[n_cast:]
    _cast_slabs(cast_in, cast_out)
    hd = pl.program_id(0)
    quad = MOBA_BLOCK // 2
    kl = lax.broadcasted_iota(jnp.int32, (quad, quad), 0)
    ql = lax.broadcasted_iota(jnp.int32, (quad, quad), 1)
    for ti, offset in enumerate((0, MOBA_BLOCK)):
        for rk in range(2):
            for cq in range(2):
                d0 = offset + cq * quad - rk * quad
                d_min, d_max = d0 - (quad - 1), d0 + (quad - 1)
                if d_max < 0:
                    tile = jnp.full((quad, quad), _MASKED, jnp.float32)
                elif d_min >= MAX_DISTANCE:
                    tile = jnp.full((quad, quad), rb_ref[NUM_BUCKETS - 1, hd], jnp.float32)
                else:
                    dist = d0 + ql - kl
                    bucket = _rel_bucket(jnp.maximum(dist, 0))
                    tile = jnp.zeros((quad, quad), jnp.float32)
                    for bi in range(NUM_BUCKETS):
                        tile = jnp.where(bucket == bi, rb_ref[bi, hd], tile)
                    if d_min < 0:
                        tile = jnp.where(dist >= 0, tile, _MASKED)
                o_ref[0, ti, rk * quad:(rk + 1) * quad, cq * quad:(cq + 1) * quad] = tile * LOG2E


def _bias_call(rel_bias, cast_stacks=(), cast_layer=0):
    cast_in, cast_out, cast_shape = _cast_rider(cast_stacks, cast_layer, N_HEADS, lambda h: h)
    return pl.pallas_call(
        functools.partial(_bias_kernel, n_cast=len(cast_stacks)),
        grid=(N_HEADS,),
        in_specs=[pl.BlockSpec(memory_space=pltpu.SMEM)] + cast_in,
        out_specs=[pl.BlockSpec((1, 2, MOBA_BLOCK, MOBA_BLOCK), lambda h: (h, 0, 0, 0))] + cast_out,
        out_shape=[jax.ShapeDtypeStruct((N_HEADS, 2, MOBA_BLOCK, MOBA_BLOCK), jnp.float32)]
        + cast_shape,
        compiler_params=pltpu.CompilerParams(dimension_semantics=("arbitrary",)),
        name="rel_bias_tiles",
    )(rel_bias, *cast_stacks)


def _sublane_all(op, x):
    shift = SUBLANES // 2
    while shift:
        x = op(x, pltpu.roll(x, shift, axis=0))
        shift //= 2
    return x


def _attn_kernel(rb_ref, x_ref, g_ref, wqt_ref, wo_ref, k_ref, vt_ref, km_ref, bias_ref,
                 o_ref, qh_ref, mask_ref, m_ref, l_ref, acc_ref):
    first_own = pl.program_id(1) * Q_SUBTILES
    nb = km_ref.shape[1]
    tq = MOBA_BLOCK
    D = x_ref.shape[2]
    row_groups = MOBA_BLOCK // SUBLANES
    head_groups = HEAD_DIM // SUBLANES

    def reset_softmax_state():
        m_ref[...] = jnp.full(m_ref.shape, _MASKED, jnp.float32)
        l_ref[...] = jnp.zeros(l_ref.shape, jnp.float32)
        acc_ref[...] = jnp.zeros(acc_ref.shape, jnp.float32)

    @pl.when((pl.program_id(0) == 0) & (pl.program_id(1) == 0))
    def _():
        reset_softmax_state()

    x = x_ref[0]
    h = _rms(x, g_ref[...]).astype(jnp.bfloat16)
    qf = lax.dot_general(wqt_ref[...], h, _NT, preferred_element_type=jnp.float32)
    q_gate = (qf * HEAD_DIM ** -0.5).astype(jnp.bfloat16)
    q = (qf * (HEAD_DIM ** -0.5 * LOG2E)).astype(jnp.bfloat16)

    feat = lax.broadcasted_iota(jnp.int32, (LANES, 1), 0)
    for sub in range(Q_SUBTILES):
        for head in range(N_HEADS):
            p, hh = divmod(head, HEADS_PER_LANE_TILE)
            qp = q[p * LANES:(p + 1) * LANES, sub * tq:(sub + 1) * tq]
            in_head = (feat >= hh * HEAD_DIM) & (feat < (hh + 1) * HEAD_DIM)
            qh_ref[sub, head] = jnp.where(in_head, qp, jnp.zeros_like(qp))

    n_q = Q_SUBTILES * tq
    n_cand = nb - 1
    own_of_query = first_own + lax.broadcasted_iota(jnp.int32, (N_HEADS, n_q), 1) // tq
    past = [own_of_query > j for j in range(n_cand)]
    some_query_ranks = first_own + Q_SUBTILES - 1 > MOBA_TOPK

    @pl.when(some_query_ranks)
    def _():
        km = jnp.broadcast_to(km_ref[0, :n_cand][:, None, :], (n_cand, N_HEADS, D))
        km = km.reshape(n_cand * N_HEADS, D)
        row_head = lax.broadcasted_iota(jnp.int32, km.shape, 0) % N_HEADS
        col_head = lax.broadcasted_iota(jnp.int32, km.shape, 1) // HEAD_DIM
        km_bd = jnp.where(row_head == col_head, km, 0.0).astype(jnp.bfloat16)
        gate = jnp.dot(km_bd, q_gate, preferred_element_type=jnp.float32)
        gates = [jnp.where(past[j], gate[j * N_HEADS:(j + 1) * N_HEADS], _NEG_INF)
                 for j in range(n_cand)]
        rank = [jnp.zeros((N_HEADS, n_q), jnp.int32) for _ in range(n_cand)]
        for lo in range(n_cand):
            for hi in range(lo + 1, n_cand):
                lo_wins = jnp.where(gates[lo] >= gates[hi], 1, 0)
                rank[hi] = rank[hi] + lo_wins
                rank[lo] = rank[lo] + (1 - lo_wins)
        for j in range(n_cand):
            keep = jnp.where(past[j], 0.0, _MASKED)
            mask_ref[j] = jnp.where(rank[j] < MOBA_TOPK, keep, _MASKED)

    @pl.when(jnp.logical_not(some_query_ranks))
    def _():
        for j in range(n_cand):
            mask_ref[j] = jnp.where(past[j], 0.0, _MASKED)


    def attend(blocks):
        units = []
        for j, kinds in blocks:
            keys = pl.ds(pl.multiple_of(j * MOBA_BLOCK, MOBA_BLOCK), MOBA_BLOCK)
            units += [(j, keys, kind, sub, head) for head in range(N_HEADS)
                      for sub, kind in enumerate(kinds) if kind is not None]

        def scores(unit):
            _, keys, _, sub, head = unit
            p = head // HEADS_PER_LANE_TILE
            kj = k_ref[0, keys, p * LANES:(p + 1) * LANES]
            return jnp.dot(kj, qh_ref[sub, head], preferred_element_type=jnp.float32)

        def softmax(unit, s):
            j, _, kind, sub, head = unit
            if kind != "far":
                tile = bias_ref[head, 0 if kind == "own" else 1]
                s = (s.reshape(row_groups, SUBLANES, tq)
                     + tile.reshape(row_groups, SUBLANES, tq)).reshape(MOBA_BLOCK, tq)
            sb = s.astype(jnp.bfloat16).reshape(MOBA_BLOCK // PACKED_ROWS, PACKED_ROWS, tq)
            bm = jnp.max(sb, axis=0).astype(jnp.float32)
            block_max = _sublane_all(jnp.maximum, jnp.maximum(bm[:SUBLANES], bm[SUBLANES:]))
            m_old = m_ref[sub, head]
            bias = rb_ref[NUM_BUCKETS - 1, head] * LOG2E if kind == "far" else 0.0
            top = block_max + bias
            if kind != "own":
                selected = jnp.broadcast_to(
                    mask_ref[j, head:head + 1, sub * tq:(sub + 1) * tq], (SUBLANES, tq)) == 0.0
                top = jnp.where(selected, top, _MASKED)
            shift = jnp.maximum(jnp.maximum(m_old, top) - bias, block_max)
            shift = shift.astype(jnp.bfloat16).astype(jnp.float32)
            shift_b = jnp.concatenate([shift, shift], axis=0).astype(jnp.bfloat16)
            pj = jnp.exp2(sb - shift_b[None])
            ref = shift + bias
            if kind != "own":
                m_new = jnp.where(selected, jnp.maximum(m_old, ref), m_old)
                scale_new = jnp.where(selected, jnp.exp2(ref - m_new), 0.0)
            else:
                m_new = jnp.maximum(m_old, ref)
                scale_new = jnp.exp2(ref - m_new)
            scale_old = jnp.exp2(m_old - m_new)
            m_ref[sub, head] = m_new
            return pj.reshape(MOBA_BLOCK, tq), (scale_old, scale_new)

        def accumulate(unit, pv, scales):
            _, _, _, sub, head = unit
            scale_old, scale_new = scales
            hs = slice(head * HEAD_DIM, (head + 1) * HEAD_DIM)
            l_ref[sub, head] = (l_ref[sub, head] * scale_old
                                + pv[HEAD_DIM:HEAD_DIM + SUBLANES] * scale_new)
            acc = acc_ref[sub, hs, :].reshape(head_groups, SUBLANES, tq)
            acc = (acc * scale_old[None]
                   + pv[:HEAD_DIM].reshape(head_groups, SUBLANES, tq) * scale_new[None])
            acc_ref[sub, hs, :] = acc.reshape(HEAD_DIM, tq)

        ones = jnp.ones((ONES_ROWS, MOBA_BLOCK), jnp.bfloat16)

        def weighted_values(unit, pj):
            _, keys, _, _, head = unit
            vj = vt_ref[0, head * HEAD_DIM:(head + 1) * HEAD_DIM, keys]
            return jnp.dot(jnp.concatenate([vj, ones], axis=0), pj,
                           preferred_element_type=jnp.float32)

        n = len(units)
        pending_scores = {i: scores(units[i]) for i in range(SCORE_LOOKAHEAD)}
        probs = {}
        products = {}
        for step in range(n + 2):
            if step + SCORE_LOOKAHEAD < n:
                pending_scores[step + SCORE_LOOKAHEAD] = scores(units[step + SCORE_LOOKAHEAD])
            if 0 <= step - 1 < n:
                pj, alpha = probs.pop(step - 1)
                products[step - 1] = (weighted_values(units[step - 1], pj), alpha)
            if step < n:
                probs[step] = softmax(units[step], pending_scores.pop(step))
            if 0 <= step - 2 < n:
                accumulate(units[step - 2], *products.pop(step - 2))

    assert Q_SUBTILES == 2 and FAR_UNROLL == 2
    all_far = ("far", "far")
    tail = [(first_own, ("own", "prev")), (first_own + 1, (None, "own"))]

    def far_pair(t, carry):
        attend([(FAR_UNROLL * t, all_far), (FAR_UNROLL * t + 1, all_far)])
        return carry

    lax.fori_loop(0, jnp.maximum(first_own - 1, 0) // FAR_UNROLL, far_pair, 0)

    @pl.when(first_own > 0)
    def _():
        attend([(first_own - 2, all_far), (first_own - 1, ("prev", "far"))] + tail)

    @pl.when(first_own == 0)
    def _():
        attend(tail)

    outs = []
    for sub in range(Q_SUBTILES):
        for head in range(N_HEADS):
            hs = slice(head * HEAD_DIM, (head + 1) * HEAD_DIM)
            inv = 1.0 / l_ref[sub, head]
            acc = acc_ref[sub, hs, :].reshape(head_groups, SUBLANES, tq) * inv[None]
            acc_ref[sub, hs, :] = acc.reshape(HEAD_DIM, tq)
        outs.append(acc_ref[sub].T.astype(jnp.bfloat16))
    o = jnp.concatenate(outs, axis=0)
    o_ref[0] = x + jnp.dot(o, wo_ref[...], preferred_element_type=jnp.float32)
    reset_softmax_state()


def _attn_call(x, g, wqt, wo, layer, k, vt, km, bias_tiles, rel_bias):
    B, T, D = x.shape
    nb = T // MOBA_BLOCK
    tq = MOBA_BLOCK
    n_q = Q_SUBTILES * tq
    return pl.pallas_call(
        _attn_kernel,
        grid=(B, T // n_q),
        in_specs=[
            pl.BlockSpec(memory_space=pltpu.SMEM),
            pl.BlockSpec((1, n_q, D), lambda b, i: (b, i, 0)),
            _resident((1, D)),
            _resident_layer(wqt.shape, layer),
            _resident_layer(wo.shape, layer),
            pl.BlockSpec((1, T, D), lambda b, i: (b, 0, 0)),
            pl.BlockSpec((1, D, T), lambda b, i: (b, 0, 0)),
            pl.BlockSpec((1, nb, D), lambda b, i: (b, 0, 0)),
            _resident(bias_tiles.shape),
        ],
        out_specs=pl.BlockSpec((1, n_q, D), lambda b, i: (b, i, 0)),
        out_shape=jax.ShapeDtypeStruct(x.shape, x.dtype),
        scratch_shapes=[
            pltpu.VMEM((Q_SUBTILES, N_HEADS, LANES, tq), jnp.bfloat16),
            pltpu.VMEM((nb - 1, N_HEADS, n_q), jnp.float32),
            pltpu.VMEM((Q_SUBTILES, N_HEADS, SUBLANES, tq), jnp.float32),
            pltpu.VMEM((Q_SUBTILES, N_HEADS, SUBLANES, tq), jnp.float32),
            pltpu.VMEM((Q_SUBTILES, D, tq), jnp.float32),
        ],
        compiler_params=pltpu.CompilerParams(
            dimension_semantics=("arbitrary", "arbitrary"),
            vmem_limit_bytes=VMEM_LIMIT),
        name="moba_attention",
    )(rel_bias, x, g.reshape(1, D), wqt, wo, k, vt, km, bias_tiles)


def kernel(x, norm_mixer, norm_ffn, pool_w, pool_scale, kv_norm, w_kv, w_q, w_o,
           rel_bias, w_gate_up, w_down, final_norm):
    bf = jnp.bfloat16
    assert x.shape[1] % MOBA_BLOCK == 0 and x.shape[2] == D_MODEL
    w_qt = jnp.swapaxes(w_q, 1, 2).astype(bf)
    wk = w_kv[:, :D_MODEL].astype(bf)
    wvt = w_kv[:, D_MODEL:].T.astype(bf)

    ffn_stacks = (w_gate_up, w_down)
    pool_w_flat = pool_w.reshape(1, -1, pool_w.shape[-1])
    w_o_flat = w_o.reshape(1, -1, w_o.shape[-1])

    bias_tiles, wgu, wd, pool_w_bf = _bias_call(rel_bias, ffn_stacks + (pool_w_flat,), 0)
    pool_w = pool_w_bf.reshape(pool_w.shape)

    k = vt = km = None
    for layer in range(DEPTH):
        last = layer == DEPTH - 1
        extra = (w_o_flat,) if layer == 0 else ()
        rider = dict(cast_stacks=(() if last else ffn_stacks) + extra,
                     cast_layer=(layer + 1,) * len(ffn_stacks) + (0,) * len(extra))
        if layer < N_A_LAYERS:
            x, *cast = _pool_ffn_call(x, norm_mixer[layer], pool_w, layer, pool_scale[layer],
                                      norm_ffn[layer], wgu, wd, **rider)
        else:
            j = layer - N_A_LAYERS
            if j == 0:
                k, vt, km = _kv_call(x, kv_norm, wk, wvt)
            x = _attn_call(x, norm_mixer[layer], w_qt, w_o, j, k, vt, km, bias_tiles, rel_bias)
            x, *cast = _ffn_call(x, norm_ffn[layer], wgu, wd, final_norm, final_norm=last,
                                 **rider)
        if cast:
            wgu, wd, *more = cast
            if more:
                w_o = more[0].reshape(w_o.shape)
    return x
```

```python
import functools
import math

import jax
import jax.numpy as jnp
from jax import lax
from jax.experimental import pallas as pl
from jax.experimental.pallas import tpu as pltpu

D_MODEL = 1024
DEPTH = 4
N_A_LAYERS = DEPTH // 2
POOL_WINDOWS = (2, 4, 8, 16)
POOL_GROUP = D_MODEL // len(POOL_WINDOWS)
HEAD_DIM = 64
N_HEADS = D_MODEL // HEAD_DIM
MOBA_BLOCK = 256
MOBA_TOPK = 3
NUM_BUCKETS = 32
MAX_DISTANCE = 128
D_FF = -(-8 * D_MODEL // (3 * 256)) * 256
EPS = 1e-6

LANES = 128
SUBLANES = 8
PACKED_ROWS = 16
POOL_HALO = 16
assert all(w & (w - 1) == 0 and w <= POOL_HALO for w in POOL_WINDOWS)
HEADS_PER_LANE_TILE = LANES // HEAD_DIM
VMEM_LIMIT = 56 * 1024 * 1024
SCORE_LOOKAHEAD = 4
FAR_UNROLL = 2
Q_SUBTILES = 2

_NT = (((1,), (1,)), ((), ()))
_NEG_INF = float("-inf")
_MASKED = -1e30
LOG2E = math.log2(math.e)
ONES_ROWS = 16


def _rms(x, g):
    return x * lax.rsqrt(jnp.mean(x * x, axis=-1, keepdims=True) + EPS) * g


def _resident(shape):
    zeros = (0,) * len(shape)
    return pl.BlockSpec(shape, lambda *_: zeros, pipeline_mode=pl.Buffered(1))


def _resident_layer(stacked_shape, layer):
    index = (layer,) + (0,) * (len(stacked_shape) - 1)
    return pl.BlockSpec((None,) + tuple(stacked_shape[1:]), lambda *_: index,
                        pipeline_mode=pl.Buffered(1))


def _cast_rider(stacks, layer, n_steps, step_of):
    in_specs, out_specs, out_shape = [], [], []
    layers = layer if isinstance(layer, tuple) else (layer,) * len(stacks)
    for w, layer in zip(stacks, layers):
        n_slabs = n_steps if (w.shape[1] // n_steps) % PACKED_ROWS == 0 else n_steps // 2
        rows = w.shape[1] // n_slabs
        assert rows * n_slabs == w.shape[1] and rows % PACKED_ROWS == 0 and n_steps % n_slabs == 0
        in_specs.append(pl.BlockSpec(
            (None, rows, w.shape[2]),
            lambda *idx, n_slabs=n_slabs, layer=layer: (
                layer, step_of(*idx) * n_slabs // n_steps, 0)))
        out_specs.append(pl.BlockSpec(
            (rows, w.shape[2]),
            lambda *idx, n_slabs=n_slabs: (step_of(*idx) * n_slabs // n_steps, 0)))
        out_shape.append(jax.ShapeDtypeStruct(w.shape[1:], jnp.bfloat16))
    return in_specs, out_specs, out_shape


def _cast_slabs(in_refs, out_refs):
    for src, dst in zip(in_refs, out_refs):
        dst[...] = src[...].astype(jnp.bfloat16)


def _ffn_kernel(x_ref, g_ref, wgu_ref, wd_ref, fg_ref, *rest, fc, final_norm, n_cast):
    cast_in, (o_ref, *cast_out, act_ref) = rest[:n_cast], rest[n_cast:]
    _cast_slabs(cast_in, cast_out)
    x = x_ref[...]
    h = _rms(x, g_ref[...]).astype(jnp.bfloat16)
    for c in range(D_FF // fc):
        gate = jnp.dot(h, wgu_ref[:, c * fc:(c + 1) * fc],
                       preferred_element_type=jnp.float32)
        up = jnp.dot(h, wgu_ref[:, D_FF + c * fc:D_FF + (c + 1) * fc],
                     preferred_element_type=jnp.float32)
        act = gate * (1.0 / (1.0 + jnp.exp(-gate))) * up
        act_ref[:, c * fc:(c + 1) * fc] = act.astype(jnp.bfloat16)
    y = x + jnp.dot(act_ref[...], wd_ref[...], preferred_element_type=jnp.float32)
    if final_norm:
        y = _rms(y, fg_ref[...])
    o_ref[...] = y


def _ffn_call(x, g, wgu, wd, final_g, final_norm, cast_stacks=(), cast_layer=0,
              tm=1024, fc=256):
    B, T, D = x.shape
    xf = x.reshape(B * T, D)
    n_steps = B * T // tm
    cast_in, cast_out, cast_shape = _cast_rider(cast_stacks, cast_layer, n_steps, lambda i: i)
    out, *cast = pl.pallas_call(
        functools.partial(_ffn_kernel, fc=fc, final_norm=final_norm, n_cast=len(cast_stacks)),
        grid=(n_steps,),
        in_specs=[
            pl.BlockSpec((tm, D), lambda i: (i, 0)),
            _resident((1, D)),
            _resident(wgu.shape),
            _resident(wd.shape),
            _resident((1, D)),
        ] + cast_in,
        out_specs=[pl.BlockSpec((tm, D), lambda i: (i, 0))] + cast_out,
        out_shape=[jax.ShapeDtypeStruct(xf.shape, xf.dtype)] + cast_shape,
        scratch_shapes=[pltpu.VMEM((tm, D_FF), jnp.bfloat16)],
        compiler_params=pltpu.CompilerParams(
            dimension_semantics=("arbitrary",),
            vmem_limit_bytes=VMEM_LIMIT),
        name="swiglu_ffn",
    )(xf, g.reshape(1, D), wgu, wd, final_g.reshape(1, D), *cast_stacks)
    return [out.reshape(B, T, D)] + cast


def _pool_ffn_kernel(x_ref, xn_ref, gm_ref, pw_ref, ps_ref, gf_ref, wgu_ref, wd_ref, *rest,
                     tm, fc, n_cast, tiles_per_seq):
    cast_in, (o_ref, *cast_out, act_ref, x1_ref, hf_ref, tail_ref) = rest[:n_cast], rest[n_cast:]
    _cast_slabs(cast_in, cast_out)
    step = pl.program_id(0)
    half = tm // 2
    gm = gm_ref[...]
    gf = gf_ref[...]

    def pool_stages(src_ref, r0, dst0, seq_pos, halo_h, state):
        rows = slice(r0, r0 + half)

        def norm():
            x = src_ref[rows, :]
            state["x"] = x
            state["h"] = _rms(x, gm)
            state["hh"] = jnp.concatenate([halo_h(), state["h"]], axis=0)

        def group(gi, win):
            sl = slice(gi * POOL_GROUP, (gi + 1) * POOL_GROUP)
            wsum = state["hh"][:, sl]
            span = 1
            while span < win:
                wsum = wsum + pltpu.roll(wsum, span, axis=0)
                span *= 2
            wsum = wsum[POOL_HALO:]
            t = seq_pos + lax.broadcasted_iota(jnp.int32, (half, 1), 0)
            cnt = jnp.minimum(t + 1, win).astype(jnp.float32)
            d = wsum / cnt - state["h"][:, sl]
            y = jnp.dot(d.astype(jnp.bfloat16), pw_ref[gi], preferred_element_type=jnp.float32)
            x1_ref[dst0:dst0 + half, sl] = state["x"][:, sl] + y * ps_ref[:, sl]

        def ffn_input():
            x1 = x1_ref[dst0:dst0 + half, :]
            hf_ref[dst0:dst0 + half, :] = _rms(x1, gf).astype(jnp.bfloat16)

        return ([norm] + [functools.partial(group, gi, win)
                          for gi, win in enumerate(POOL_WINDOWS)] + [ffn_input])

    def first_half_stages(src_ref, seq_tile, halo_h, state):
        def keep_tail():
            tail_ref[...] = state["h"][half - POOL_HALO:]
        return pool_stages(src_ref, 0, 0, seq_tile * tm, halo_h, state) + [keep_tail]

    def ffn_half(r0, between):
        rows = slice(r0, r0 + half)
        x1 = x1_ref[rows, :]
        h = hf_ref[rows, :]
        for c in range(D_FF // fc):
            gate = jnp.dot(h, wgu_ref[:, c * fc:(c + 1) * fc],
                           preferred_element_type=jnp.float32)
            up = jnp.dot(h, wgu_ref[:, D_FF + c * fc:D_FF + (c + 1) * fc],
                         preferred_element_type=jnp.float32)
            act = gate * (1.0 / (1.0 + jnp.exp(-gate))) * up
            act_ref[rows, c * fc:(c + 1) * fc] = act.astype(jnp.bfloat16)
            if c < len(between):
                between[c]()
        o_ref[rows, :] = x1 + jnp.dot(act_ref[rows, :], wd_ref[...],
                                      preferred_element_type=jnp.float32)

    seq_tile = step % tiles_per_seq
    next_seq_tile = (step + 1) % tiles_per_seq
    no_history = lambda: jnp.zeros((POOL_HALO, x_ref.shape[1]), jnp.float32)

    @pl.when(step == 0)
    def _():
        for stage in first_half_stages(x_ref, 0, no_history, {}):
            stage()

    second, upcoming = {}, {}
    ffn_half(0, pool_stages(x_ref, half, half, seq_tile * tm + half,
                            lambda: tail_ref[...], second))
    ffn_half(half, first_half_stages(
        xn_ref, next_seq_tile,
        lambda: jnp.where(next_seq_tile > 0, second["h"][half - POOL_HALO:], 0.0), upcoming))


def _pool_ffn_call(x, g_mixer, pool_w, layer, pool_scale, g_ffn, wgu, wd,
                   cast_stacks=(), cast_layer=0, tm=512, fc=256):
    B, T, D = x.shape
    xf = x.reshape(B * T, D)
    n_steps = B * T // tm
    half = tm // 2
    last_half = B * T // half - 1
    cast_in, cast_out, cast_shape = _cast_rider(cast_stacks, cast_layer, n_steps, lambda i: i)
    out, *cast = pl.pallas_call(
        functools.partial(_pool_ffn_kernel, tm=tm, fc=fc, n_cast=len(cast_stacks),
                          tiles_per_seq=T // tm),
        grid=(n_steps,),
        in_specs=[
            pl.BlockSpec((tm, D), lambda i: (i, 0)),
            pl.BlockSpec((half, D), lambda i: (jnp.minimum(2 * i + 2, last_half), 0)),
            _resident((1, D)),
            _resident_layer(pool_w.shape, layer),
            _resident((1, D)),
            _resident((1, D)),
            _resident(wgu.shape),
            _resident(wd.shape),
        ] + cast_in,
        out_specs=[pl.BlockSpec((tm, D), lambda i: (i, 0))] + cast_out,
        out_shape=[jax.ShapeDtypeStruct(xf.shape, xf.dtype)] + cast_shape,
        scratch_shapes=[
            pltpu.VMEM((tm, D_FF), jnp.bfloat16),
            pltpu.VMEM((tm, D), jnp.float32),
            pltpu.VMEM((tm, D), jnp.bfloat16),
            pltpu.VMEM((POOL_HALO, D), jnp.float32),
        ],
        compiler_params=pltpu.CompilerParams(
            dimension_semantics=("arbitrary",),
            vmem_limit_bytes=VMEM_LIMIT),
        name="pool_layer",
    )(xf, xf, g_mixer.reshape(1, D), pool_w, pool_scale.reshape(1, D), g_ffn.reshape(1, D),
      wgu, wd, *cast_stacks)
    return [out.reshape(B, T, D)] + cast


def _kv_kernel(x_ref, g_ref, wk_ref, wvt_ref, k_ref, vt_ref, km_ref, *, blocks_per_tile):
    j = pl.program_id(1)
    h = _rms(x_ref[0], g_ref[...]).astype(jnp.bfloat16)
    k = jnp.dot(h, wk_ref[...], preferred_element_type=jnp.float32)
    k_ref[0] = k.astype(jnp.bfloat16)
    vt = lax.dot_general(wvt_ref[...], h, _NT, preferred_element_type=jnp.float32)
    vt_ref[0] = vt.astype(jnp.bfloat16)
    for i in range(blocks_per_tile):
        km_ref[0, pl.ds(j * blocks_per_tile + i, 1), :] = jnp.mean(
            k[i * MOBA_BLOCK:(i + 1) * MOBA_BLOCK], axis=0, keepdims=True)


def _kv_call(x, g, wk, wvt, tm=1024):
    B, T, D = x.shape
    nb = T // MOBA_BLOCK
    return pl.pallas_call(
        functools.partial(_kv_kernel, blocks_per_tile=tm // MOBA_BLOCK),
        grid=(B, T // tm),
        in_specs=[
            pl.BlockSpec((1, tm, D), lambda b, j: (b, j, 0)),
            _resident((1, D)),
            _resident(wk.shape),
            _resident(wvt.shape),
        ],
        out_specs=[
            pl.BlockSpec((1, tm, D), lambda b, j: (b, j, 0)),
            pl.BlockSpec((1, D, tm), lambda b, j: (b, 0, j)),
            pl.BlockSpec((1, nb, D), lambda b, j: (b, 0, 0)),
        ],
        out_shape=[
            jax.ShapeDtypeStruct((B, T, D), jnp.bfloat16),
            jax.ShapeDtypeStruct((B, D, T), jnp.bfloat16),
            jax.ShapeDtypeStruct((B, nb, D), jnp.float32),
        ],
        compiler_params=pltpu.CompilerParams(
            dimension_semantics=("arbitrary", "arbitrary"),
            vmem_limit_bytes=VMEM_LIMIT),
        name="shared_kv",
    )(x, g.reshape(1, D), wk, wvt)


def _rel_bucket(n):
    max_exact = NUM_BUCKETS // 2
    nf = jnp.maximum(n, max_exact).astype(jnp.float32)
    large = max_exact + (jnp.log(nf / max_exact) / math.log(MAX_DISTANCE / max_exact)
                         * (NUM_BUCKETS - max_exact)).astype(jnp.int32)
    large = jnp.minimum(large, NUM_BUCKETS - 1)
    return jnp.where(n < max_exact, n, large)


def _bias_kernel(rb_ref, *rest, n_cast):
    cast_in, (o_ref, *cast_out) = rest[:n_cast], rest[n_cast:]
    _cast_slabs(cast_in, cast_out)
    hd = pl.program_id(0)
    quad = MOBA_BLOCK // 2
    kl = lax.broadcasted_iota(jnp.int32, (quad, quad), 0)
    ql = lax.broadcasted_iota(jnp.int32, (quad, quad), 1)
    for ti, offset in enumerate((0, MOBA_BLOCK)):
        for rk in range(2):
            for cq in range(2):
                d0 = offset + cq * quad - rk * quad
                d_min, d_max = d0 - (quad - 1), d0 + (quad - 1)
                if d_max < 0:
                    tile = jnp.full((quad, quad), _MASKED, jnp.float32)
                elif d_min >= MAX_DISTANCE:
                    tile = jnp.full((quad, quad), rb_ref[NUM_BUCKETS - 1, hd], jnp.float32)
                else:
                    dist = d0 + ql - kl
                    bucket = _rel_bucket(jnp.maximum(dist, 0))
                    tile = jnp.zeros((quad, quad), jnp.float32)
                    for bi in range(NUM_BUCKETS):
                        tile = jnp.where(bucket == bi, rb_ref[bi, hd], tile)
                    if d_min < 0:
                        tile = jnp.where(dist >= 0, tile, _MASKED)
                o_ref[0, ti, rk * quad:(rk + 1) * quad, cq * quad:(cq + 1) * quad] = tile * LOG2E


def _bias_call(rel_bias, cast_stacks=(), cast_layer=0):
    cast_in, cast_out, cast_shape = _cast_rider(cast_stacks, cast_layer, N_HEADS, lambda h: h)
    return pl.pallas_call(
        functools.partial(_bias_kernel, n_cast=len(cast_stacks)),
        grid=(N_HEADS,),
        in_specs=[pl.BlockSpec(memory_space=pltpu.SMEM)] + cast_in,
        out_specs=[pl.BlockSpec((1, 2, MOBA_BLOCK, MOBA_BLOCK), lambda h: (h, 0, 0, 0))] + cast_out,
        out_shape=[jax.ShapeDtypeStruct((N_HEADS, 2, MOBA_BLOCK, MOBA_BLOCK), jnp.float32)]
        + cast_shape,
        compiler_params=pltpu.CompilerParams(dimension_semantics=("arbitrary",)),
        name="rel_bias_tiles",
    )(rel_bias, *cast_stacks)


def _sublane_all(op, x):
    shift = SUBLANES // 2
    while shift:
        x = op(x, pltpu.roll(x, shift, axis=0))
        shift //= 2
    return x


def _attn_kernel(rb_ref, x_ref, g_ref, wqt_ref, wo_ref, k_ref, vt_ref, km_ref, bias_ref,
                 o_ref, qh_ref, mask_ref, m_ref, l_ref, acc_ref):
    first_own = pl.program_id(1) * Q_SUBTILES
    nb = km_ref.shape[1]
    tq = MOBA_BLOCK
    D = x_ref.shape[2]
    row_groups = MOBA_BLOCK // SUBLANES
    head_groups = HEAD_DIM // SUBLANES

    def reset_softmax_state():
        m_ref[...] = jnp.full(m_ref.shape, _MASKED, jnp.float32)
        l_ref[...] = jnp.zeros(l_ref.shape, jnp.float32)
        acc_ref[...] = jnp.zeros(acc_ref.shape, jnp.float32)

    @pl.when((pl.program_id(0) == 0) & (pl.program_id(1) == 0))
    def _():
        reset_softmax_state()

    x = x_ref[0]
    h = _rms(x, g_ref[...]).astype(jnp.bfloat16)
    qf = lax.dot_general(wqt_ref[...], h, _NT, preferred_element_type=jnp.float32)
    q = (qf * (HEAD_DIM ** -0.5 * LOG2E)).astype(jnp.bfloat16)

    feat = lax.broadcasted_iota(jnp.int32, (LANES, 1), 0)
    for sub in range(Q_SUBTILES):
        for head in range(N_HEADS):
            p, hh = divmod(head, HEADS_PER_LANE_TILE)
            qp = q[p * LANES:(p + 1) * LANES, sub * tq:(sub + 1) * tq]
            in_head = (feat >= hh * HEAD_DIM) & (feat < (hh + 1) * HEAD_DIM)
            qh_ref[sub, head] = jnp.where(in_head, qp, jnp.zeros_like(qp))

    n_q = Q_SUBTILES * tq
    n_cand = nb - 1
    own_of_query = first_own + lax.broadcasted_iota(jnp.int32, (N_HEADS, n_q), 1) // tq
    past = [own_of_query > j for j in range(n_cand)]
    some_query_ranks = first_own + Q_SUBTILES - 1 > MOBA_TOPK

    @pl.when(some_query_ranks)
    def _():
        km = jnp.broadcast_to(km_ref[0, :n_cand][:, None, :], (n_cand, N_HEADS, D))
        km = km.reshape(n_cand * N_HEADS, D)
        row_head = lax.broadcasted_iota(jnp.int32, km.shape, 0) % N_HEADS
        col_head = lax.broadcasted_iota(jnp.int32, km.shape, 1) // HEAD_DIM
        km_bd = jnp.where(row_head == col_head, km, 0.0).astype(jnp.bfloat16)
        q_gate = (qf * HEAD_DIM ** -0.5).astype(jnp.bfloat16)
        gate = jnp.dot(km_bd, q_gate, preferred_element_type=jnp.float32)
        gates = [jnp.where(past[j], gate[j * N_HEADS:(j + 1) * N_HEADS], _NEG_INF)
                 for j in range(n_cand)]
        rank = [jnp.zeros((N_HEADS, n_q), jnp.int32) for _ in range(n_cand)]
        for lo in range(n_cand):
            for hi in range(lo + 1, n_cand):
                lo_wins = jnp.where(gates[lo] >= gates[hi], 1, 0)
                rank[hi] = rank[hi] + lo_wins
                rank[lo] = rank[lo] + (1 - lo_wins)
        for j in range(n_cand):
            keep = jnp.where(past[j], 0.0, _MASKED)
            mask_ref[j] = jnp.where(rank[j] < MOBA_TOPK, keep, _MASKED)

    @pl.when(jnp.logical_not(some_query_ranks))
    def _():
        for j in range(n_cand):
            mask_ref[j] = jnp.where(past[j], 0.0, _MASKED)


    def attend(blocks):
        units = []
        for j, kinds in blocks:
            keys = pl.ds(pl.multiple_of(j * MOBA_BLOCK, MOBA_BLOCK), MOBA_BLOCK)
            units += [(j, keys, kind, sub, head) for head in range(N_HEADS)
                      for sub, kind in enumerate(kinds) if kind is not None]

        def scores(unit):
            _, keys, _, sub, head = unit
            p = head // HEADS_PER_LANE_TILE
            kj = k_ref[0, keys, p * LANES:(p + 1) * LANES]
            return jnp.dot(kj, qh_ref[sub, head], preferred_element_type=jnp.float32)

        def softmax(unit, s):
            j, _, kind, sub, head = unit
            if kind != "far":
                tile = bias_ref[head, 0 if kind == "own" else 1]
                s = (s.reshape(row_groups, SUBLANES, tq)
                     + tile.reshape(row_groups, SUBLANES, tq)).reshape(MOBA_BLOCK, tq)
            sb = s.astype(jnp.bfloat16).reshape(MOBA_BLOCK // PACKED_ROWS, PACKED_ROWS, tq)
            bm = jnp.max(sb, axis=0).astype(jnp.float32)
            block_max = _sublane_all(jnp.maximum, jnp.maximum(bm[:SUBLANES], bm[SUBLANES:]))
            m_old = m_ref[sub, head]
            bias = rb_ref[NUM_BUCKETS - 1, head] * LOG2E if kind == "far" else 0.0
            top = block_max + bias
            if kind != "own":
                selected = jnp.broadcast_to(
                    mask_ref[j, head:head + 1, sub * tq:(sub + 1) * tq], (SUBLANES, tq)) == 0.0
                top = jnp.where(selected, top, _MASKED)
            shift = jnp.maximum(jnp.maximum(m_old, top) - bias, block_max)
            shift = shift.astype(jnp.bfloat16).astype(jnp.float32)
            shift_b = jnp.concatenate([shift, shift], axis=0).astype(jnp.bfloat16)
            pj = jnp.exp2(sb - shift_b[None])
            ref = shift + bias
            if kind != "own":
                m_new = jnp.where(selected, jnp.maximum(m_old, ref), m_old)
                scale_new = jnp.where(selected, jnp.exp2(ref - m_new), 0.0)
            else:
                m_new = jnp.maximum(m_old, ref)
                scale_new = jnp.exp2(ref - m_new)
            scale_old = jnp.exp2(m_old - m_new)
            m_ref[sub, head] = m_new
            return pj.reshape(MOBA_BLOCK, tq), (scale_old, scale_new)

        def accumulate(unit, pv, scales):
            _, _, _, sub, head = unit
            scale_old, scale_new = scales
            hs = slice(head * HEAD_DIM, (head + 1) * HEAD_DIM)
            l_ref[sub, head] = (l_ref[sub, head] * scale_old
                                + pv[HEAD_DIM:HEAD_DIM + SUBLANES] * scale_new)
            acc = acc_ref[sub, hs, :].reshape(head_groups, SUBLANES, tq)
            acc = (acc * scale_old[None]
                   + pv[:HEAD_DIM].reshape(head_groups, SUBLANES, tq) * scale_new[None])
            acc_ref[sub, hs, :] = acc.reshape(HEAD_DIM, tq)

        ones = jnp.ones((ONES_ROWS, MOBA_BLOCK), jnp.bfloat16)

        def weighted_values(unit, pj):
            _, keys, _, _, head = unit
            vj = vt_ref[0, head * HEAD_DIM:(head + 1) * HEAD_DIM, keys]
            return jnp.dot(jnp.concatenate([vj, ones], axis=0), pj,
                           preferred_element_type=jnp.float32)

        n = len(units)
        pending_scores = {i: scores(units[i]) for i in range(SCORE_LOOKAHEAD)}
        probs = {}
        products = {}
        for step in range(n + 2):
            if step + SCORE_LOOKAHEAD < n:
                pending_scores[step + SCORE_LOOKAHEAD] = scores(units[step + SCORE_LOOKAHEAD])
            if 0 <= step - 1 < n:
                pj, alpha = probs.pop(step - 1)
                products[step - 1] = (weighted_values(units[step - 1], pj), alpha)
            if step < n:
                probs[step] = softmax(units[step], pending_scores.pop(step))
            if 0 <= step - 2 < n:
                accumulate(units[step - 2], *products.pop(step - 2))

    assert Q_SUBTILES == 2 and FAR_UNROLL == 2
    all_far = ("far", "far")
    tail = [(first_own, ("own", "prev")), (first_own + 1, (None, "own"))]

    def far_pair(t, carry):
        attend([(FAR_UNROLL * t, all_far), (FAR_UNROLL * t + 1, all_far)])
        return carry

    lax.fori_loop(0, jnp.maximum(first_own - 1, 0) // FAR_UNROLL, far_pair, 0)

    @pl.when(first_own > 0)
    def _():
        attend([(first_own - 2, all_far), (first_own - 1, ("prev", "far"))] + tail)

    @pl.when(first_own == 0)
    def _():
        attend(tail)

    outs = []
    for sub in range(Q_SUBTILES):
        for head in range(N_HEADS):
            hs = slice(head * HEAD_DIM, (head + 1) * HEAD_DIM)
            inv = 1.0 / l_ref[sub, head]
            acc = acc_ref[sub, hs, :].reshape(head_groups, SUBLANES, tq) * inv[None]
            acc_ref[sub, hs, :] = acc.reshape(HEAD_DIM, tq)
        outs.append(acc_ref[sub].T.astype(jnp.bfloat16))
    o = jnp.concatenate(outs, axis=0)
    o_ref[0] = x + jnp.dot(o, wo_ref[...], preferred_element_type=jnp.float32)
    reset_softmax_state()


def _attn_call(x, g, wqt, wo, layer, k, vt, km, bias_tiles, rel_bias):
    B, T, D = x.shape
    nb = T // MOBA_BLOCK
    tq = MOBA_BLOCK
    n_q = Q_SUBTILES * tq
    return pl.pallas_call(
        _attn_kernel,
        grid=(B, T // n_q),
        in_specs=[
            pl.BlockSpec(memory_space=pltpu.SMEM),
            pl.BlockSpec((1, n_q, D), lambda b, i: (b, i, 0)),
            _resident((1, D)),
            _resident_layer(wqt.shape, layer),
            _resident_layer(wo.shape, layer),
            pl.BlockSpec((1, T, D), lambda b, i: (b, 0, 0)),
            pl.BlockSpec((1, D, T), lambda b, i: (b, 0, 0)),
            pl.BlockSpec((1, nb, D), lambda b, i: (b, 0, 0)),
            _resident(bias_tiles.shape),
        ],
        out_specs=pl.BlockSpec((1, n_q, D), lambda b, i: (b, i, 0)),
        out_shape=jax.ShapeDtypeStruct(x.shape, x.dtype),
        scratch_shapes=[
            pltpu.VMEM((Q_SUBTILES, N_HEADS, LANES, tq), jnp.bfloat16),
            pltpu.VMEM((nb - 1, N_HEADS, n_q), jnp.float32),
            pltpu.VMEM((Q_SUBTILES, N_HEADS, SUBLANES, tq), jnp.float32),
            pltpu.VMEM((Q_SUBTILES, N_HEADS, SUBLANES, tq), jnp.float32),
            pltpu.VMEM((Q_SUBTILES, D, tq), jnp.float32),
        ],
        compiler_params=pltpu.CompilerParams(
            dimension_semantics=("arbitrary", "arbitrary"),
            vmem_limit_bytes=VMEM_LIMIT),
        name="moba_attention",
    )(rel_bias, x, g.reshape(1, D), wqt, wo, k, vt, km, bias_tiles)


def kernel(x, norm_mixer, norm_ffn, pool_w, pool_scale, kv_norm, w_kv, w_q, w_o,
           rel_bias, w_gate_up, w_down, final_norm):
    bf = jnp.bfloat16
    assert x.shape[1] % MOBA_BLOCK == 0 and x.shape[2] == D_MODEL
    w_qt = jnp.swapaxes(w_q, 1, 2).astype(bf)
    wk = w_kv[:, :D_MODEL].astype(bf)
    wvt = w_kv[:, D_MODEL:].T.astype(bf)

    ffn_stacks = (w_gate_up, w_down)
    pool_w_flat = pool_w.reshape(1, -1, pool_w.shape[-1])
    w_o_flat = w_o.reshape(1, -1, w_o.shape[-1])

    bias_tiles, wgu, wd, pool_w_bf = _bias_call(rel_bias, ffn_stacks + (pool_w_flat,), 0)
    pool_w = pool_w_bf.reshape(pool_w.shape)

    k = vt = km = None
    for layer in range(DEPTH):
        last = layer == DEPTH - 1
        extra = (w_o_flat,) if layer == 0 else ()
        rider = dict(cast_stacks=(() if last else ffn_stacks) + extra,
                     cast_layer=(layer + 1,) * len(ffn_stacks) + (0,) * len(extra))
        if layer < N_A_LAYERS:
            x, *cast = _pool_ffn_call(x, norm_mixer[layer], pool_w, layer, pool_scale[layer],
                                      norm_ffn[layer], wgu, wd, **rider)
        else:
            j = layer - N_A_LAYERS
            if j == 0:
                k, vt, km = _kv_call(x, kv_norm, wk, wvt)
            x = _attn_call(x, norm_mixer[layer], w_qt, w_o, j, k, vt, km, bias_tiles, rel_bias)
            x, *cast = _ffn_call(x, norm_ffn[layer], wgu, wd, final_norm, final_norm=last,
                                 **rider)
        if cast:
            wgu, wd, *more = cast
            if more:
                w_o = more[0].reshape(w_o.shape)
    return x
```
